```python
import math
import jax, jax.numpy as jnp
from jax import lax
import numpy as np

D_MODEL = 1024
BATCH = 8
SEQ = 2048
DEPTH = 4
DEC_BATCH = 128
DEC_SEQ = 4
PAST_LEN = 16384
PAGE_SIZE = 128

N_MIXERS = 2
N_MAMBA = (DEPTH + 1) // 2
N_RWKV = DEPTH // 2
N_VRES = max(N_RWKV - 1, 0)

D_FF = 2816
PLE_DIM = 256
NORM_EPS = 1e-6

M_D_INNER = 2 * D_MODEL
M_HEADDIM = 64
M_HEADS = M_D_INNER // M_HEADDIM
M_GROUPS = 8
M_HEADS_PER_GROUP = M_HEADS // M_GROUPS
M_STATE = 128
M_CONV_W = 4
M_CONV_DIM = M_D_INNER + 2 * M_GROUPS * M_STATE
M_IN_DIM = 2 * M_D_INNER + 2 * M_GROUPS * M_STATE + M_HEADS
M_CHUNK = 128
M_NORM_EPS = 1e-5

R_HEADDIM = 64
R_HEADS = D_MODEL // R_HEADDIM
R_DECAY_LORA = 64
R_AAA_LORA = 64
R_MV_LORA = 32
R_GATE_LORA = 160
R_GN_EPS = 64e-5

kernel_name = 'mamba2_rwkv7_macaron_ple_step'

RWKV_KEYS = ('r_mu', 'r_wr', 'r_wk', 'r_wv', 'r_wo', 'r_w0', 'r_w1', 'r_w2', 'r_a0', 'r_a1', 'r_a2',
             'r_g1', 'r_g2', 'r_k_k', 'r_k_a', 'r_r_k', 'r_gn_w', 'r_gn_b')


def rmsnorm(x, g, eps=NORM_EPS):
    xf = x.astype(jnp.float32)
    ms = jnp.mean(xf * xf, axis=-1, keepdims=True)
    return (xf * lax.rsqrt(ms + eps)).astype(x.dtype) * g


def swiglu(u, w_gate_up, w_down):
    gate, up = jnp.split(u @ w_gate_up, 2, axis=-1)
    return (jax.nn.silu(gate) * up) @ w_down


def causal_depthwise_conv(x, buf, w, b):
    L = x.shape[1]
    full = jnp.concatenate([buf.astype(x.dtype), x], axis=1)
    out = full[:, 0:L] * w[0]
    for k in range(1, M_CONV_W):
        out = out + full[:, k:k + L] * w[k]
    return out + b, full[:, L:]


def ssd_scan(x, dt, A, B, C, h0):
    bsz, L = x.shape[0], x.shape[1]
    Q = min(M_CHUNK, L)
    nc = -(-L // Q)
    pad = nc * Q - L
    if pad:
        padf = lambda t: jnp.pad(t, [(0, 0), (0, pad)] + [(0, 0)] * (t.ndim - 2))
        x, dt, B, C = padf(x), padf(dt), padf(B), padf(C)
    G, R = M_GROUPS, M_HEADS_PER_GROUP
    x = x.reshape(bsz, nc, Q, G, R, M_HEADDIM)
    dt = dt.reshape(bsz, nc, Q, G, R)
    B = B.reshape(bsz, nc, Q, G, M_STATE)
    C = C.reshape(bsz, nc, Q, G, M_STATE)
    a_cum = jnp.cumsum(dt * A.reshape(G, R), axis=2)
    seg = a_cum[:, :, :, None] - a_cum[:, :, None, :]
    causal = jnp.tril(jnp.ones((Q, Q), bool))[:, :, None, None]
    decay_ij = jnp.exp(jnp.where(causal, seg, -jnp.inf))
    xdt = x * dt[..., None]
    cb = jnp.einsum('bcign,bcjgn->bcijg', C, B)
    y_intra = jnp.einsum('bcijg,bcijgr,bcjgrp->bcigrp', cb, decay_ij, xdt)
    decay_end = jnp.exp(a_cum[:, :, -1:] - a_cum)
    chunk_states = jnp.einsum('bcjgn,bcjgr,bcjgrp->bcgrpn', B, decay_end, xdt)
    chunk_decay = jnp.exp(a_cum[:, :, -1])

    def step(h, inp):
        s, d = inp
        return h * d[..., None, None] + s, h

    h_last, h_starts = lax.scan(step, h0.reshape(bsz, G, R, M_HEADDIM, M_STATE),
                                (jnp.moveaxis(chunk_states, 1, 0), jnp.moveaxis(chunk_decay, 1, 0)))
    h_starts = jnp.moveaxis(h_starts, 0, 1)
    y_inter = jnp.einsum('bcign,bcigr,bcgrpn->bcigrp', C, jnp.exp(a_cum), h_starts)
    y = (y_intra + y_inter).reshape(bsz, nc * Q, M_HEADS, M_HEADDIM)[:, :L]
    return y, h_last.reshape(bsz, M_HEADS, M_HEADDIM, M_STATE)


def mamba2_mixer(u, h0, conv_buf, w_in, conv_w, conv_b, dt_bias, A_log, D_skip, norm_w, w_out):
    f32 = jnp.float32
    bsz, L, _ = u.shape
    z, xbc, dt = jnp.split(u @ w_in, [M_D_INNER, M_D_INNER + M_CONV_DIM], axis=-1)
    xbc, new_buf = causal_depthwise_conv(xbc, conv_buf, conv_w, conv_b)
    xbc = jax.nn.silu(xbc)
    xs, Bm, Cm = jnp.split(xbc, [M_D_INNER, M_D_INNER + M_GROUPS * M_STATE], axis=-1)
    dt = jax.nn.softplus((dt + dt_bias).astype(f32))
    A = -jnp.exp(A_log.astype(f32))
    xh = xs.reshape(bsz, L, M_HEADS, M_HEADDIM).astype(f32)
    y, h_new = ssd_scan(xh, dt, A,
                        Bm.reshape(bsz, L, M_GROUPS, M_STATE).astype(f32),
                        Cm.reshape(bsz, L, M_GROUPS, M_STATE).astype(f32),
                        h0.astype(f32))
    y = y + D_skip.astype(f32)[:, None] * xh
    y = y.reshape(bsz, L, M_D_INNER) * jax.nn.silu(z.astype(f32))
    yg = y.reshape(bsz, L, M_GROUPS, M_D_INNER // M_GROUPS)
    yg = yg * lax.rsqrt(jnp.mean(yg * yg, axis=-1, keepdims=True) + M_NORM_EPS)
    y = yg.reshape(bsz, L, M_D_INNER).astype(u.dtype) * norm_w
    return y @ w_out, h_new.astype(h0.dtype), new_buf.astype(conv_buf.dtype)


def rwkv7_mixer(u, S0, shift0, v_first, vres, mu, w_r, w_k, w_v, w_o, w0, w1, w2, a0, a1, a2,
                g1, g2, k_k, k_a, r_k, gn_w, gn_b):
    f32 = jnp.float32
    bsz, L, _ = u.shape
    u_prev = jnp.concatenate([shift0[:, None].astype(u.dtype), u[:, :-1]], axis=1)
    xx = u_prev - u
    mixed = u[None] + xx[None] * mu[:, None, None, :]
    xr, xw, xk, xv, xa, xg = mixed[0], mixed[1], mixed[2], mixed[3], mixed[4], mixed[5]
    r = xr @ w_r
    k = xk @ w_k
    v = xv @ w_v
    w = -jax.nn.softplus(-(w0 + jnp.tanh(xw @ w1) @ w2)) - 0.5
    a = jax.nn.sigmoid(a0 + (xa @ a1) @ a2)
    g = jax.nn.sigmoid(xg @ g1) @ g2
    if vres is None:
        v_first = v
    else:
        v0, v1, v2 = vres
        v = v + (v_first - v) * jax.nn.sigmoid(v0 + (xv @ v1) @ v2)
    heads = lambda t: t.reshape(bsz, L, R_HEADS, R_HEADDIM).astype(f32)
    kk = heads(k * k_k)
    kk = kk * lax.rsqrt(jnp.maximum(jnp.sum(kk * kk, axis=-1, keepdims=True), 1e-24))
    k = k * (1 + (a - 1) * k_a)
    rh, kh, vh, ah = heads(r), heads(k), heads(v), heads(a)
    decay = jnp.exp(-jnp.exp(heads(w)))

    def step(S, inp):
        r_t, d_t, k_t, v_t, kk_t, a_t = inp
        sa = jnp.einsum('bhij,bhj->bhi', S, -kk_t)
        S = (S * d_t[:, :, None, :] + sa[..., None] * (kk_t * a_t)[:, :, None, :]
             + v_t[..., None] * k_t[:, :, None, :])
        return S, jnp.einsum('bhij,bhj->bhi', S, r_t)

    tm = lambda t: jnp.moveaxis(t, 1, 0)
    S_new, y = lax.scan(step, S0.astype(f32), (tm(rh), tm(decay), tm(kh), tm(vh), tm(kk), tm(ah)))
    y = jnp.moveaxis(y, 0, 1)
    mean = jnp.mean(y, axis=-1, keepdims=True)
    var = jnp.mean(jnp.square(y - mean), axis=-1, keepdims=True)
    y = ((y - mean) * lax.rsqrt(var + R_GN_EPS)).reshape(bsz, L, D_MODEL) * gn_w + gn_b
    bonus = jnp.sum(rh * kh * r_k, axis=-1, keepdims=True) * vh
    y = y + bonus.reshape(bsz, L, D_MODEL)
    out = (y.astype(u.dtype) * g) @ w_o
    return out, S_new.astype(S0.dtype), u[:, -1].astype(shift0.dtype), v_first


def trunk(x, p, ssm0, conv0, wkv0, shift0, prm):
    h = x
    v_first = None
    ssm_new, conv_new, wkv_new, shift_new = [], [], [], []
    for i in range(DEPTH):
        j = i // N_MIXERS
        h = h + 0.5 * swiglu(rmsnorm(h, prm['norm_ffn1'][i]), prm['ffn1_gate_up'][i], prm['ffn1_down'][i])
        u = rmsnorm(h, prm['norm_mix'][i])
        if i % N_MIXERS == 0:
            mix, s_new, c_new = mamba2_mixer(u, ssm0[j], conv0[j], prm['m_in_proj'][j], prm['m_conv_w'][j],
                                             prm['m_conv_b'][j], prm['m_dt_bias'][j], prm['m_A_log'][j],
                                             prm['m_D'][j], prm['m_norm'][j], prm['m_out_proj'][j])
            ssm_new.append(s_new)
            conv_new.append(c_new)
        else:
            vres = None if j == 0 else (prm['r_v0'][j - 1], prm['r_v1'][j - 1], prm['r_v2'][j - 1])
            mix, s_new, sh_new, v_first = rwkv7_mixer(u, wkv0[j], shift0[j], v_first, vres,
                                                      *[prm[n][j] for n in RWKV_KEYS])
            wkv_new.append(s_new)
            shift_new.append(sh_new)
        h = h + mix
        h = h + 0.5 * swiglu(rmsnorm(h, prm['norm_ffn2'][i]), prm['ffn2_gate_up'][i], prm['ffn2_down'][i])
        gate = jax.nn.sigmoid(rmsnorm(h, prm['norm_ple'][i]) @ prm['ple_gate'][i])
        h = h + (p[i] @ prm['ple_in'][i]) * gate
    y = rmsnorm(h, prm['norm_final'])
    return y, jnp.stack(ssm_new), jnp.stack(conv_new), jnp.stack(wkv_new), jnp.stack(shift_new)


def setup_inputs(seed: int = 0) -> dict:
    key = jax.random.key(seed)
    ks = iter(jax.random.split(key, 64))
    f32 = jnp.float32

    def nrm(shape, scale=1.0):
        return jax.random.normal(next(ks), shape, f32) * scale

    def gain(shape):
        return 1.0 + nrm(shape, 0.02)

    def unif(shape, lo, hi):
        return jax.random.uniform(next(ks), shape, f32, lo, hi)

    NM, NR, NV = N_MAMBA, N_RWKV, N_VRES
    D = D_MODEL
    return {
        'x_prompt': nrm((BATCH, SEQ, D)),
        'x_sample': nrm((DEC_BATCH, DEC_SEQ, D)),
        'p_prompt': nrm((DEPTH, BATCH, SEQ, PLE_DIM)),
        'p_sample': nrm((DEPTH, DEC_BATCH, DEC_SEQ, PLE_DIM)),
        'state_ssm': nrm((NM, DEC_BATCH, M_HEADS, M_HEADDIM, M_STATE), 0.1),
        'state_conv': nrm((NM, DEC_BATCH, M_CONV_W - 1, M_CONV_DIM)),
        'state_wkv': nrm((NR, DEC_BATCH, R_HEADS, R_HEADDIM, R_HEADDIM), 0.1),
        'state_shift': nrm((NR, DEC_BATCH, D)),
        'norm_ffn1': gain((DEPTH, D)),
        'ffn1_gate_up': nrm((DEPTH, D, 2 * D_FF), D ** -0.5),
        'ffn1_down': nrm((DEPTH, D_FF, D), D_FF ** -0.5),
        'norm_mix': gain((DEPTH, D)),
        'norm_ffn2': gain((DEPTH, D)),
        'ffn2_gate_up': nrm((DEPTH, D, 2 * D_FF), D ** -0.5),
        'ffn2_down': nrm((DEPTH, D_FF, D), D_FF ** -0.5),
        'norm_ple': gain((DEPTH, D)),
        'ple_in': nrm((DEPTH, PLE_DIM, D), PLE_DIM ** -0.5),
        'ple_gate': nrm((DEPTH, D, D), D ** -0.5),
        'norm_final': gain((D,)),
        'm_in_proj': nrm((NM, D, M_IN_DIM), D ** -0.5),
        'm_conv_w': nrm((NM, M_CONV_W, M_CONV_DIM), M_CONV_W ** -0.5),
        'm_conv_b': nrm((NM, M_CONV_DIM), 0.02),
        'm_dt_bias': (lambda dt0: dt0 + jnp.log(-jnp.expm1(-dt0)))(
            jnp.exp(unif((NM, M_HEADS), math.log(1e-3), math.log(1e-1)))),
        'm_A_log': jnp.log(unif((NM, M_HEADS), 1.0, 16.0)),
        'm_D': gain((NM, M_HEADS)),
        'm_norm': gain((NM, M_D_INNER)),
        'm_out_proj': nrm((NM, M_D_INNER, D), M_D_INNER ** -0.5),
        'r_mu': unif((NR, 6, D), 0.0, 1.0),
        'r_wr': nrm((NR, D, D), D ** -0.5),
        'r_wk': nrm((NR, D, D), D ** -0.5),
        'r_wv': nrm((NR, D, D), D ** -0.5),
        'r_wo': nrm((NR, D, D), D ** -0.5),
        'r_w0': unif((NR, D), -6.0, -1.0),
        'r_w1': nrm((NR, D, R_DECAY_LORA), D ** -0.5),
        'r_w2': nrm((NR, R_DECAY_LORA, D), 0.1 * R_DECAY_LORA ** -0.5),
        'r_a0': nrm((NR, D), 0.1),
        'r_a1': nrm((NR, D, R_AAA_LORA), D ** -0.5),
        'r_a2': nrm((NR, R_AAA_LORA, D), 0.1 * R_AAA_LORA ** -0.5),
        'r_g1': nrm((NR, D, R_GATE_LORA), D ** -0.5),
        'r_g2': nrm((NR, R_GATE_LORA, D), R_GATE_LORA ** -0.5),
        'r_k_k': 0.85 + nrm((NR, D), 0.02),
        'r_k_a': gain((NR, D)),
        'r_r_k': nrm((NR, R_HEADS, R_HEADDIM), 0.1),
        'r_gn_w': gain((NR, D)),
        'r_gn_b': nrm((NR, D), 0.02),
        'r_v0': gain((NV, D)),
        'r_v1': nrm((NV, D, R_MV_LORA), D ** -0.5),
        'r_v2': nrm((NV, R_MV_LORA, D), 0.1 * R_MV_LORA ** -0.5),
    }


def reference(x_prompt, x_sample, p_prompt, p_sample, state_ssm, state_conv, state_wkv, state_shift,
              norm_ffn1, ffn1_gate_up, ffn1_down, norm_mix, norm_ffn2, ffn2_gate_up, ffn2_down,
              norm_ple, ple_in, ple_gate, norm_final,
              m_in_proj, m_conv_w, m_conv_b, m_dt_bias, m_A_log, m_D, m_norm, m_out_proj,
              r_mu, r_wr, r_wk, r_wv, r_wo, r_w0, r_w1, r_w2, r_a0, r_a1, r_a2, r_g1, r_g2,
              r_k_k, r_k_a, r_r_k, r_gn_w, r_gn_b, r_v0, r_v1, r_v2):
    prm = dict(norm_ffn1=norm_ffn1, ffn1_gate_up=ffn1_gate_up, ffn1_down=ffn1_down, norm_mix=norm_mix,
               norm_ffn2=norm_ffn2, ffn2_gate_up=ffn2_gate_up, ffn2_down=ffn2_down,
               norm_ple=norm_ple, ple_in=ple_in, ple_gate=ple_gate, norm_final=norm_final,
               m_in_proj=m_in_proj, m_conv_w=m_conv_w, m_conv_b=m_conv_b, m_dt_bias=m_dt_bias,
               m_A_log=m_A_log, m_D=m_D, m_norm=m_norm, m_out_proj=m_out_proj,
               r_mu=r_mu, r_wr=r_wr, r_wk=r_wk, r_wv=r_wv, r_wo=r_wo, r_w0=r_w0, r_w1=r_w1, r_w2=r_w2,
               r_a0=r_a0, r_a1=r_a1, r_a2=r_a2, r_g1=r_g1, r_g2=r_g2, r_k_k=r_k_k, r_k_a=r_k_a,
               r_r_k=r_r_k, r_gn_w=r_gn_w, r_gn_b=r_gn_b, r_v0=r_v0, r_v1=r_v1, r_v2=r_v2)
    bp = x_prompt.shape[0]
    dtp = x_prompt.dtype
    ssm0 = jnp.zeros((state_ssm.shape[0], bp) + state_ssm.shape[2:], dtp)
    conv0 = jnp.zeros((state_conv.shape[0], bp) + state_conv.shape[2:], dtp)
    wkv0 = jnp.zeros((state_wkv.shape[0], bp) + state_wkv.shape[2:], dtp)
    shift0 = jnp.zeros((state_shift.shape[0], bp) + state_shift.shape[2:], dtp)
    y_prompt, ssm_p, conv_p, wkv_p, shift_p = trunk(x_prompt, p_prompt, ssm0, conv0, wkv0, shift0, prm)
    y_sample, ssm_s, conv_s, wkv_s, shift_s = trunk(x_sample, p_sample, state_ssm, state_conv,
                                                    state_wkv, state_shift, prm)
    return (y_prompt, y_sample, ssm_p, conv_p, wkv_p, shift_p, ssm_s, conv_s, wkv_s, shift_s)
```

```python
import functools
import math

import jax
import jax.numpy as jnp
from jax import lax
from jax.experimental import pallas as pl
from jax.experimental.pallas import tpu as pltpu

F32 = jnp.float32
BF16 = jnp.bfloat16

NORM_EPS = 1e-6
M_NORM_EPS = 1e-5
R_GN_EPS = 64e-5

LANES = 128
M_HEADDIM = 64
M_STATE = 128
M_GROUPS = 8
M_CHUNK = 128
M_CONV_W = 4
R_HEADDIM = 64
R_CHUNK = 64
SAMPLE_SLOT_ROWS = 16
VMEM_LIMIT_BYTES = 56 * 1024 * 1024


def _cparams(n_axes):
    return pltpu.CompilerParams(dimension_semantics=("arbitrary",) * n_axes,
                                vmem_limit_bytes=VMEM_LIMIT_BYTES)


def _resident(shape):
    zeros = (0,) * len(shape)
    return pl.BlockSpec(shape, lambda *_: zeros, pipeline_mode=pl.Buffered(1))


def _dot(a, b):
    return jnp.dot(a.astype(BF16), b.astype(BF16), preferred_element_type=F32)


def _dot_nt(a, b):
    return lax.dot_general(a.astype(BF16), b.astype(BF16), (((1,), (1,)), ((), ())),
                           preferred_element_type=F32)


def _split3(x):
    hi = x.astype(BF16)
    r1 = x - hi.astype(F32)
    mid = r1.astype(BF16)
    lo = (r1 - mid.astype(F32)).astype(BF16)
    return hi, mid, lo


def _dot_sel_l(sel, x):
    hi, mid, lo = _split3(x)
    s = sel.astype(BF16)
    return (jnp.dot(s, hi, preferred_element_type=F32) + jnp.dot(s, mid, preferred_element_type=F32)
            + jnp.dot(s, lo, preferred_element_type=F32))


def _dot_sel_r(x, sel):
    hi, mid, lo = _split3(x)
    s = sel.astype(BF16)
    return (jnp.dot(hi, s, preferred_element_type=F32) + jnp.dot(mid, s, preferred_element_type=F32)
            + jnp.dot(lo, s, preferred_element_type=F32))


def _dot3(a, b):
    a_hi = a.astype(BF16)
    a_lo = (a - a_hi.astype(F32)).astype(BF16)
    b_hi = b.astype(BF16)
    b_lo = (b - b_hi.astype(F32)).astype(BF16)
    return (jnp.dot(a_hi, b_hi, preferred_element_type=F32) + jnp.dot(a_hi, b_lo, preferred_element_type=F32)
            + jnp.dot(a_lo, b_hi, preferred_element_type=F32))


def _rms(x, g):
    ms = jnp.mean(x * x, axis=-1, keepdims=True)
    return (x * lax.rsqrt(ms + NORM_EPS)) * g


def _silu(x):
    return x * jax.nn.sigmoid(x)


def _softplus(x):
    return jnp.maximum(x, 0.0) + jnp.log1p(jnp.exp(-jnp.abs(x)))


def _ffn_kernel(h_ref, g_ref, wg_ref, wu_ref, wd_ref, o_ref, ub_ref):
    x = h_ref[...]
    ub_ref[...] = _rms(x, g_ref[...]).astype(BF16)
    o_ref[...] = x

    def step(c, carry):
        ub = ub_ref[...]
        gate = jnp.dot(ub, wg_ref[c], preferred_element_type=F32)
        up = jnp.dot(ub, wu_ref[c], preferred_element_type=F32)
        act = (_silu(gate) * up).astype(BF16)
        o_ref[...] += 0.5 * jnp.dot(act, wd_ref[c], preferred_element_type=F32)
        return carry

    lax.fori_loop(0, wg_ref.shape[0], step, 0)


def _ffn(h, gain, w_gate_up, w_down, *, tm=512, fc=256):
    T, D = h.shape
    F = w_down.shape[0]
    nf = F // fc
    wg = w_gate_up[:, :F].astype(BF16).reshape(D, nf, fc).transpose(1, 0, 2)
    wu = w_gate_up[:, F:].astype(BF16).reshape(D, nf, fc).transpose(1, 0, 2)
    wd = w_down.astype(BF16).reshape(nf, fc, D)
    return pl.pallas_call(
        _ffn_kernel,
        out_shape=jax.ShapeDtypeStruct((T, D), F32),
        grid=(T // tm,),
        in_specs=[pl.BlockSpec((tm, D), lambda i: (i, 0)),
                  _resident((1, D)), _resident((nf, D, fc)), _resident((nf, D, fc)), _resident((nf, fc, D))],
        out_specs=pl.BlockSpec((tm, D), lambda i: (i, 0)),
        scratch_shapes=[pltpu.VMEM((tm, D), BF16)],
        compiler_params=_cparams(1),
        name="ffn",
    )(h, gain.reshape(1, D), wg, wu, wd)


def _ple_kernel(h_ref, p_ref, g_ref, wi_ref, wg_ref, gf_ref, o_ref, *, final):
    x = h_ref[...]
    gate = jax.nn.sigmoid(_dot(_rms(x, g_ref[...]), wg_ref[...]))
    y = x + _dot(p_ref[...], wi_ref[...]) * gate
    if final:
        y = _rms(y, gf_ref[...])
    o_ref[...] = y


def _ple(h, p, gain, w_in, w_gate, gain_final, *, final, tm=512):
    T, D = h.shape
    P = p.shape[1]
    return pl.pallas_call(
        functools.partial(_ple_kernel, final=final),
        out_shape=jax.ShapeDtypeStruct((T, D), F32),
        grid=(T // tm,),
        in_specs=[pl.BlockSpec((tm, D), lambda i: (i, 0)), pl.BlockSpec((tm, P), lambda i: (i, 0)),
                  _resident((1, D)), _resident((P, D)), _resident((D, D)), _resident((1, D))],
        out_specs=pl.BlockSpec((tm, D), lambda i: (i, 0)),
        compiler_params=_cparams(1),
        name="ple",
    )(h, p, gain.reshape(1, D), w_in.astype(BF16), w_gate.astype(BF16), gain_final.reshape(1, D))


def _out_proj_kernel(h_ref, y_ref, w_ref, o_ref):
    o_ref[...] = h_ref[...] + _dot(y_ref[...], w_ref[...])


def _out_proj(h, y, w, *, tm=512):
    T, D = h.shape
    K = y.shape[1]
    return pl.pallas_call(
        _out_proj_kernel,
        out_shape=jax.ShapeDtypeStruct((T, D), F32),
        grid=(T // tm,),
        in_specs=[pl.BlockSpec((tm, D), lambda i: (i, 0)), pl.BlockSpec((tm, K), lambda i: (i, 0)),
                  _resident((K, D))],
        out_specs=pl.BlockSpec((tm, D), lambda i: (i, 0)),
        compiler_params=_cparams(1),
        name="out_proj",
    )(h, y, w.astype(BF16))


def _m_in_kernel(h_ref, g_ref, wz_ref, wx_ref, wdt_ref, z_ref, xbc_ref, dt_ref):
    u = _rms(h_ref[...], g_ref[...]).astype(BF16)
    z_ref[...] = jnp.dot(u, wz_ref[...], preferred_element_type=F32)
    xbc_ref[...] = jnp.dot(u, wx_ref[...], preferred_element_type=F32)
    dt_ref[...] = jnp.dot(u, wdt_ref[...], preferred_element_type=F32)


def _m_in_proj(h, gain, w_in, d_inner, conv_dim, n_heads, *, tm=256):
    T, D = h.shape
    wz = w_in[:, :d_inner].astype(BF16)
    wx = w_in[:, d_inner:d_inner + conv_dim].astype(BF16)
    wdt = jnp.pad(w_in[:, d_inner + conv_dim:], ((0, 0), (0, LANES - n_heads))).astype(BF16)
    return pl.pallas_call(
        _m_in_kernel,
        out_shape=(jax.ShapeDtypeStruct((T, d_inner), F32), jax.ShapeDtypeStruct((T, conv_dim), F32),
                   jax.ShapeDtypeStruct((T, LANES), F32)),
        grid=(T // tm,),
        in_specs=[pl.BlockSpec((tm, D), lambda i: (i, 0)), _resident((1, D)),
                  _resident((D, d_inner)), _resident((D, conv_dim)), _resident((D, LANES))],
        out_specs=(pl.BlockSpec((tm, d_inner), lambda i: (i, 0)), pl.BlockSpec((tm, conv_dim), lambda i: (i, 0)),
                   pl.BlockSpec((tm, LANES), lambda i: (i, 0))),
        compiler_params=_cparams(1),
        name="m_in_proj",
    )(h, gain.reshape(1, D), wz, wx, wdt)


def _head_expanders(n_heads):
    hh = jnp.arange(LANES)[:, None]
    e64 = (hh == (jnp.arange(n_heads * M_HEADDIM)[None, :] // M_HEADDIM)).astype(BF16)
    e128 = (hh == (jnp.arange(n_heads * LANES)[None, :] // LANES)).astype(BF16)
    return e64, e128


def _ssd_prompt_kernel(z_ref, xbc_ref, dt_ref, cw_ref, cb_ref, dtb_ref, alog_ref, dsk_ref, nw_ref,
                       e64_ref, e128_ref, y_ref, ssm_ref, conv_ref, xpad_ref, s_ref, *, n_heads):
    c = pl.program_id(1)
    nc = pl.num_programs(1)
    Q = M_CHUNK
    d_inner = n_heads * M_HEADDIM
    gw = d_inner // M_GROUPS
    bc0 = d_inner
    cc0 = d_inner + M_GROUPS * M_STATE

    @pl.when(c == 0)
    def _():
        xpad_ref[0:8, :] = jnp.zeros((8, xpad_ref.shape[1]), F32)
        s_ref[...] = jnp.zeros(s_ref.shape, F32)

    xraw = xbc_ref[...]
    xpad_ref[8:8 + Q, :] = xraw
    xc = (xpad_ref[5:5 + Q, :] * cw_ref[0:1, :] + xpad_ref[6:6 + Q, :] * cw_ref[1:2, :]
          + xpad_ref[7:7 + Q, :] * cw_ref[2:3, :] + xraw * cw_ref[3:4, :] + cb_ref[...])
    xpad_ref[0:8, :] = xraw[Q - 8:Q, :]
    xs = _silu(xc)
    xh = xs[:, :d_inner]

    dt = _softplus(dt_ref[...] + dtb_ref[...])
    dta = dt * (-jnp.exp(alog_ref[...]))
    ri = lax.broadcasted_iota(jnp.int32, (Q, Q), 0)
    ci = lax.broadcasted_iota(jnp.int32, (Q, Q), 1)
    causal = ri >= ci
    acum = _dot_sel_l(causal, dta)
    a_full = _dot_sel_r(acum, e64_ref[...])
    dt_full = _dot_sel_r(dt, e64_ref[...])
    a_col = _dot_sel_r(acum, e128_ref[...])
    acum_t = acum.T
    a_last = a_full[Q - 1:Q, :]
    exp_a = jnp.exp(a_full)
    xdt = xh * dt_full
    xw = xdt * jnp.exp(a_last - a_full)
    chunk_decay = jnp.exp(a_last)
    lane = lax.broadcasted_iota(jnp.int32, (Q, LANES), 1)
    first_head = lane < M_HEADDIM
    hpg = n_heads // M_GROUPS

    for g in range(M_GROUPS):
        bg = xs[:, bc0 + g * M_STATE: bc0 + (g + 1) * M_STATE]
        cg = xs[:, cc0 + g * M_STATE: cc0 + (g + 1) * M_STATE]
        cb = _dot_nt(cg, bg)
        sl = slice(g * gw, (g + 1) * gw)
        s_g = s_ref[:, sl]
        y_g = _dot(cg, s_g) * exp_a[:, sl]
        parts = []
        for q in range(hpg // 2):
            h0 = g * hpg + 2 * q
            xp = xdt[:, h0 * M_HEADDIM: h0 * M_HEADDIM + LANES]
            ys = []
            for hh in (h0, h0 + 1):
                seg = a_col[:, hh * LANES:(hh + 1) * LANES] - acum_t[hh:hh + 1, :]
                m = cb * jnp.exp(jnp.where(causal, seg, -jnp.inf))
                ys.append(_dot(m, xp))
            parts.append(jnp.where(first_head, ys[0], ys[1]))
        y_g = y_g + jnp.concatenate(parts, axis=1) + dsk_ref[:, sl] * xh[:, sl]
        y_g = y_g * _silu(z_ref[:, sl])
        y_g = y_g * lax.rsqrt(jnp.mean(y_g * y_g, axis=-1, keepdims=True) + M_NORM_EPS)
        y_ref[:, sl] = y_g * nw_ref[:, sl]
        s_ref[:, sl] = s_g * chunk_decay[:, sl] + _dot(bg.T, xw[:, sl])

    @pl.when(c == nc - 1)
    def _():
        ssm_ref[0] = s_ref[...].T.reshape(n_heads, M_HEADDIM, M_STATE)
        conv_ref[0] = xraw[Q - (M_CONV_W - 1):Q, :]


def _ssd_prompt(z, xbc, dt, prm, B, L, n_heads):
    T, d_inner = z.shape
    conv_dim = xbc.shape[1]
    Q = M_CHUNK
    nc = L // Q
    row = lambda b, c: (b * nc + c, 0)
    kern = functools.partial(_ssd_prompt_kernel, n_heads=n_heads)
    return pl.pallas_call(
        kern,
        out_shape=(jax.ShapeDtypeStruct((T, d_inner), F32),
                   jax.ShapeDtypeStruct((B, n_heads, M_HEADDIM, M_STATE), F32),
                   jax.ShapeDtypeStruct((B, M_CONV_W - 1, conv_dim), F32)),
        grid=(B, nc),
        in_specs=[pl.BlockSpec((Q, d_inner), row), pl.BlockSpec((Q, conv_dim), row), pl.BlockSpec((Q, LANES), row),
                  _resident((M_CONV_W, conv_dim)), _resident((1, conv_dim)), _resident((1, LANES)),
                  _resident((1, LANES)), _resident((1, d_inner)), _resident((1, d_inner)),
                  _resident((LANES, d_inner)), _resident((LANES, n_heads * LANES))],
        out_specs=(pl.BlockSpec((Q, d_inner), row),
                   pl.BlockSpec((1, n_heads, M_HEADDIM, M_STATE), lambda b, c: (b, 0, 0, 0)),
                   pl.BlockSpec((1, M_CONV_W - 1, conv_dim), lambda b, c: (b, 0, 0))),
        scratch_shapes=[pltpu.VMEM((Q + 8, conv_dim), F32), pltpu.VMEM((M_STATE, d_inner), F32)],
        compiler_params=_cparams(2),
        name="ssd_prompt",
    )(z, xbc, dt, prm["conv_w"], prm["conv_b"], prm["dt_bias"], prm["a_log"], prm["d_full"], prm["norm_w"],
      prm["e64"], prm["e128"])


def _ssd_sample_kernel(y_in_ref, z_ref, xbc_ref, dt_ref, s0_ref, c0_ref, cw_ref, cb_ref, dtb_ref, alog_ref,
                       dsk_ref, nw_ref, e64_ref, y_ref, ssm_ref, conv_ref, xpad_ref, zs_ref, dts_ref,
                       *, n_heads, seq, nb):
    del y_in_ref
    d_inner = n_heads * M_HEADDIM
    gw = d_inner // M_GROUPS
    bc0 = d_inner
    cc0 = d_inner + M_GROUPS * M_STATE
    hpg = n_heads // M_GROUPS
    R = SAMPLE_SLOT_ROWS
    x0 = 8
    row = lax.broadcasted_iota(jnp.int32, (R, 1), 0)
    valid = row < seq
    eye = (lax.broadcasted_iota(jnp.int32, (LANES, LANES), 0)
           == lax.broadcasted_iota(jnp.int32, (LANES, LANES), 1)).astype(BF16)

    xpad_ref[...] = jnp.zeros(xpad_ref.shape, F32)
    zs_ref[...] = jnp.zeros(zs_ref.shape, F32)
    dts_ref[...] = jnp.zeros(dts_ref.shape, F32)

    def shift_rows(x, s):
        return jnp.where(row >= s, pltpu.roll(x, s, 0), 0.0)

    for bi in range(nb):
        r0 = bi * seq
        xpad_ref[x0 - (M_CONV_W - 1):x0, :] = c0_ref[bi]
        xpad_ref[x0:x0 + seq, :] = xbc_ref[r0:r0 + seq, :]
        zs_ref[0:seq, :] = z_ref[r0:r0 + seq, :]
        dts_ref[0:seq, :] = dt_ref[r0:r0 + seq, :]
        xc = None
        for k in range(M_CONV_W):
            lo = x0 - (M_CONV_W - 1) + k
            term = xpad_ref[lo:lo + R, :] * cw_ref[k:k + 1, :]
            xc = term if xc is None else xc + term
        xc = xc + cb_ref[...]
        conv_ref[bi] = xpad_ref[x0 + seq - (M_CONV_W - 1):x0 + seq, :]
        xs = _silu(xc)
        xh = xs[:, :d_inner]

        dt = jnp.where(valid, _softplus(dts_ref[...] + dtb_ref[...]), 0.0)
        dta = dt * (-jnp.exp(alog_ref[...]))
        acum = dta
        s = 1
        while s < seq:
            acum = acum + shift_rows(acum, s)
            s *= 2
        a_full = _dot_sel_r(acum, e64_ref[...])
        dt_full = _dot_sel_r(dt, e64_ref[...])
        a_last = a_full[seq - 1:seq, :]
        exp_a = jnp.exp(a_full)
        xdt = xh * dt_full
        xw = jnp.where(valid, xdt * jnp.exp(a_last - a_full), 0.0)
        dec_rows = jnp.broadcast_to(jnp.exp(a_last), (R, d_inner))

        y = dsk_ref[...] * xh
        for j in range(seq):
            lj = jnp.where(row >= j, jnp.exp(a_full - a_full[j:j + 1, :]), 0.0)
            wj = lj * xdt[j:j + 1, :]
            parts = []
            for g in range(M_GROUPS):
                cgb = (xs[:, cc0 + g * M_STATE: cc0 + (g + 1) * M_STATE]
                       * xs[j:j + 1, bc0 + g * M_STATE: bc0 + (g + 1) * M_STATE])
                parts.append(jnp.sum(cgb, axis=-1, keepdims=True) * wj[:, g * gw:(g + 1) * gw])
            y = y + jnp.concatenate(parts, axis=1)

        inter = []
        for pr in range(n_heads // 2):
            g = (2 * pr) // hpg
            cg = xs[:, cc0 + g * M_STATE: cc0 + (g + 1) * M_STATE]
            bg = xs[:, bc0 + g * M_STATE: bc0 + (g + 1) * M_STATE]
            h0 = s0_ref[bi, 2 * pr:2 * pr + 2].reshape(2 * M_HEADDIM, M_STATE)
            inter.append(_dot_nt(cg, h0))
            lanes = slice(pr * LANES, (pr + 1) * LANES)
            xw_t = _dot_nt(eye, xw[:, lanes])
            hi, mid, lo3 = _split3(dec_rows[:, lanes])
            dec_t = _dot_nt(eye, hi) + _dot_nt(eye, mid) + _dot_nt(eye, lo3)
            upd = _dot(xw_t, bg)
            ssm_ref[bi, 2 * pr:2 * pr + 2] = (h0 * dec_t[:, 0:1] + upd).reshape(2, M_HEADDIM, M_STATE)
        y = y + jnp.concatenate(inter, axis=1) * exp_a

        y = y * _silu(zs_ref[...])
        outs = []
        for g in range(M_GROUPS):
            yg = y[:, g * gw:(g + 1) * gw]
            outs.append(yg * lax.rsqrt(jnp.mean(yg * yg, axis=-1, keepdims=True) + M_NORM_EPS))
        yn = jnp.concatenate(outs, axis=1) * nw_ref[...]
        y_ref[r0:r0 + seq, :] = yn[0:seq, :]


def _ssd_sample(y_all, z, xbc, dt, s0, c0, prm, row0, seq, n_heads, *, nb=4):
    T, d_inner = z.shape
    conv_dim = xbc.shape[1]
    DB = s0.shape[0]
    rows = nb * seq
    blk0 = row0 // rows
    row = lambda i: (blk0 + i, 0)
    kern = functools.partial(_ssd_sample_kernel, n_heads=n_heads, seq=seq, nb=nb)
    return pl.pallas_call(
        kern,
        out_shape=(jax.ShapeDtypeStruct((T, d_inner), F32),
                   jax.ShapeDtypeStruct(s0.shape, F32), jax.ShapeDtypeStruct(c0.shape, F32)),
        grid=(DB // nb,),
        in_specs=[pl.BlockSpec(memory_space=pl.ANY),
                  pl.BlockSpec((rows, d_inner), row), pl.BlockSpec((rows, conv_dim), row),
                  pl.BlockSpec((rows, LANES), row),
                  pl.BlockSpec((nb, n_heads, M_HEADDIM, M_STATE), lambda i: (i, 0, 0, 0)),
                  pl.BlockSpec((nb, M_CONV_W - 1, conv_dim), lambda i: (i, 0, 0)),
                  _resident((M_CONV_W, conv_dim)), _resident((1, conv_dim)), _resident((1, LANES)),
                  _resident((1, LANES)), _resident((1, d_inner)), _resident((1, d_inner)),
                  _resident((LANES, d_inner))],
        out_specs=(pl.BlockSpec((rows, d_inner), row),
                   pl.BlockSpec((nb, n_heads, M_HEADDIM, M_STATE), lambda i: (i, 0, 0, 0)),
                   pl.BlockSpec((nb, M_CONV_W - 1, conv_dim), lambda i: (i, 0, 0))),
        scratch_shapes=[pltpu.VMEM((8 + SAMPLE_SLOT_ROWS + 8, conv_dim), F32), pltpu.VMEM((SAMPLE_SLOT_ROWS, d_inner), F32),
                        pltpu.VMEM((SAMPLE_SLOT_ROWS, LANES), F32)],
        input_output_aliases={0: 0},
        compiler_params=_cparams(1),
        name="ssd_sample",
    )(y_all, z, xbc, dt, s0, c0, prm["conv_w"], prm["conv_b"], prm["dt_bias"], prm["a_log"], prm["d_full"],
      prm["norm_w"], prm["e64"])


def _r_proj_kernel(*refs, tm, blocks_per_seq, n_prompt_blocks, sample_seq, has_vres):
    if has_vres:
        (h_ref, g_ref, pshift_ref, srows_ref, mu_ref, wr_ref, wk_ref, wv_ref, w1_ref, w2_ref, a1_ref, a2_ref,
         g1_ref, g2_ref, w0_ref, a0_ref, kk_ref, ka_ref, vf_ref, v0_ref, v1_ref, v2_ref,
         r_o, k_o, v_o, kkr_o, a_o, w_o, g_o, u_o, sc_ref) = refs
    else:
        (h_ref, g_ref, pshift_ref, srows_ref, mu_ref, wr_ref, wk_ref, wv_ref, w1_ref, w2_ref, a1_ref, a2_ref,
         g1_ref, g2_ref, w0_ref, a0_ref, kk_ref, ka_ref,
         r_o, k_o, v_o, kkr_o, a_o, w_o, g_o, u_o, sc_ref) = refs
    i = pl.program_id(0)
    u = _rms(h_ref[...], g_ref[...])
    u_o[...] = u
    sc_ref[8:8 + tm, :] = u

    @pl.when(jnp.logical_and(i % blocks_per_seq == 0, i < n_prompt_blocks))
    def _():
        sc_ref[7:8, :] = pshift_ref[pl.ds(i // blocks_per_seq, 1), :]

    prev = sc_ref[7:7 + tm, :]
    row = lax.broadcasted_iota(jnp.int32, (tm, 1), 0)
    use_state = jnp.logical_and(i >= n_prompt_blocks, jnp.bitwise_and(row, sample_seq - 1) == 0)
    prev = jnp.where(use_state, srows_ref[...], prev)
    sc_ref[7:8, :] = sc_ref[7 + tm:8 + tm, :]

    xx = prev - u
    xr = u + xx * mu_ref[0:1, :]
    xw = u + xx * mu_ref[1:2, :]
    xk = u + xx * mu_ref[2:3, :]
    xv = u + xx * mu_ref[3:4, :]
    xa = u + xx * mu_ref[4:5, :]
    xg = u + xx * mu_ref[5:6, :]
    r = _dot(xr, wr_ref[...])
    k = _dot(xk, wk_ref[...])
    v = _dot(xv, wv_ref[...])
    w = -_softplus(-(w0_ref[...] + _dot(jnp.tanh(_dot(xw, w1_ref[...])), w2_ref[...]))) - 0.5
    a = jax.nn.sigmoid(a0_ref[...] + _dot(_dot(xa, a1_ref[...]), a2_ref[...]))
    g = _dot(jax.nn.sigmoid(_dot(xg, g1_ref[...])), g2_ref[...])
    npair = r_o.shape[0]
    if has_vres:
        vf = jnp.concatenate([vf_ref[p] for p in range(npair)], axis=1)
        v = v + (vf - v) * jax.nn.sigmoid(v0_ref[...] + _dot(_dot(xv, v1_ref[...]), v2_ref[...]))
    kkr = k * kk_ref[...]
    k = k * (1.0 + (a - 1.0) * ka_ref[...])
    for p in range(npair):
        sl = slice(p * LANES, (p + 1) * LANES)
        r_o[p] = r[:, sl]
        k_o[p] = k[:, sl]
        v_o[p] = v[:, sl]
        kkr_o[p] = kkr[:, sl]
        a_o[p] = a[:, sl]
        w_o[p] = w[:, sl]
        g_o[p] = g[:, sl]


def _pad_cols(w, n):
    return jnp.pad(w, ((0, 0), (0, n - w.shape[1])))


def _pad_rows(w, n):
    return jnp.pad(w, ((0, n - w.shape[0]), (0, 0)))


def _round_up(n, m):
    return -(-n // m) * m


def _r_proj(h, gain, pshift, srows, rp, v_first, L, n_prompt_rows, sample_seq, *, tm=256):
    T, D = h.shape
    npair = D // LANES
    has_vres = v_first is not None
    row2 = lambda i: (i, 0)
    npb = n_prompt_rows // tm
    lw = _round_up(rp["w1"].shape[1], LANES)
    la = _round_up(rp["a1"].shape[1], LANES)
    lg = _round_up(rp["g1"].shape[1], LANES)
    ops = [h, gain.reshape(1, D), pshift, srows, rp["mu"],
           rp["wr"].astype(BF16), rp["wk"].astype(BF16), rp["wv"].astype(BF16),
           _pad_cols(rp["w1"], lw).astype(BF16), _pad_rows(rp["w2"], lw).astype(BF16),
           _pad_cols(rp["a1"], la).astype(BF16), _pad_rows(rp["a2"], la).astype(BF16),
           _pad_cols(rp["g1"], lg).astype(BF16), _pad_rows(rp["g2"], lg).astype(BF16),
           rp["w0"].reshape(1, D), rp["a0"].reshape(1, D), rp["k_k"].reshape(1, D), rp["k_a"].reshape(1, D)]
    specs = [pl.BlockSpec((tm, D), row2), _resident((1, D)), _resident(pshift.shape),
             pl.BlockSpec((tm, D), lambda i: (jnp.maximum(i - npb, 0), 0)), _resident((6, D)),
             _resident((D, D)), _resident((D, D)), _resident((D, D)),
             _resident((D, lw)), _resident((lw, D)), _resident((D, la)), _resident((la, D)),
             _resident((D, lg)), _resident((lg, D)),
             _resident((1, D)), _resident((1, D)), _resident((1, D)), _resident((1, D))]
    pair_spec = pl.BlockSpec((npair, tm, LANES), lambda i: (0, i, 0))
    if has_vres:
        lv = _round_up(rp["v1"].shape[1], LANES)
        ops += [v_first, rp["v0"].reshape(1, D), _pad_cols(rp["v1"], lv).astype(BF16),
                _pad_rows(rp["v2"], lv).astype(BF16)]
        specs += [pair_spec, _resident((1, D)), _resident((D, lv)), _resident((lv, D))]
    pair_shape = jax.ShapeDtypeStruct((npair, T, LANES), F32)
    kern = functools.partial(_r_proj_kernel, tm=tm, blocks_per_seq=L // tm, n_prompt_blocks=npb,
                             sample_seq=sample_seq, has_vres=has_vres)
    return pl.pallas_call(
        kern,
        out_shape=(pair_shape,) * 7 + (jax.ShapeDtypeStruct((T, D), F32),),
        grid=(T // tm,),
        in_specs=specs,
        out_specs=(pair_spec,) * 7 + (pl.BlockSpec((tm, D), row2),),
        scratch_shapes=[pltpu.VMEM((tm + 8, D), F32)],
        compiler_params=_cparams(1),
        name="r_proj",
    )(*ops)


def _wkv_pair(r, k, v, kkr, a, w, g, rk, gnw, gnb, states, lseq):
    C = r.shape[0]
    ns = C // lseq
    lane = lax.broadcasted_iota(jnp.int32, (1, LANES), 1)
    m0 = lane < R_HEADDIM

    def headsum(x):
        s0 = jnp.sum(jnp.where(m0, x, 0.0), axis=-1, keepdims=True)
        s1 = jnp.sum(jnp.where(m0, 0.0, x), axis=-1, keepdims=True)
        return jnp.where(m0, s0, s1)

    kk = kkr * lax.rsqrt(jnp.maximum(headsum(kkr * kkr), 1e-24))
    alpha = -kk
    beta = kk * a
    logd = -jnp.exp(w)

    ti = lax.broadcasted_iota(jnp.int32, (C, C), 0)
    si = lax.broadcasted_iota(jnp.int32, (C, C), 1)
    sh = int(math.log2(lseq))
    same = jnp.right_shift(ti, sh) == jnp.right_shift(si, sh)
    incl = jnp.logical_and(same, si <= ti)
    strict = jnp.logical_and(same, si < ti)
    lg = _dot_sel_l(incl, logd)
    ltot = _dot_sel_l(same, logd)
    g_in = jnp.exp(lg)
    ginv = jnp.exp(-lg)
    gend = jnp.exp(ltot - lg)
    w0 = alpha * jnp.exp(lg - logd)
    rg = r * g_in
    bi = beta * ginv
    ki = k * ginv
    wr = jnp.concatenate([w0, rg], axis=0)
    rsel = jnp.right_shift(jnp.bitwise_and(lax.broadcasted_iota(jnp.int32, (2 * C, 1), 0), C - 1), sh)

    xy0 = None
    for s in range(ns):
        t = _dot_nt(wr, states[s])
        if ns > 1:
            t = jnp.where(rsel == s, t, 0.0)
        xy0 = t if xy0 is None else xy0 + t
    x0 = xy0[:C]
    y0 = xy0[C:]

    eye = (ti == si).astype(F32)
    u = None
    y = None
    for hh in range(2):
        mh = m0 if hh == 0 else jnp.logical_not(m0)
        wrm = jnp.where(mh, wr, 0.0)
        gb = _dot_nt(wrm, bi)
        gk = _dot_nt(wrm, ki)
        a_ab = jnp.where(strict, gb[:C], 0.0)
        a_ak = jnp.where(strict, gk[:C], 0.0)
        b_rb = jnp.where(incl, gb[C:], 0.0)
        b_rk = jnp.where(incl, gk[C:], 0.0)
        tinv = eye + a_ab
        pw = a_ab
        for _ in range(int(math.log2(lseq)) - 1):
            pw = _dot3(pw, pw)
            tinv = tinv + _dot3(tinv, pw)
        uh = _dot3(tinv, x0 + _dot(a_ak, v))
        yh = y0 + _dot(b_rb, uh) + _dot(b_rk, v)
        u = uh if u is None else jnp.where(m0, u, uh)
        y = yh if y is None else jnp.where(m0, y, yh)

    uvt = jnp.concatenate([u, v], axis=0).T
    bkd = jnp.concatenate([beta * gend, k * gend], axis=0)
    bi_r = lax.broadcasted_iota(jnp.int32, (LANES, LANES), 0) >= R_HEADDIM
    bi_c = lax.broadcasted_iota(jnp.int32, (LANES, LANES), 1) >= R_HEADDIM
    same_head = bi_r == bi_c
    new_states = []
    for s in range(ns):
        rhs = bkd if ns == 1 else jnp.where(rsel == s, bkd, 0.0)
        ds = jnp.where(same_head, _dot(uvt, rhs), 0.0)
        new_states.append(states[s] * jnp.exp(ltot[s * lseq:s * lseq + 1, :]) + ds)

    mean = headsum(y) * (1.0 / R_HEADDIM)
    d = y - mean
    var = headsum(d * d) * (1.0 / R_HEADDIM)
    yn = d * lax.rsqrt(var + R_GN_EPS) * gnw + gnb
    out = (yn + headsum(r * k * rk) * v) * g
    return out, new_states


def _wkv_prompt_kernel(r_ref, k_ref, v_ref, kkr_ref, a_ref, w_ref, g_ref, rk_ref, gnw_ref, gnb_ref,
                       y_ref, s_out_ref, s_ref, *, chunks_per_seq):
    c = pl.program_id(0) % chunks_per_seq
    npair = r_ref.shape[0]

    @pl.when(c == 0)
    def _():
        s_ref[...] = jnp.zeros(s_ref.shape, F32)

    def pair(p, carry):
        out, new = _wkv_pair(r_ref[p], k_ref[p], v_ref[p], kkr_ref[p], a_ref[p], w_ref[p], g_ref[p],
                             rk_ref[p], gnw_ref[p], gnb_ref[p], [s_ref[p]], R_CHUNK)
        y_ref[p] = out.astype(y_ref.dtype)
        s_ref[p] = new[0]
        return carry

    lax.fori_loop(0, npair, pair, 0)

    @pl.when(c == chunks_per_seq - 1)
    def _():
        for p in range(npair):
            s = s_ref[p]
            s_out_ref[0, p] = s[:, :R_HEADDIM] + s[:, R_HEADDIM:]


def _wkv_prompt(streams, rk, gnw, gnb, B, L):
    npair, T, _ = streams[0].shape
    C = R_CHUNK
    cps = L // C
    blk = pl.BlockSpec((npair, C, LANES), lambda i: (0, i, 0))
    vec = _resident((npair, 1, LANES))
    return pl.pallas_call(
        functools.partial(_wkv_prompt_kernel, chunks_per_seq=cps),
        out_shape=(jax.ShapeDtypeStruct((npair, T, LANES), BF16),
                   jax.ShapeDtypeStruct((B, npair, LANES, R_HEADDIM), F32)),
        grid=(B * cps,),
        in_specs=[blk] * 7 + [vec] * 3,
        out_specs=(blk, pl.BlockSpec((1, npair, LANES, R_HEADDIM), lambda i: (i // cps, 0, 0, 0))),
        scratch_shapes=[pltpu.VMEM((npair, LANES, LANES), F32)],
        compiler_params=_cparams(1),
        name="wkv_prompt",
    )(*streams, rk, gnw, gnb)


def _wkv_sample_kernel(y_in_ref, r_ref, k_ref, v_ref, kkr_ref, a_ref, w_ref, g_ref, rk_ref, gnw_ref, gnb_ref,
                       s0_ref, y_ref, s_out_ref, *, seq):
    del y_in_ref
    npair = r_ref.shape[0]
    ns = R_CHUNK // seq
    lane = lax.broadcasted_iota(jnp.int32, (LANES, LANES), 1) >= R_HEADDIM
    rowh = lax.broadcasted_iota(jnp.int32, (LANES, LANES), 0) >= R_HEADDIM
    same_head = lane == rowh

    def pair(p, carry):
        states = []
        for s in range(ns):
            s2 = s0_ref[s, p]
            states.append(jnp.where(same_head, jnp.concatenate([s2, s2], axis=1), 0.0))
        out, new = _wkv_pair(r_ref[p], k_ref[p], v_ref[p], kkr_ref[p], a_ref[p], w_ref[p], g_ref[p],
                             rk_ref[p], gnw_ref[p], gnb_ref[p], states, seq)
        y_ref[p] = out.astype(y_ref.dtype)
        for s in range(ns):
            s_out_ref[s, p] = new[s][:, :R_HEADDIM] + new[s][:, R_HEADDIM:]
        return carry

    lax.fori_loop(0, npair, pair, 0)


def _wkv_sample(y_all, streams, rk, gnw, gnb, s0, row0, seq):
    npair, T, _ = streams[0].shape
    C = R_CHUNK
    ns = C // seq
    DB = s0.shape[0]
    blk0 = row0 // C
    blk = pl.BlockSpec((npair, C, LANES), lambda i: (0, blk0 + i, 0))
    vec = _resident((npair, 1, LANES))
    sblk = pl.BlockSpec((ns, npair, LANES, R_HEADDIM), lambda i: (i, 0, 0, 0))
    return pl.pallas_call(
        functools.partial(_wkv_sample_kernel, seq=seq),
        out_shape=(jax.ShapeDtypeStruct((npair, T, LANES), BF16), jax.ShapeDtypeStruct(s0.shape, F32)),
        grid=(DB // ns,),
        in_specs=[pl.BlockSpec(memory_space=pl.ANY)] + [blk] * 7 + [vec] * 3 + [sblk],
        out_specs=(blk, sblk),
        input_output_aliases={0: 0},
        compiler_params=_cparams(1),
        name="wkv_sample",
    )(y_all, *streams, rk, gnw, gnb, s0)


def _r_out_kernel(h_ref, y_ref, w_ref, o_ref):
    y = jnp.concatenate([y_ref[p] for p in range(y_ref.shape[0])], axis=1)
    o_ref[...] = h_ref[...] + jnp.dot(y, w_ref[...], preferred_element_type=F32)


def _r_out(h, y, w, *, tm=512):
    T, D = h.shape
    npair = y.shape[0]
    return pl.pallas_call(
        _r_out_kernel,
        out_shape=jax.ShapeDtypeStruct((T, D), F32),
        grid=(T // tm,),
        in_specs=[pl.BlockSpec((tm, D), lambda i: (i, 0)), pl.BlockSpec((npair, tm, LANES), lambda i: (0, i, 0)),
                  _resident((D, D))],
        out_specs=pl.BlockSpec((tm, D), lambda i: (i, 0)),
        compiler_params=_cparams(1),
        name="r_out",
    )(h, y, w.astype(BF16))


def kernel(x_prompt, x_sample, p_prompt, p_sample, state_ssm, state_conv, state_wkv, state_shift, norm_ffn1, ffn1_gate_up, ffn1_down, norm_mix, norm_ffn2, ffn2_gate_up, ffn2_down, norm_ple, ple_in, ple_gate, norm_final, m_in_proj, m_conv_w, m_conv_b, m_dt_bias, m_A_log, m_D, m_norm, m_out_proj, r_mu, r_wr, r_wk, r_wv, r_wo, r_w0, r_w1, r_w2, r_a0, r_a1, r_a2, r_g1, r_g2, r_k_k, r_k_a, r_r_k, r_gn_w, r_gn_b, r_v0, r_v1, r_v2):
    B, L, D = x_prompt.shape
    DB, DL, _ = x_sample.shape
    depth = norm_ffn1.shape[0]
    TP, TS = B * L, DB * DL
    n_heads = m_A_log.shape[1]
    d_inner = n_heads * M_HEADDIM
    conv_dim = m_conv_w.shape[2]
    r_heads = D // R_HEADDIM
    npair = D // LANES

    h = jnp.concatenate([x_prompt.reshape(TP, D), x_sample.reshape(TS, D)], axis=0)
    p_all = jnp.concatenate([p_prompt.reshape(depth, TP, -1), p_sample.reshape(depth, TS, -1)], axis=1)
    e64, e128 = _head_expanders(n_heads)

    ssm_p, conv_p, ssm_s, conv_s = [], [], [], []
    wkv_p, shift_p, wkv_s, shift_s = [], [], [], []
    v_first = None
    for i in range(depth):
        j = i // 2
        h = _ffn(h, norm_ffn1[i], ffn1_gate_up[i], ffn1_down[i])
        if i % 2 == 0:
            z, xbc, dt = _m_in_proj(h, norm_mix[i], m_in_proj[j], d_inner, conv_dim, n_heads)
            prm = dict(conv_w=m_conv_w[j], conv_b=m_conv_b[j].reshape(1, conv_dim),
                       dt_bias=jnp.pad(m_dt_bias[j], (0, LANES - n_heads)).reshape(1, LANES),
                       a_log=jnp.pad(m_A_log[j], (0, LANES - n_heads)).reshape(1, LANES),
                       d_full=jnp.repeat(m_D[j], M_HEADDIM).reshape(1, d_inner),
                       norm_w=m_norm[j].reshape(1, d_inner), e64=e64, e128=e128)
            y, s_p, c_p = _ssd_prompt(z, xbc, dt, prm, B, L, n_heads)
            y, s_s, c_s = _ssd_sample(y, z, xbc, dt, state_ssm[j], state_conv[j], prm, TP, DL, n_heads)
            ssm_p.append(s_p)
            conv_p.append(c_p)
            ssm_s.append(s_s)
            conv_s.append(c_s)
            h = _out_proj(h, y, m_out_proj[j])
        else:
            rp = dict(mu=r_mu[j], wr=r_wr[j], wk=r_wk[j], wv=r_wv[j], w0=r_w0[j], w1=r_w1[j], w2=r_w2[j],
                      a0=r_a0[j], a1=r_a1[j], a2=r_a2[j], g1=r_g1[j], g2=r_g2[j], k_k=r_k_k[j], k_a=r_k_a[j])
            if j > 0:
                rp.update(v0=r_v0[j - 1], v1=r_v1[j - 1], v2=r_v2[j - 1])
            pshift = jnp.zeros((B, D), F32)
            srows = jnp.zeros((DB, DL, D), F32).at[:, 0].set(state_shift[j]).reshape(TS, D)
            outs = _r_proj(h, norm_mix[i], pshift, srows, rp, v_first if j > 0 else None, L, TP, DL)
            streams, u = outs[:7], outs[7]
            if j == 0:
                v_first = streams[2]
            rk = r_r_k[j].reshape(npair, 1, LANES)
            gnw = r_gn_w[j].reshape(npair, 1, LANES)
            gnb = r_gn_b[j].reshape(npair, 1, LANES)
            y, s_p = _wkv_prompt(streams, rk, gnw, gnb, B, L)
            s0 = state_wkv[j].reshape(DB, npair, LANES, R_HEADDIM)
            y, s_s = _wkv_sample(y, streams, rk, gnw, gnb, s0, TP, DL)
            wkv_p.append(s_p.reshape(B, r_heads, R_HEADDIM, R_HEADDIM))
            wkv_s.append(s_s.reshape(DB, r_heads, R_HEADDIM, R_HEADDIM))
            shift_p.append(u[:TP].reshape(B, L, D)[:, L - 1])
            shift_s.append(u[TP:].reshape(DB, DL, D)[:, DL - 1])
            h = _r_out(h, y, r_wo[j])
        h = _ffn(h, norm_ffn2[i], ffn2_gate_up[i], ffn2_down[i])
        h = _ple(h, p_all[i], norm_ple[i], ple_in[i], ple_gate[i], norm_final, final=(i == depth - 1))

    return (h[:TP].reshape(B, L, D), h[TP:].reshape(DB, DL, D),
            jnp.stack(ssm_p), jnp.stack(conv_p), jnp.stack(wkv_p), jnp.stack(shift_p),
            jnp.stack(ssm_s), jnp.stack(conv_s), jnp.stack(wkv_s), jnp.stack(shift_s))
```

```python
import functools
import math

import jax
import jax.numpy as jnp
from jax import lax
from jax.experimental import pallas as pl
from jax.experimental.pallas import tpu as pltpu

F32 = jnp.float32
BF16 = jnp.bfloat16

NORM_EPS = 1e-6
M_NORM_EPS = 1e-5
R_GN_EPS = 64e-5

LANES = 128
M_HEADDIM = 64
M_STATE = 128
M_GROUPS = 8
M_CHUNK = 128
M_CONV_W = 4
R_HEADDIM = 64
R_CHUNK = 64
SAMPLE_SLOT_ROWS = 16
VMEM_LIMIT_BYTES = 56 * 1024 * 1024


def _cparams(n_axes):
    return pltpu.CompilerParams(dimension_semantics=("arbitrary",) * n_axes,
                                vmem_limit_bytes=VMEM_LIMIT_BYTES)


def _resident(shape):
    zeros = (0,) * len(shape)
    return pl.BlockSpec(shape, lambda *_: zeros, pipeline_mode=pl.Buffered(1))


def _dot(a, b):
    return jnp.dot(a.astype(BF16), b.astype(BF16), preferred_element_type=F32)


def _dot_nt(a, b):
    return lax.dot_general(a.astype(BF16), b.astype(BF16), (((1,), (1,)), ((), ())),
                           preferred_element_type=F32)


def _split3(x):
    hi = x.astype(BF16)
    r1 = x - hi.astype(F32)
    mid = r1.astype(BF16)
    lo = (r1 - mid.astype(F32)).astype(BF16)
    return hi, mid, lo


def _dot_sel_l(sel, x):
    hi, mid, lo = _split3(x)
    s = sel.astype(BF16)
    return (jnp.dot(s, hi, preferred_element_type=F32) + jnp.dot(s, mid, preferred_element_type=F32)
            + jnp.dot(s, lo, preferred_element_type=F32))


def _dot_sel_r(x, sel):
    hi, mid, lo = _split3(x)
    s = sel.astype(BF16)
    return (jnp.dot(hi, s, preferred_element_type=F32) + jnp.dot(mid, s, preferred_element_type=F32)
            + jnp.dot(lo, s, preferred_element_type=F32))


def _rms(x, g):
    ms = jnp.mean(x * x, axis=-1, keepdims=True)
    return (x * lax.rsqrt(ms + NORM_EPS)) * g


def _silu(x):
    return x * jax.nn.sigmoid(x)


def _softplus(x):
    return jnp.maximum(x, 0.0) + jnp.log1p(jnp.exp(-jnp.abs(x)))


def _ffn_kernel(h_ref, g_ref, wgu_ref, wd_ref, o_ref, ub_ref, *, fc):
    x = h_ref[...]
    ub_ref[...] = _rms(x, g_ref[...]).astype(BF16)
    F = wd_ref.shape[0]
    acc = x
    for c in range(F // fc):
        ub = ub_ref[...]
        gate = jnp.dot(ub, wgu_ref[:, c * fc:(c + 1) * fc], preferred_element_type=F32)
        up = jnp.dot(ub, wgu_ref[:, F + c * fc:F + (c + 1) * fc], preferred_element_type=F32)
        act = (_silu(gate) * up).astype(BF16)
        acc = acc + 0.5 * jnp.dot(act, wd_ref[c * fc:(c + 1) * fc, :], preferred_element_type=F32)
    o_ref[...] = acc


def _ffn(h, gain, w_gate_up, w_down, *, tm=512, fc=256):
    T, D = h.shape
    F = w_down.shape[0]
    return pl.pallas_call(
        functools.partial(_ffn_kernel, fc=fc),
        out_shape=jax.ShapeDtypeStruct((T, D), F32),
        grid=(T // tm,),
        in_specs=[pl.BlockSpec((tm, D), lambda i: (i, 0)),
                  _resident((1, D)), _resident((D, 2 * F)), _resident((F, D))],
        out_specs=pl.BlockSpec((tm, D), lambda i: (i, 0)),
        scratch_shapes=[pltpu.VMEM((tm, D), BF16)],
        compiler_params=_cparams(1),
        name="ffn",
    )(h, gain.reshape(1, D), w_gate_up.astype(BF16), w_down.astype(BF16))


def _ple_kernel(h_ref, p_ref, g_ref, wi_ref, wg_ref, gf_ref, o_ref, *, final):
    x = h_ref[...]
    gate = jax.nn.sigmoid(_dot(_rms(x, g_ref[...]), wg_ref[...]))
    y = x + _dot(p_ref[...], wi_ref[...]) * gate
    if final:
        y = _rms(y, gf_ref[...])
    o_ref[...] = y


def _ple(h, p, gain, w_in, w_gate, gain_final, *, final, tm=512):
    T, D = h.shape
    P = p.shape[1]
    return pl.pallas_call(
        functools.partial(_ple_kernel, final=final),
        out_shape=jax.ShapeDtypeStruct((T, D), F32),
        grid=(T // tm,),
        in_specs=[pl.BlockSpec((tm, D), lambda i: (i, 0)), pl.BlockSpec((tm, P), lambda i: (i, 0)),
                  _resident((1, D)), _resident((P, D)), _resident((D, D)), _resident((1, D))],
        out_specs=pl.BlockSpec((tm, D), lambda i: (i, 0)),
        compiler_params=_cparams(1),
        name="ple",
    )(h, p, gain.reshape(1, D), w_in.astype(BF16), w_gate.astype(BF16), gain_final.reshape(1, D))


def _out_proj_kernel(h_ref, y_ref, w_ref, o_ref):
    o_ref[...] = h_ref[...] + _dot(y_ref[...], w_ref[...])


def _out_proj(h, y, w, *, tm=512):
    T, D = h.shape
    K = y.shape[1]
    return pl.pallas_call(
        _out_proj_kernel,
        out_shape=jax.ShapeDtypeStruct((T, D), F32),
        grid=(T // tm,),
        in_specs=[pl.BlockSpec((tm, D), lambda i: (i, 0)), pl.BlockSpec((tm, K), lambda i: (i, 0)),
                  _resident((K, D))],
        out_specs=pl.BlockSpec((tm, D), lambda i: (i, 0)),
        compiler_params=_cparams(1),
        name="out_proj",
    )(h, y, w.astype(BF16))


def _m_in_kernel(h_ref, g_ref, wz_ref, wx_ref, wdt_ref, z_ref, xbc_ref, dt_ref):
    u = _rms(h_ref[...], g_ref[...]).astype(BF16)
    z_ref[...] = jnp.dot(u, wz_ref[...], preferred_element_type=F32)
    xbc_ref[...] = jnp.dot(u, wx_ref[...], preferred_element_type=F32)
    dt_ref[...] = jnp.dot(u, wdt_ref[...], preferred_element_type=F32)


def _m_in_proj(h, gain, w_in, d_inner, conv_dim, n_heads, *, tm=256):
    T, D = h.shape
    wz = w_in[:, :d_inner].astype(BF16)
    wx = w_in[:, d_inner:d_inner + conv_dim].astype(BF16)
    wdt = jnp.pad(w_in[:, d_inner + conv_dim:], ((0, 0), (0, LANES - n_heads))).astype(BF16)
    return pl.pallas_call(
        _m_in_kernel,
        out_shape=(jax.ShapeDtypeStruct((T, d_inner), F32), jax.ShapeDtypeStruct((T, conv_dim), F32),
                   jax.ShapeDtypeStruct((T, LANES), F32)),
        grid=(T // tm,),
        in_specs=[pl.BlockSpec((tm, D), lambda i: (i, 0)), _resident((1, D)),
                  _resident((D, d_inner)), _resident((D, conv_dim)), _resident((D, LANES))],
        out_specs=(pl.BlockSpec((tm, d_inner), lambda i: (i, 0)), pl.BlockSpec((tm, conv_dim), lambda i: (i, 0)),
                   pl.BlockSpec((tm, LANES), lambda i: (i, 0))),
        compiler_params=_cparams(1),
        name="m_in_proj",
    )(h, gain.reshape(1, D), wz, wx, wdt)


def _head_expanders(n_heads):
    hh = jnp.arange(LANES)[:, None]
    e64 = (hh == (jnp.arange(n_heads * M_HEADDIM)[None, :] // M_HEADDIM)).astype(BF16)
    e128 = (hh == (jnp.arange(n_heads * LANES)[None, :] // LANES)).astype(BF16)
    return e64, e128


def _ssd_prompt_kernel(z_ref, xbc_ref, dt_ref, cw_ref, cb_ref, dtb_ref, alog_ref, dsk_ref, nw_ref,
                       e64_ref, e128_ref, y_ref, ssm_ref, conv_ref, xpad_ref, s_ref, *, n_heads):
    c = pl.program_id(1)
    nc = pl.num_programs(1)
    Q = M_CHUNK
    d_inner = n_heads * M_HEADDIM
    gw = d_inner // M_GROUPS
    bc0 = d_inner
    cc0 = d_inner + M_GROUPS * M_STATE

    @pl.when(c == 0)
    def _():
        xpad_ref[0:8, :] = jnp.zeros((8, xpad_ref.shape[1]), F32)
        s_ref[...] = jnp.zeros(s_ref.shape, F32)

    xraw = xbc_ref[...]
    xpad_ref[8:8 + Q, :] = xraw
    xc = (xpad_ref[5:5 + Q, :] * cw_ref[0:1, :] + xpad_ref[6:6 + Q, :] * cw_ref[1:2, :]
          + xpad_ref[7:7 + Q, :] * cw_ref[2:3, :] + xraw * cw_ref[3:4, :] + cb_ref[...])
    xpad_ref[0:8, :] = xraw[Q - 8:Q, :]
    xs = _silu(xc)
    xh = xs[:, :d_inner]

    dt = _softplus(dt_ref[...] + dtb_ref[...])
    dta = dt * (-jnp.exp(alog_ref[...]))
    ri = lax.broadcasted_iota(jnp.int32, (Q, Q), 0)
    ci = lax.broadcasted_iota(jnp.int32, (Q, Q), 1)
    causal = ri >= ci
    acum = _dot_sel_l(causal, dta)
    a_full = _dot_sel_r(acum, e64_ref[...])
    dt_full = _dot_sel_r(dt, e64_ref[...])
    a_col = _dot_sel_r(acum, e128_ref[...])
    acum_t = acum.T
    a_last = a_full[Q - 1:Q, :]
    exp_a = jnp.exp(a_full)
    xdt = xh * dt_full
    xw = xdt * jnp.exp(a_last - a_full)
    chunk_decay = jnp.exp(a_last)
    lane = lax.broadcasted_iota(jnp.int32, (Q, LANES), 1)
    first_head = lane < M_HEADDIM
    hpg = n_heads // M_GROUPS

    for g in range(M_GROUPS):
        bg = xs[:, bc0 + g * M_STATE: bc0 + (g + 1) * M_STATE]
        cg = xs[:, cc0 + g * M_STATE: cc0 + (g + 1) * M_STATE]
        cb = _dot_nt(cg, bg)
        sl = slice(g * gw, (g + 1) * gw)
        s_g = s_ref[:, sl]
        y_g = _dot(cg, s_g) * exp_a[:, sl]
        parts = []
        for q in range(hpg // 2):
            h0 = g * hpg + 2 * q
            xp = xdt[:, h0 * M_HEADDIM: h0 * M_HEADDIM + LANES]
            ys = []
            for hh in (h0, h0 + 1):
                seg = a_col[:, hh * LANES:(hh + 1) * LANES] - acum_t[hh:hh + 1, :]
                m = cb * jnp.exp(jnp.where(causal, seg, -jnp.inf))
                ys.append(_dot(m, xp))
            parts.append(jnp.where(first_head, ys[0], ys[1]))
        y_g = y_g + jnp.concatenate(parts, axis=1) + dsk_ref[:, sl] * xh[:, sl]
        y_g = y_g * _silu(z_ref[:, sl])
        y_g = y_g * lax.rsqrt(jnp.mean(y_g * y_g, axis=-1, keepdims=True) + M_NORM_EPS)
        y_ref[:, sl] = y_g * nw_ref[:, sl]
        s_ref[:, sl] = s_g * chunk_decay[:, sl] + _dot(bg.T, xw[:, sl])

    @pl.when(c == nc - 1)
    def _():
        ssm_ref[0] = s_ref[...].T.reshape(n_heads, M_HEADDIM, M_STATE)
        conv_ref[0] = xraw[Q - (M_CONV_W - 1):Q, :]


def _ssd_prompt(z, xbc, dt, prm, B, L, n_heads):
    T, d_inner = z.shape
    conv_dim = xbc.shape[1]
    Q = M_CHUNK
    nc = L // Q
    row = lambda b, c: (b * nc + c, 0)
    kern = functools.partial(_ssd_prompt_kernel, n_heads=n_heads)
    return pl.pallas_call(
        kern,
        out_shape=(jax.ShapeDtypeStruct((T, d_inner), F32),
                   jax.ShapeDtypeStruct((B, n_heads, M_HEADDIM, M_STATE), F32),
                   jax.ShapeDtypeStruct((B, M_CONV_W - 1, conv_dim), F32)),
        grid=(B, nc),
        in_specs=[pl.BlockSpec((Q, d_inner), row), pl.BlockSpec((Q, conv_dim), row), pl.BlockSpec((Q, LANES), row),
                  _resident((M_CONV_W, conv_dim)), _resident((1, conv_dim)), _resident((1, LANES)),
                  _resident((1, LANES)), _resident((1, d_inner)), _resident((1, d_inner)),
                  _resident((LANES, d_inner)), _resident((LANES, n_heads * LANES))],
        out_specs=(pl.BlockSpec((Q, d_inner), row),
                   pl.BlockSpec((1, n_heads, M_HEADDIM, M_STATE), lambda b, c: (b, 0, 0, 0)),
                   pl.BlockSpec((1, M_CONV_W - 1, conv_dim), lambda b, c: (b, 0, 0))),
        scratch_shapes=[pltpu.VMEM((Q + 8, conv_dim), F32), pltpu.VMEM((M_STATE, d_inner), F32)],
        compiler_params=_cparams(2),
        name="ssd_prompt",
    )(z, xbc, dt, prm["conv_w"], prm["conv_b"], prm["dt_bias"], prm["a_log"], prm["d_full"], prm["norm_w"],
      prm["e64"], prm["e128"])


def _ssd_sample_kernel(y_in_ref, z_ref, xbc_ref, dt_ref, s0_ref, c0_ref, cw_ref, cb_ref, dtb_ref, alog_ref,
                       dsk_ref, nw_ref, e64_ref, y_ref, ssm_ref, conv_ref, xpad_ref, zs_ref, dts_ref,
                       *, n_heads, seq, nb):
    del y_in_ref
    d_inner = n_heads * M_HEADDIM
    gw = d_inner // M_GROUPS
    bc0 = d_inner
    cc0 = d_inner + M_GROUPS * M_STATE
    hpg = n_heads // M_GROUPS
    R = SAMPLE_SLOT_ROWS
    x0 = 8
    row = lax.broadcasted_iota(jnp.int32, (R, 1), 0)
    valid = row < seq
    eye = (lax.broadcasted_iota(jnp.int32, (LANES, LANES), 0)
           == lax.broadcasted_iota(jnp.int32, (LANES, LANES), 1)).astype(BF16)

    xpad_ref[...] = jnp.zeros(xpad_ref.shape, F32)
    zs_ref[...] = jnp.zeros(zs_ref.shape, F32)
    dts_ref[...] = jnp.zeros(dts_ref.shape, F32)

    def shift_rows(x, s):
        return jnp.where(row >= s, pltpu.roll(x, s, 0), 0.0)

    for bi in range(nb):
        r0 = bi * seq
        xpad_ref[x0 - (M_CONV_W - 1):x0, :] = c0_ref[bi]
        xpad_ref[x0:x0 + seq, :] = xbc_ref[r0:r0 + seq, :]
        zs_ref[0:seq, :] = z_ref[r0:r0 + seq, :]
        dts_ref[0:seq, :] = dt_ref[r0:r0 + seq, :]
        xc = None
        for k in range(M_CONV_W):
            lo = x0 - (M_CONV_W - 1) + k
            term = xpad_ref[lo:lo + R, :] * cw_ref[k:k + 1, :]
            xc = term if xc is None else xc + term
        xc = xc + cb_ref[...]
        conv_ref[bi] = xpad_ref[x0 + seq - (M_CONV_W - 1):x0 + seq, :]
        xs = _silu(xc)
        xh = xs[:, :d_inner]

        dt = jnp.where(valid, _softplus(dts_ref[...] + dtb_ref[...]), 0.0)
        dta = dt * (-jnp.exp(alog_ref[...]))
        acum = dta
        s = 1
        while s < seq:
            acum = acum + shift_rows(acum, s)
            s *= 2
        a_full = _dot_sel_r(acum, e64_ref[...])
        dt_full = _dot_sel_r(dt, e64_ref[...])
        a_last = a_full[seq - 1:seq, :]
        exp_a = jnp.exp(a_full)
        xdt = xh * dt_full
        xw = jnp.where(valid, xdt * jnp.exp(a_last - a_full), 0.0)
        dec_rows = jnp.broadcast_to(jnp.exp(a_last), (R, d_inner))

        y = dsk_ref[...] * xh
        for j in range(seq):
            lj = jnp.where(row >= j, jnp.exp(a_full - a_full[j:j + 1, :]), 0.0)
            wj = lj * xdt[j:j + 1, :]
            parts = []
            for g in range(M_GROUPS):
                cgb = (xs[:, cc0 + g * M_STATE: cc0 + (g + 1) * M_STATE]
                       * xs[j:j + 1, bc0 + g * M_STATE: bc0 + (g + 1) * M_STATE])
                parts.append(jnp.sum(cgb, axis=-1, keepdims=True) * wj[:, g * gw:(g + 1) * gw])
            y = y + jnp.concatenate(parts, axis=1)

        inter = []
        for pr in range(n_heads // 2):
            g = (2 * pr) // hpg
            cg = xs[:, cc0 + g * M_STATE: cc0 + (g + 1) * M_STATE]
            bg = xs[:, bc0 + g * M_STATE: bc0 + (g + 1) * M_STATE]
            h0 = s0_ref[bi, 2 * pr:2 * pr + 2].reshape(2 * M_HEADDIM, M_STATE)
            inter.append(_dot_nt(cg, h0))
            lanes = slice(pr * LANES, (pr + 1) * LANES)
            xw_t = _dot_nt(eye, xw[:, lanes])
            hi, mid, lo3 = _split3(dec_rows[:, lanes])
            dec_t = _dot_nt(eye, hi) + _dot_nt(eye, mid) + _dot_nt(eye, lo3)
            upd = _dot(xw_t, bg)
            ssm_ref[bi, 2 * pr:2 * pr + 2] = (h0 * dec_t[:, 0:1] + upd).reshape(2, M_HEADDIM, M_STATE)
        y = y + jnp.concatenate(inter, axis=1) * exp_a

        y = y * _silu(zs_ref[...])
        outs = []
        for g in range(M_GROUPS):
            yg = y[:, g * gw:(g + 1) * gw]
            outs.append(yg * lax.rsqrt(jnp.mean(yg * yg, axis=-1, keepdims=True) + M_NORM_EPS))
        yn = jnp.concatenate(outs, axis=1) * nw_ref[...]
        y_ref[r0:r0 + seq, :] = yn[0:seq, :]


def _ssd_sample(y_all, z, xbc, dt, s0, c0, prm, row0, seq, n_heads, *, nb=4):
    T, d_inner = z.shape
    conv_dim = xbc.shape[1]
    DB = s0.shape[0]
    rows = nb * seq
    blk0 = row0 // rows
    row = lambda i: (blk0 + i, 0)
    kern = functools.partial(_ssd_sample_kernel, n_heads=n_heads, seq=seq, nb=nb)
    return pl.pallas_call(
        kern,
        out_shape=(jax.ShapeDtypeStruct((T, d_inner), F32),
                   jax.ShapeDtypeStruct(s0.shape, F32), jax.ShapeDtypeStruct(c0.shape, F32)),
        grid=(DB // nb,),
        in_specs=[pl.BlockSpec(memory_space=pl.ANY),
                  pl.BlockSpec((rows, d_inner), row), pl.BlockSpec((rows, conv_dim), row),
                  pl.BlockSpec((rows, LANES), row),
                  pl.BlockSpec((nb, n_heads, M_HEADDIM, M_STATE), lambda i: (i, 0, 0, 0)),
                  pl.BlockSpec((nb, M_CONV_W - 1, conv_dim), lambda i: (i, 0, 0)),
                  _resident((M_CONV_W, conv_dim)), _resident((1, conv_dim)), _resident((1, LANES)),
                  _resident((1, LANES)), _resident((1, d_inner)), _resident((1, d_inner)),
                  _resident((LANES, d_inner))],
        out_specs=(pl.BlockSpec((rows, d_inner), row),
                   pl.BlockSpec((nb, n_heads, M_HEADDIM, M_STATE), lambda i: (i, 0, 0, 0)),
                   pl.BlockSpec((nb, M_CONV_W - 1, conv_dim), lambda i: (i, 0, 0))),
        scratch_shapes=[pltpu.VMEM((8 + SAMPLE_SLOT_ROWS + 8, conv_dim), F32), pltpu.VMEM((SAMPLE_SLOT_ROWS, d_inner), F32),
                        pltpu.VMEM((SAMPLE_SLOT_ROWS, LANES), F32)],
        input_output_aliases={0: 0},
        compiler_params=_cparams(1),
        name="ssd_sample",
    )(y_all, z, xbc, dt, s0, c0, prm["conv_w"], prm["conv_b"], prm["dt_bias"], prm["a_log"], prm["d_full"],
      prm["norm_w"], prm["e64"])


def _r_proj_kernel(*refs, tm, blocks_per_seq, n_prompt_blocks, sample_seq, has_vres):
    if has_vres:
        (h_ref, g_ref, pshift_ref, srows_ref, mu_ref, wr_ref, wk_ref, wv_ref, w1_ref, w2_ref, a1_ref, a2_ref,
         g1_ref, g2_ref, w0_ref, a0_ref, kk_ref, ka_ref, vf_ref, v0_ref, v1_ref, v2_ref,
         r_o, k_o, v_o, kkr_o, a_o, w_o, g_o, u_o, sc_ref) = refs
    else:
        (h_ref, g_ref, pshift_ref, srows_ref, mu_ref, wr_ref, wk_ref, wv_ref, w1_ref, w2_ref, a1_ref, a2_ref,
         g1_ref, g2_ref, w0_ref, a0_ref, kk_ref, ka_ref,
         r_o, k_o, v_o, kkr_o, a_o, w_o, g_o, u_o, sc_ref) = refs
    i = pl.program_id(0)
    u = _rms(h_ref[...], g_ref[...])
    u_o[...] = u
    sc_ref[8:8 + tm, :] = u

    @pl.when(jnp.logical_and(i % blocks_per_seq == 0, i < n_prompt_blocks))
    def _():
        sc_ref[7:8, :] = pshift_ref[pl.ds(i // blocks_per_seq, 1), :]

    prev = sc_ref[7:7 + tm, :]
    row = lax.broadcasted_iota(jnp.int32, (tm, 1), 0)
    use_state = jnp.logical_and(i >= n_prompt_blocks, jnp.bitwise_and(row, sample_seq - 1) == 0)
    prev = jnp.where(use_state, srows_ref[...], prev)
    sc_ref[7:8, :] = sc_ref[7 + tm:8 + tm, :]

    xx = prev - u
    xr = u + xx * mu_ref[0:1, :]
    xw = u + xx * mu_ref[1:2, :]
    xk = u + xx * mu_ref[2:3, :]
    xv = u + xx * mu_ref[3:4, :]
    xa = u + xx * mu_ref[4:5, :]
    xg = u + xx * mu_ref[5:6, :]
    r = _dot(xr, wr_ref[...])
    k = _dot(xk, wk_ref[...])
    v = _dot(xv, wv_ref[...])
    w = -_softplus(-(w0_ref[...] + _dot(jnp.tanh(_dot(xw, w1_ref[...])), w2_ref[...]))) - 0.5
    a = jax.nn.sigmoid(a0_ref[...] + _dot(_dot(xa, a1_ref[...]), a2_ref[...]))
    g = _dot(jax.nn.sigmoid(_dot(xg, g1_ref[...])), g2_ref[...])
    npair = r_o.shape[0]
    if has_vres:
        vf = jnp.concatenate([vf_ref[p] for p in range(npair)], axis=1)
        v = v + (vf - v) * jax.nn.sigmoid(v0_ref[...] + _dot(_dot(xv, v1_ref[...]), v2_ref[...]))
    kkr = k * kk_ref[...]
    k = k * (1.0 + (a - 1.0) * ka_ref[...])
    for p in range(npair):
        sl = slice(p * LANES, (p + 1) * LANES)
        r_o[p] = r[:, sl]
        k_o[p] = k[:, sl]
        v_o[p] = v[:, sl]
        kkr_o[p] = kkr[:, sl]
        a_o[p] = a[:, sl]
        w_o[p] = w[:, sl]
        g_o[p] = g[:, sl]


def _pad_cols(w, n):
    return jnp.pad(w, ((0, 0), (0, n - w.shape[1])))


def _pad_rows(w, n):
    return jnp.pad(w, ((0, n - w.shape[0]), (0, 0)))


def _round_up(n, m):
    return -(-n // m) * m


def _r_proj(h, gain, pshift, srows, rp, v_first, L, n_prompt_rows, sample_seq, *, tm=256):
    T, D = h.shape
    npair = D // LANES
    has_vres = v_first is not None
    row2 = lambda i: (i, 0)
    npb = n_prompt_rows // tm
    lw = _round_up(rp["w1"].shape[1], LANES)
    la = _round_up(rp["a1"].shape[1], LANES)
    lg = _round_up(rp["g1"].shape[1], LANES)
    ops = [h, gain.reshape(1, D), pshift, srows, rp["mu"],
           rp["wr"].astype(BF16), rp["wk"].astype(BF16), rp["wv"].astype(BF16),
           _pad_cols(rp["w1"], lw).astype(BF16), _pad_rows(rp["w2"], lw).astype(BF16),
           _pad_cols(rp["a1"], la).astype(BF16), _pad_rows(rp["a2"], la).astype(BF16),
           _pad_cols(rp["g1"], lg).astype(BF16), _pad_rows(rp["g2"], lg).astype(BF16),
           rp["w0"].reshape(1, D), rp["a0"].reshape(1, D), rp["k_k"].reshape(1, D), rp["k_a"].reshape(1, D)]
    specs = [pl.BlockSpec((tm, D), row2), _resident((1, D)), _resident(pshift.shape),
             pl.BlockSpec((tm, D), lambda i: (jnp.maximum(i - npb, 0), 0)), _resident((6, D)),
             _resident((D, D)), _resident((D, D)), _resident((D, D)),
             _resident((D, lw)), _resident((lw, D)), _resident((D, la)), _resident((la, D)),
             _resident((D, lg)), _resident((lg, D)),
             _resident((1, D)), _resident((1, D)), _resident((1, D)), _resident((1, D))]
    pair_spec = pl.BlockSpec((npair, tm, LANES), lambda i: (0, i, 0))
    if has_vres:
        lv = _round_up(rp["v1"].shape[1], LANES)
        ops += [v_first, rp["v0"].reshape(1, D), _pad_cols(rp["v1"], lv).astype(BF16),
                _pad_rows(rp["v2"], lv).astype(BF16)]
        specs += [pair_spec, _resident((1, D)), _resident((D, lv)), _resident((lv, D))]
    pair_shape = jax.ShapeDtypeStruct((npair, T, LANES), F32)
    kern = functools.partial(_r_proj_kernel, tm=tm, blocks_per_seq=L // tm, n_prompt_blocks=npb,
                             sample_seq=sample_seq, has_vres=has_vres)
    return pl.pallas_call(
        kern,
        out_shape=(pair_shape,) * 7 + (jax.ShapeDtypeStruct((T, D), F32),),
        grid=(T // tm,),
        in_specs=specs,
        out_specs=(pair_spec,) * 7 + (pl.BlockSpec((tm, D), row2),),
        scratch_shapes=[pltpu.VMEM((tm + 8, D), F32)],
        compiler_params=_cparams(1),
        name="r_proj",
    )(*ops)


def _wkv_pair(r, k, v, kkr, a, w, g, rk, gnw, gnb, states, lseq):
    C = r.shape[0]
    ns = C // lseq
    lane = lax.broadcasted_iota(jnp.int32, (1, LANES), 1)
    m0 = lane < R_HEADDIM

    def headsum(x):
        s0 = jnp.sum(jnp.where(m0, x, 0.0), axis=-1, keepdims=True)
        s1 = jnp.sum(jnp.where(m0, 0.0, x), axis=-1, keepdims=True)
        return jnp.where(m0, s0, s1)

    kk = kkr * lax.rsqrt(jnp.maximum(headsum(kkr * kkr), 1e-24))
    alpha = -kk
    beta = kk * a
    logd = -jnp.exp(w)

    ti = lax.broadcasted_iota(jnp.int32, (C, C), 0)
    si = lax.broadcasted_iota(jnp.int32, (C, C), 1)
    sh = int(math.log2(lseq))
    same = jnp.right_shift(ti, sh) == jnp.right_shift(si, sh)
    incl = jnp.logical_and(same, si <= ti)
    strict = jnp.logical_and(same, si < ti)
    lg = _dot_sel_l(incl, logd)
    ltot = _dot_sel_l(same, logd)
    g_in = jnp.exp(lg)
    ginv = jnp.exp(-lg)
    gend = jnp.exp(ltot - lg)
    w0 = alpha * jnp.exp(lg - logd)
    rg = r * g_in
    bi = beta * ginv
    ki = k * ginv
    wr = jnp.concatenate([w0, rg], axis=0)
    rsel = jnp.right_shift(jnp.bitwise_and(lax.broadcasted_iota(jnp.int32, (2 * C, 1), 0), C - 1), sh)

    xy0 = None
    for s in range(ns):
        t = _dot_nt(wr, states[s])
        if ns > 1:
            t = jnp.where(rsel == s, t, 0.0)
        xy0 = t if xy0 is None else xy0 + t
    x0 = xy0[:C]
    y0 = xy0[C:]

    eye = (ti == si).astype(F32)
    u = None
    y = None
    for hh in range(2):
        mh = m0 if hh == 0 else jnp.logical_not(m0)
        wrm = jnp.where(mh, wr, 0.0)
        gb = _dot_nt(wrm, bi)
        gk = _dot_nt(wrm, ki)
        a_ab = jnp.where(strict, gb[:C], 0.0)
        a_ak = jnp.where(strict, gk[:C], 0.0)
        b_rb = jnp.where(incl, gb[C:], 0.0)
        b_rk = jnp.where(incl, gk[C:], 0.0)
        tinv = eye + a_ab
        pw = a_ab
        for _ in range(int(math.log2(lseq)) - 1):
            pw = _dot(pw, pw)
            tinv = tinv + _dot(tinv, pw)
        uh = _dot(tinv, x0 + _dot(a_ak, v))
        yh = y0 + _dot(b_rb, uh) + _dot(b_rk, v)
        u = uh if u is None else jnp.where(m0, u, uh)
        y = yh if y is None else jnp.where(m0, y, yh)

    uvt = jnp.concatenate([u, v], axis=0).T
    bkd = jnp.concatenate([beta * gend, k * gend], axis=0)
    bi_r = lax.broadcasted_iota(jnp.int32, (LANES, LANES), 0) >= R_HEADDIM
    bi_c = lax.broadcasted_iota(jnp.int32, (LANES, LANES), 1) >= R_HEADDIM
    same_head = bi_r == bi_c
    new_states = []
    for s in range(ns):
        rhs = bkd if ns == 1 else jnp.where(rsel == s, bkd, 0.0)
        ds = jnp.where(same_head, _dot(uvt, rhs), 0.0)
        new_states.append(states[s] * jnp.exp(ltot[s * lseq:s * lseq + 1, :]) + ds)

    mean = headsum(y) * (1.0 / R_HEADDIM)
    d = y - mean
    var = headsum(d * d) * (1.0 / R_HEADDIM)
    yn = d * lax.rsqrt(var + R_GN_EPS) * gnw + gnb
    out = (yn + headsum(r * k * rk) * v) * g
    return out, new_states


def _wkv_pair_chunk(r, k, v, kkr, a, w, g, rk, gnw, gnb, state_ref, y_ref):
    C = r.shape[0]
    H = 2 * C
    lane = lax.broadcasted_iota(jnp.int32, (1, LANES), 1)
    m0 = lane < R_HEADDIM

    def headsum(x):
        s0 = jnp.sum(jnp.where(m0, x, 0.0), axis=-1, keepdims=True)
        s1 = jnp.sum(jnp.where(m0, 0.0, x), axis=-1, keepdims=True)
        return jnp.where(m0, s0, s1)

    def stack(x):
        return jnp.concatenate([jnp.where(m0, x, 0.0), jnp.where(m0, 0.0, x)], axis=0)

    kk = kkr * lax.rsqrt(jnp.maximum(headsum(kkr * kkr), 1e-24))
    beta = kk * a
    logd = -jnp.exp(w)
    ti = lax.broadcasted_iota(jnp.int32, (C, C), 0)
    si = lax.broadcasted_iota(jnp.int32, (C, C), 1)
    lg = _dot_sel_l(si <= ti, logd)
    yield
    ltot = lg[C - 1:C, :]
    ginv = jnp.exp(-lg)
    gend = jnp.exp(ltot - lg)
    w0s = stack(-kk * jnp.exp(lg - logd))
    rgs = stack(r * jnp.exp(lg))
    vs = stack(v)
    bds = stack(beta * gend)
    kds = stack(k * gend)
    gram = _dot_nt(jnp.concatenate([w0s, rgs], axis=0),
                   jnp.concatenate([stack(beta * ginv), stack(k * ginv)], axis=0))
    yield
    tr = jnp.bitwise_and(lax.broadcasted_iota(jnp.int32, (H, H), 0), C - 1)
    sc = jnp.bitwise_and(lax.broadcasted_iota(jnp.int32, (H, H), 1), C - 1)
    strict = sc < tr
    incl = sc <= tr
    a_ab = jnp.where(strict, gram[:H, :H], 0.0)
    a_ak = jnp.where(strict, gram[:H, H:], 0.0)
    b_rb = jnp.where(incl, gram[H:, :H], 0.0)
    b_rk = jnp.where(incl, gram[H:, H:], 0.0)
    eye = (lax.broadcasted_iota(jnp.int32, (H, H), 0) == lax.broadcasted_iota(jnp.int32, (H, H), 1)).astype(F32)
    tinv = eye + a_ab
    pw = _dot(a_ab, a_ab)
    av = _dot(a_ak, vs)
    bv = _dot(b_rk, vs)
    yield
    for _ in range(int(math.log2(C)) - 2):
        tinv = tinv + _dot(tinv, pw)
        pw = _dot(pw, pw)
        yield
    tinv = tinv + _dot(tinv, pw)
    yield
    tw = _dot(tinv, jnp.concatenate([w0s, av], axis=1))
    yield
    br = _dot(b_rb, tw)
    m_t = _dot(tw[:, :LANES].T, bds)
    z = _dot(jnp.concatenate([tw[:, LANES:], vs], axis=0).T, jnp.concatenate([bds, kds], axis=0))
    yield
    r_t = rgs + br[:, :LANES]
    y_t = br[:, LANES:] + bv
    state = state_ref[...]
    ys = _dot_nt(r_t, state) + y_t
    state_ref[...] = state * jnp.exp(ltot) + _dot(state, m_t) + z
    yield
    y = ys[:C] + ys[C:]
    mean = headsum(y) * (1.0 / R_HEADDIM)
    d = y - mean
    var = headsum(d * d) * (1.0 / R_HEADDIM)
    yn = d * lax.rsqrt(var + R_GN_EPS) * gnw + gnb
    y_ref[...] = ((yn + headsum(r * k * rk) * v) * g).astype(y_ref.dtype)


def _interleave(generators):
    live = list(generators)
    while live:
        nxt = []
        for gen in live:
            try:
                next(gen)
                nxt.append(gen)
            except StopIteration:
                pass
        live = nxt


def _wkv_prompt_kernel(r_ref, k_ref, v_ref, kkr_ref, a_ref, w_ref, g_ref, rk_ref, gnw_ref, gnb_ref,
                       y_ref, s_out_ref, s_ref, *, chunks_per_seq):
    c = pl.program_id(0) % chunks_per_seq
    npair = r_ref.shape[0]

    @pl.when(c == 0)
    def _():
        s_ref[...] = jnp.zeros(s_ref.shape, F32)

    _interleave(_wkv_pair_chunk(r_ref[p], k_ref[p], v_ref[p], kkr_ref[p], a_ref[p], w_ref[p], g_ref[p],
                                rk_ref[p], gnw_ref[p], gnb_ref[p], s_ref.at[p], y_ref.at[p])
                for p in range(npair))

    @pl.when(c == chunks_per_seq - 1)
    def _():
        for p in range(npair):
            s = s_ref[p]
            s_out_ref[0, p] = s[:, :R_HEADDIM] + s[:, R_HEADDIM:]


def _wkv_prompt(streams, rk, gnw, gnb, B, L):
    npair, T, _ = streams[0].shape
    C = R_CHUNK
    cps = L // C
    blk = pl.BlockSpec((npair, C, LANES), lambda i: (0, i, 0))
    vec = _resident((npair, 1, LANES))
    return pl.pallas_call(
        functools.partial(_wkv_prompt_kernel, chunks_per_seq=cps),
        out_shape=(jax.ShapeDtypeStruct((npair, T, LANES), BF16),
                   jax.ShapeDtypeStruct((B, npair, LANES, R_HEADDIM), F32)),
        grid=(B * cps,),
        in_specs=[blk] * 7 + [vec] * 3,
        out_specs=(blk, pl.BlockSpec((1, npair, LANES, R_HEADDIM), lambda i: (i // cps, 0, 0, 0))),
        scratch_shapes=[pltpu.VMEM((npair, LANES, LANES), F32)],
        compiler_params=_cparams(1),
        name="wkv_prompt",
    )(*streams, rk, gnw, gnb)


def _wkv_sample_kernel(y_in_ref, r_ref, k_ref, v_ref, kkr_ref, a_ref, w_ref, g_ref, rk_ref, gnw_ref, gnb_ref,
                       s0_ref, y_ref, s_out_ref, *, seq):
    del y_in_ref
    npair = r_ref.shape[0]
    ns = R_CHUNK // seq
    lane = lax.broadcasted_iota(jnp.int32, (LANES, LANES), 1) >= R_HEADDIM
    rowh = lax.broadcasted_iota(jnp.int32, (LANES, LANES), 0) >= R_HEADDIM
    same_head = lane == rowh

    def pair(p, carry):
        states = []
        for s in range(ns):
            s2 = s0_ref[s, p]
            states.append(jnp.where(same_head, jnp.concatenate([s2, s2], axis=1), 0.0))
        out, new = _wkv_pair(r_ref[p], k_ref[p], v_ref[p], kkr_ref[p], a_ref[p], w_ref[p], g_ref[p],
                             rk_ref[p], gnw_ref[p], gnb_ref[p], states, seq)
        y_ref[p] = out.astype(y_ref.dtype)
        for s in range(ns):
            s_out_ref[s, p] = new[s][:, :R_HEADDIM] + new[s][:, R_HEADDIM:]
        return carry

    lax.fori_loop(0, npair, pair, 0)


def _wkv_sample(y_all, streams, rk, gnw, gnb, s0, row0, seq):
    npair, T, _ = streams[0].shape
    C = R_CHUNK
    ns = C // seq
    DB = s0.shape[0]
    blk0 = row0 // C
    blk = pl.BlockSpec((npair, C, LANES), lambda i: (0, blk0 + i, 0))
    vec = _resident((npair, 1, LANES))
    sblk = pl.BlockSpec((ns, npair, LANES, R_HEADDIM), lambda i: (i, 0, 0, 0))
    return pl.pallas_call(
        functools.partial(_wkv_sample_kernel, seq=seq),
        out_shape=(jax.ShapeDtypeStruct((npair, T, LANES), BF16), jax.ShapeDtypeStruct(s0.shape, F32)),
        grid=(DB // ns,),
        in_specs=[pl.BlockSpec(memory_space=pl.ANY)] + [blk] * 7 + [vec] * 3 + [sblk],
        out_specs=(blk, sblk),
        input_output_aliases={0: 0},
        compiler_params=_cparams(1),
        name="wkv_sample",
    )(y_all, *streams, rk, gnw, gnb, s0)


def _r_out_kernel(h_ref, y_ref, w_ref, o_ref):
    y = jnp.concatenate([y_ref[p] for p in range(y_ref.shape[0])], axis=1)
    o_ref[...] = h_ref[...] + jnp.dot(y, w_ref[...], preferred_element_type=F32)


def _r_out(h, y, w, *, tm=512):
    T, D = h.shape
    npair = y.shape[0]
    return pl.pallas_call(
        _r_out_kernel,
        out_shape=jax.ShapeDtypeStruct((T, D), F32),
        grid=(T // tm,),
        in_specs=[pl.BlockSpec((tm, D), lambda i: (i, 0)), pl.BlockSpec((npair, tm, LANES), lambda i: (0, i, 0)),
                  _resident((D, D))],
        out_specs=pl.BlockSpec((tm, D), lambda i: (i, 0)),
        compiler_params=_cparams(1),
        name="r_out",
    )(h, y, w.astype(BF16))


def kernel(x_prompt, x_sample, p_prompt, p_sample, state_ssm, state_conv, state_wkv, state_shift, norm_ffn1, ffn1_gate_up, ffn1_down, norm_mix, norm_ffn2, ffn2_gate_up, ffn2_down, norm_ple, ple_in, ple_gate, norm_final, m_in_proj, m_conv_w, m_conv_b, m_dt_bias, m_A_log, m_D, m_norm, m_out_proj, r_mu, r_wr, r_wk, r_wv, r_wo, r_w0, r_w1, r_w2, r_a0, r_a1, r_a2, r_g1, r_g2, r_k_k, r_k_a, r_r_k, r_gn_w, r_gn_b, r_v0, r_v1, r_v2):
    B, L, D = x_prompt.shape
    DB, DL, _ = x_sample.shape
    depth = norm_ffn1.shape[0]
    TP, TS = B * L, DB * DL
    n_heads = m_A_log.shape[1]
    d_inner = n_heads * M_HEADDIM
    conv_dim = m_conv_w.shape[2]
    r_heads = D // R_HEADDIM
    npair = D // LANES

    h = jnp.concatenate([x_prompt.reshape(TP, D), x_sample.reshape(TS, D)], axis=0)
    p_all = jnp.concatenate([p_prompt.reshape(depth, TP, -1), p_sample.reshape(depth, TS, -1)], axis=1)
    e64, e128 = _head_expanders(n_heads)

    ssm_p, conv_p, ssm_s, conv_s = [], [], [], []
    wkv_p, shift_p, wkv_s, shift_s = [], [], [], []
    v_first = None
    for i in range(depth):
        j = i // 2
        h = _ffn(h, norm_ffn1[i], ffn1_gate_up[i], ffn1_down[i])
        if i % 2 == 0:
            z, xbc, dt = _m_in_proj(h, norm_mix[i], m_in_proj[j], d_inner, conv_dim, n_heads)
            prm = dict(conv_w=m_conv_w[j], conv_b=m_conv_b[j].reshape(1, conv_dim),
                       dt_bias=jnp.pad(m_dt_bias[j], (0, LANES - n_heads)).reshape(1, LANES),
                       a_log=jnp.pad(m_A_log[j], (0, LANES - n_heads)).reshape(1, LANES),
                       d_full=jnp.repeat(m_D[j], M_HEADDIM).reshape(1, d_inner),
                       norm_w=m_norm[j].reshape(1, d_inner), e64=e64, e128=e128)
            y, s_p, c_p = _ssd_prompt(z, xbc, dt, prm, B, L, n_heads)
            y, s_s, c_s = _ssd_sample(y, z, xbc, dt, state_ssm[j], state_conv[j], prm, TP, DL, n_heads)
            ssm_p.append(s_p)
            conv_p.append(c_p)
            ssm_s.append(s_s)
            conv_s.append(c_s)
            h = _out_proj(h, y, m_out_proj[j])
        else:
            rp = dict(mu=r_mu[j], wr=r_wr[j], wk=r_wk[j], wv=r_wv[j], w0=r_w0[j], w1=r_w1[j], w2=r_w2[j],
                      a0=r_a0[j], a1=r_a1[j], a2=r_a2[j], g1=r_g1[j], g2=r_g2[j], k_k=r_k_k[j], k_a=r_k_a[j])
            if j > 0:
                rp.update(v0=r_v0[j - 1], v1=r_v1[j - 1], v2=r_v2[j - 1])
            pshift = jnp.zeros((B, D), F32)
            srows = jnp.zeros((DB, DL, D), F32).at[:, 0].set(state_shift[j]).reshape(TS, D)
            outs = _r_proj(h, norm_mix[i], pshift, srows, rp, v_first if j > 0 else None, L, TP, DL)
            streams, u = outs[:7], outs[7]
            if j == 0:
                v_first = streams[2]
            rk = r_r_k[j].reshape(npair, 1, LANES)
            gnw = r_gn_w[j].reshape(npair, 1, LANES)
            gnb = r_gn_b[j].reshape(npair, 1, LANES)
            y, s_p = _wkv_prompt(streams, rk, gnw, gnb, B, L)
            s0 = state_wkv[j].reshape(DB, npair, LANES, R_HEADDIM)
            y, s_s = _wkv_sample(y, streams, rk, gnw, gnb, s0, TP, DL)
            wkv_p.append(s_p.reshape(B, r_heads, R_HEADDIM, R_HEADDIM))
            wkv_s.append(s_s.reshape(DB, r_heads, R_HEADDIM, R_HEADDIM))
            shift_p.append(u[:TP].reshape(B, L, D)[:, L - 1])
            shift_s.append(u[TP:].reshape(DB, DL, D)[:, DL - 1])
            h = _r_out(h, y, r_wo[j])
        h = _ffn(h, norm_ffn2[i], ffn2_gate_up[i], ffn2_down[i])
        h = _ple(h, p_all[i], norm_ple[i], ple_in[i], ple_gate[i], norm_final, final=(i == depth - 1))

    return (h[:TP].reshape(B, L, D), h[TP:].reshape(DB, DL, D),
            jnp.stack(ssm_p), jnp.stack(conv_p), jnp.stack(wkv_p), jnp.stack(shift_p),
            jnp.stack(ssm_s), jnp.stack(conv_s), jnp.stack(wkv_s), jnp.stack(shift_s))
```

```python
import functools
import math

import jax
import jax.numpy as jnp
from jax import lax
from jax.experimental import pallas as pl
from jax.experimental.pallas import tpu as pltpu

F32 = jnp.float32
BF16 = jnp.bfloat16

NORM_EPS = 1e-6
M_NORM_EPS = 1e-5
R_GN_EPS = 64e-5

LANES = 128
M_HEADDIM = 64
M_STATE = 128
M_GROUPS = 8
M_CHUNK = 128
M_CONV_W = 4
R_HEADDIM = 64
R_CHUNK = 64
SAMPLE_SLOT_ROWS = 16
VMEM_LIMIT_BYTES = 56 * 1024 * 1024


def _cparams(n_axes):
    return pltpu.CompilerParams(dimension_semantics=("arbitrary",) * n_axes,
                                vmem_limit_bytes=VMEM_LIMIT_BYTES)


def _resident(shape):
    zeros = (0,) * len(shape)
    return pl.BlockSpec(shape, lambda *_: zeros, pipeline_mode=pl.Buffered(1))


def _dot(a, b):
    return jnp.dot(a.astype(BF16), b.astype(BF16), preferred_element_type=F32)


def _dot_nt(a, b):
    return lax.dot_general(a.astype(BF16), b.astype(BF16), (((1,), (1,)), ((), ())),
                           preferred_element_type=F32)


def _split3(x):
    hi = x.astype(BF16)
    r1 = x - hi.astype(F32)
    mid = r1.astype(BF16)
    lo = (r1 - mid.astype(F32)).astype(BF16)
    return hi, mid, lo


def _dot_sel_l(sel, x):
    s = sel.astype(BF16)
    return jnp.dot(jnp.concatenate([s, s, s], axis=1), jnp.concatenate(_split3(x), axis=0),
                   preferred_element_type=F32)


def _dot_sel_r(x, sel3):
    return jnp.dot(jnp.concatenate(_split3(x), axis=1), sel3, preferred_element_type=F32)


def _rms(x, g):
    ms = jnp.mean(x * x, axis=-1, keepdims=True)
    return (x * lax.rsqrt(ms + NORM_EPS)) * g


def _silu(x):
    return x * jax.nn.sigmoid(x)


def _softplus(x):
    return jnp.maximum(x, 0.0) + jnp.log1p(jnp.exp(-jnp.abs(x)))


def _split_rows_specs(tm, width, n_first_blocks, lead=()):
    none = (None,) * len(lead)
    first = pl.BlockSpec(none + (tm, width), lambda i: lead + (jnp.minimum(i, n_first_blocks - 1), 0))
    second = pl.BlockSpec(none + (tm, width), lambda i: lead + (jnp.maximum(i - n_first_blocks, 0), 0))
    return first, second


def _pick_rows(first_ref, second_ref, n_first_blocks):
    return jnp.where(pl.program_id(0) < n_first_blocks, first_ref[...], second_ref[...])


def _ffn_kernel(*refs, fc, n_first_blocks):
    if n_first_blocks is None:
        h_ref, g_ref, wgu_ref, wd_ref, o_ref, ub_ref = refs
        x = h_ref[...]
    else:
        hp_ref, hs_ref, g_ref, wgu_ref, wd_ref, o_ref, ub_ref = refs
        x = _pick_rows(hp_ref, hs_ref, n_first_blocks)
    ub_ref[...] = _rms(x, g_ref[...]).astype(BF16)
    F = wd_ref.shape[0]
    acc = x
    for c in range(F // fc):
        ub = ub_ref[...]
        gate = jnp.dot(ub, wgu_ref[:, c * fc:(c + 1) * fc], preferred_element_type=F32)
        up = jnp.dot(ub, wgu_ref[:, F + c * fc:F + (c + 1) * fc], preferred_element_type=F32)
        act = (_silu(gate) * up).astype(BF16)
        acc = acc + 0.5 * jnp.dot(act, wd_ref[c * fc:(c + 1) * fc, :], preferred_element_type=F32)
    o_ref[...] = acc


def _ffn(h, gain, w_gate_up, w_down, *, tm=512, fc=256):
    parts = h if isinstance(h, tuple) else (h,)
    T = sum(p.shape[0] for p in parts)
    D = parts[0].shape[1]
    F = w_down.shape[0]
    if len(parts) == 2:
        nfb = parts[0].shape[0] // tm
        h_specs = list(_split_rows_specs(tm, D, nfb))
    else:
        nfb = None
        h_specs = [pl.BlockSpec((tm, D), lambda i: (i, 0))]
    return pl.pallas_call(
        functools.partial(_ffn_kernel, fc=fc, n_first_blocks=nfb),
        out_shape=jax.ShapeDtypeStruct((T, D), F32),
        grid=(T // tm,),
        in_specs=h_specs + [_resident((1, D)), _resident((D, 2 * F)), _resident((F, D))],
        out_specs=pl.BlockSpec((tm, D), lambda i: (i, 0)),
        scratch_shapes=[pltpu.VMEM((tm, D), BF16)],
        compiler_params=_cparams(1),
        name="ffn",
    )(*parts, gain.reshape(1, D), w_gate_up.astype(BF16), w_down.astype(BF16))


def _ple_kernel(*refs, final, n_first_blocks):
    if final:
        h_ref, pp_ref, ps_ref, g_ref, wi_ref, wg_ref, gf_ref, op_ref, os_ref = refs
    else:
        h_ref, pp_ref, ps_ref, g_ref, wi_ref, wg_ref, o_ref = refs
    x = h_ref[...]
    gate = jax.nn.sigmoid(_dot(_rms(x, g_ref[...]), wg_ref[...]))
    y = x + _dot(_pick_rows(pp_ref, ps_ref, n_first_blocks), wi_ref[...]) * gate
    if not final:
        o_ref[...] = y
        return
    y = _rms(y, gf_ref[...])
    i = pl.program_id(0)

    @pl.when(i < n_first_blocks)
    def _():
        op_ref[...] = y

    @pl.when(i >= n_first_blocks)
    def _():
        os_ref[...] = y


def _ple(h, p_first, p_second, layer, gain, w_in, w_gate, gain_final, *, final, tm=512):
    T, D = h.shape
    P = p_first.shape[2]
    nfb = p_first.shape[1] // tm
    row = pl.BlockSpec((tm, D), lambda i: (i, 0))
    ops = [h, p_first, p_second, gain.reshape(1, D), w_in.astype(BF16), w_gate.astype(BF16)]
    specs = [row, *_split_rows_specs(tm, P, nfb, lead=(layer,)), _resident((1, D)), _resident((P, D)),
             _resident((D, D))]
    if final:
        ops.append(gain_final.reshape(1, D))
        specs.append(_resident((1, D)))
        out_shape = (jax.ShapeDtypeStruct((p_first.shape[1], D), F32), jax.ShapeDtypeStruct((p_second.shape[1], D), F32))
        out_specs = _split_rows_specs(tm, D, nfb)
    else:
        out_shape = jax.ShapeDtypeStruct((T, D), F32)
        out_specs = row
    return pl.pallas_call(
        functools.partial(_ple_kernel, final=final, n_first_blocks=nfb),
        out_shape=out_shape,
        grid=(T // tm,),
        in_specs=specs,
        out_specs=out_specs,
        compiler_params=_cparams(1),
        name="ple",
    )(*ops)


def _out_proj_kernel(h_ref, y_ref, w_ref, o_ref):
    o_ref[...] = h_ref[...] + _dot(y_ref[...], w_ref[...])


def _out_proj(h, y, w, *, tm=512):
    T, D = h.shape
    K = y.shape[1]
    return pl.pallas_call(
        _out_proj_kernel,
        out_shape=jax.ShapeDtypeStruct((T, D), F32),
        grid=(T // tm,),
        in_specs=[pl.BlockSpec((tm, D), lambda i: (i, 0)), pl.BlockSpec((tm, K), lambda i: (i, 0)),
                  _resident((K, D))],
        out_specs=pl.BlockSpec((tm, D), lambda i: (i, 0)),
        compiler_params=_cparams(1),
        name="out_proj",
    )(h, y, w.astype(BF16))


def _m_in_kernel(h_ref, g_ref, wz_ref, wx_ref, wdt_ref, z_ref, xbc_ref, dt_ref):
    u = _rms(h_ref[...], g_ref[...]).astype(BF16)
    z_ref[...] = jnp.dot(u, wz_ref[...], preferred_element_type=F32)
    xbc_ref[...] = jnp.dot(u, wx_ref[...], preferred_element_type=F32)
    dt_ref[...] = jnp.dot(u, wdt_ref[...], preferred_element_type=F32)


def _m_in_proj(h, gain, w_in, d_inner, conv_dim, n_heads, *, tm=256):
    T, D = h.shape
    wz = w_in[:, :d_inner].astype(BF16)
    wx = w_in[:, d_inner:d_inner + conv_dim].astype(BF16)
    wdt = jnp.pad(w_in[:, d_inner + conv_dim:], ((0, 0), (0, LANES - n_heads))).astype(BF16)
    return pl.pallas_call(
        _m_in_kernel,
        out_shape=(jax.ShapeDtypeStruct((T, d_inner), F32), jax.ShapeDtypeStruct((T, conv_dim), F32),
                   jax.ShapeDtypeStruct((T, LANES), F32)),
        grid=(T // tm,),
        in_specs=[pl.BlockSpec((tm, D), lambda i: (i, 0)), _resident((1, D)),
                  _resident((D, d_inner)), _resident((D, conv_dim)), _resident((D, LANES))],
        out_specs=(pl.BlockSpec((tm, d_inner), lambda i: (i, 0)), pl.BlockSpec((tm, conv_dim), lambda i: (i, 0)),
                   pl.BlockSpec((tm, LANES), lambda i: (i, 0))),
        compiler_params=_cparams(1),
        name="m_in_proj",
    )(h, gain.reshape(1, D), wz, wx, wdt)


def _head_expanders(n_heads):
    hh = jnp.arange(LANES)[:, None]
    e64 = (hh == (jnp.arange(n_heads * M_HEADDIM)[None, :] // M_HEADDIM)).astype(BF16)
    e128 = (hh == (jnp.arange(n_heads * LANES)[None, :] // LANES)).astype(BF16)
    return jnp.tile(e64, (3, 1)), jnp.tile(e128, (3, 1))


def _ssd_prompt_kernel(*refs, n_heads, n_aliased):
    (z_ref, xbc_ref, dt_ref, cw_ref, cb_ref, dtb_ref, alog_ref, dsk_ref, nw_ref,
     e64_ref, e128_ref, y_ref, ssm_ref, conv_ref, xpad_ref, s_ref) = refs[n_aliased:]
    c = pl.program_id(1)
    nc = pl.num_programs(1)
    Q = M_CHUNK
    d_inner = n_heads * M_HEADDIM
    gw = d_inner // M_GROUPS
    bc0 = d_inner
    cc0 = d_inner + M_GROUPS * M_STATE

    @pl.when(c == 0)
    def _():
        xpad_ref[0:8, :] = jnp.zeros((8, xpad_ref.shape[1]), F32)
        s_ref[...] = jnp.zeros(s_ref.shape, F32)

    xraw = xbc_ref[...]
    xpad_ref[8:8 + Q, :] = xraw
    xc = (xpad_ref[5:5 + Q, :] * cw_ref[0:1, :] + xpad_ref[6:6 + Q, :] * cw_ref[1:2, :]
          + xpad_ref[7:7 + Q, :] * cw_ref[2:3, :] + xraw * cw_ref[3:4, :] + cb_ref[...])
    xpad_ref[0:8, :] = xraw[Q - 8:Q, :]
    xs = _silu(xc)
    xh = xs[:, :d_inner]

    dt = _softplus(dt_ref[...] + dtb_ref[...])
    dta = dt * (-jnp.exp(alog_ref[...]))
    ri = lax.broadcasted_iota(jnp.int32, (Q, Q), 0)
    ci = lax.broadcasted_iota(jnp.int32, (Q, Q), 1)
    causal = ri >= ci
    acum = _dot_sel_l(causal, dta)
    a_full = _dot_sel_r(acum, e64_ref[...])
    dt_full = _dot_sel_r(dt, e64_ref[...])
    a_col = _dot_sel_r(acum, e128_ref[...])
    acum_t = acum.T
    a_last = a_full[Q - 1:Q, :]
    exp_a = jnp.exp(a_full)
    xdt = xh * dt_full
    xw = xdt * jnp.exp(a_last - a_full)
    chunk_decay = jnp.exp(a_last)
    lane = lax.broadcasted_iota(jnp.int32, (Q, LANES), 1)
    first_head = lane < M_HEADDIM
    hpg = n_heads // M_GROUPS

    for g in range(M_GROUPS):
        bg = xs[:, bc0 + g * M_STATE: bc0 + (g + 1) * M_STATE]
        cg = xs[:, cc0 + g * M_STATE: cc0 + (g + 1) * M_STATE]
        cb = _dot_nt(cg, bg)
        sl = slice(g * gw, (g + 1) * gw)
        s_g = s_ref[:, sl]
        y_g = _dot(cg, s_g) * exp_a[:, sl]
        parts = []
        for q in range(hpg // 2):
            h0 = g * hpg + 2 * q
            xp = xdt[:, h0 * M_HEADDIM: h0 * M_HEADDIM + LANES]
            ys = []
            for hh in (h0, h0 + 1):
                seg = a_col[:, hh * LANES:(hh + 1) * LANES] - acum_t[hh:hh + 1, :]
                m = cb * jnp.exp(jnp.where(causal, seg, -jnp.inf))
                ys.append(_dot(m, xp))
            parts.append(jnp.where(first_head, ys[0], ys[1]))
        y_g = y_g + jnp.concatenate(parts, axis=1) + dsk_ref[:, sl] * xh[:, sl]
        y_g = y_g * _silu(z_ref[:, sl])
        y_g = y_g * lax.rsqrt(jnp.mean(y_g * y_g, axis=-1, keepdims=True) + M_NORM_EPS)
        y_ref[:, sl] = y_g * nw_ref[:, sl]
        s_ref[:, sl] = s_g * chunk_decay[:, sl] + _dot(bg.T, xw[:, sl])

    @pl.when(c == nc - 1)
    def _():
        ssm_ref[0] = s_ref[...].T.reshape(n_heads, M_HEADDIM, M_STATE)
        conv_ref[0] = xraw[Q - (M_CONV_W - 1):Q, :]


def _alias_prev(prev, first_out):
    prev = tuple(prev or ())
    specs = [pl.BlockSpec(memory_space=pl.ANY)] * len(prev)
    return prev, specs, {k: first_out + k for k in range(len(prev))}


def _ssd_prompt(z, xbc, dt, prm, B, L, n_heads, layer, n_layers, prev):
    T, d_inner = z.shape
    conv_dim = xbc.shape[1]
    Q = M_CHUNK
    nc = L // Q
    row = lambda b, c: (b * nc + c, 0)
    prev, prev_specs, aliases = _alias_prev(prev, first_out=1)
    kern = functools.partial(_ssd_prompt_kernel, n_heads=n_heads, n_aliased=len(prev))
    return pl.pallas_call(
        kern,
        out_shape=(jax.ShapeDtypeStruct((T, d_inner), F32),
                   jax.ShapeDtypeStruct((n_layers, B, n_heads, M_HEADDIM, M_STATE), F32),
                   jax.ShapeDtypeStruct((n_layers, B, M_CONV_W - 1, conv_dim), F32)),
        grid=(B, nc),
        in_specs=prev_specs + [
            pl.BlockSpec((Q, d_inner), row), pl.BlockSpec((Q, conv_dim), row), pl.BlockSpec((Q, LANES), row),
            _resident((M_CONV_W, conv_dim)), _resident((1, conv_dim)), _resident((1, LANES)),
            _resident((1, LANES)), _resident((1, d_inner)), _resident((1, d_inner)),
            _resident((3 * LANES, d_inner)), _resident((3 * LANES, n_heads * LANES))],
        out_specs=(pl.BlockSpec((Q, d_inner), row),
                   pl.BlockSpec((None, 1, n_heads, M_HEADDIM, M_STATE), lambda b, c: (layer, b, 0, 0, 0)),
                   pl.BlockSpec((None, 1, M_CONV_W - 1, conv_dim), lambda b, c: (layer, b, 0, 0))),
        scratch_shapes=[pltpu.VMEM((Q + 8, conv_dim), F32), pltpu.VMEM((M_STATE, d_inner), F32)],
        input_output_aliases=aliases,
        compiler_params=_cparams(2),
        name="ssd_prompt",
    )(*prev, z, xbc, dt, prm["conv_w"], prm["conv_b"], prm["dt_bias"], prm["a_log"], prm["d_full"], prm["norm_w"],
      prm["e64"], prm["e128"])


def _ssd_sample_kernel(*refs, n_heads, seq, nb, n_aliased):
    (z_ref, xbc_ref, dt_ref, s0_ref, c0_ref, cw_ref, cb_ref, dtb_ref, alog_ref,
     dsk_ref, nw_ref, e64_ref, y_ref, ssm_ref, conv_ref, xpad_ref, zs_ref, dts_ref) = refs[n_aliased:]
    d_inner = n_heads * M_HEADDIM
    gw = d_inner // M_GROUPS
    bc0 = d_inner
    cc0 = d_inner + M_GROUPS * M_STATE
    hpg = n_heads // M_GROUPS
    R = SAMPLE_SLOT_ROWS
    x0 = 8
    row = lax.broadcasted_iota(jnp.int32, (R, 1), 0)
    valid = row < seq
    eye = (lax.broadcasted_iota(jnp.int32, (LANES, LANES), 0)
           == lax.broadcasted_iota(jnp.int32, (LANES, LANES), 1)).astype(BF16)

    xpad_ref[...] = jnp.zeros(xpad_ref.shape, F32)
    zs_ref[...] = jnp.zeros(zs_ref.shape, F32)
    dts_ref[...] = jnp.zeros(dts_ref.shape, F32)

    def shift_rows(x, s):
        return jnp.where(row >= s, pltpu.roll(x, s, 0), 0.0)

    for bi in range(nb):
        r0 = bi * seq
        xpad_ref[x0 - (M_CONV_W - 1):x0, :] = c0_ref[bi]
        xpad_ref[x0:x0 + seq, :] = xbc_ref[r0:r0 + seq, :]
        zs_ref[0:seq, :] = z_ref[r0:r0 + seq, :]
        dts_ref[0:seq, :] = dt_ref[r0:r0 + seq, :]
        xc = None
        for k in range(M_CONV_W):
            lo = x0 - (M_CONV_W - 1) + k
            term = xpad_ref[lo:lo + R, :] * cw_ref[k:k + 1, :]
            xc = term if xc is None else xc + term
        xc = xc + cb_ref[...]
        conv_ref[bi] = xpad_ref[x0 + seq - (M_CONV_W - 1):x0 + seq, :]
        xs = _silu(xc)
        xh = xs[:, :d_inner]

        dt = jnp.where(valid, _softplus(dts_ref[...] + dtb_ref[...]), 0.0)
        dta = dt * (-jnp.exp(alog_ref[...]))
        acum = dta
        s = 1
        while s < seq:
            acum = acum + shift_rows(acum, s)
            s *= 2
        a_full = _dot_sel_r(acum, e64_ref[...])
        dt_full = _dot_sel_r(dt, e64_ref[...])
        a_last = a_full[seq - 1:seq, :]
        exp_a = jnp.exp(a_full)
        xdt = xh * dt_full
        aux = jnp.where(valid, xdt * jnp.exp(a_last - a_full), 0.0)
        for k, piece in enumerate(_split3(jnp.exp(a_last))):
            aux = jnp.where(row == seq + k, piece.astype(F32), aux)
        col = lax.broadcasted_iota(jnp.int32, (1, R), 1)

        y = dsk_ref[...] * xh
        for j in range(seq):
            lj = jnp.where(row >= j, jnp.exp(a_full - a_full[j:j + 1, :]), 0.0)
            wj = lj * xdt[j:j + 1, :]
            parts = []
            for g in range(M_GROUPS):
                cgb = (xs[:, cc0 + g * M_STATE: cc0 + (g + 1) * M_STATE]
                       * xs[j:j + 1, bc0 + g * M_STATE: bc0 + (g + 1) * M_STATE])
                parts.append(jnp.sum(cgb, axis=-1, keepdims=True) * wj[:, g * gw:(g + 1) * gw])
            y = y + jnp.concatenate(parts, axis=1)

        inter = []
        for pr in range(n_heads // 2):
            g = (2 * pr) // hpg
            cg = xs[:, cc0 + g * M_STATE: cc0 + (g + 1) * M_STATE]
            bg = xs[:, bc0 + g * M_STATE: bc0 + (g + 1) * M_STATE]
            h0 = s0_ref[bi, 2 * pr:2 * pr + 2].reshape(2 * M_HEADDIM, M_STATE)
            inter.append(_dot_nt(cg, h0))
            aux_t = _dot_nt(eye, aux[:, pr * LANES:(pr + 1) * LANES])
            dec = aux_t[:, seq:seq + 1] + aux_t[:, seq + 1:seq + 2] + aux_t[:, seq + 2:seq + 3]
            upd = _dot(jnp.where(col < seq, aux_t, 0.0), bg)
            ssm_ref[bi, 2 * pr:2 * pr + 2] = (h0 * dec + upd).reshape(2, M_HEADDIM, M_STATE)
        y = y + jnp.concatenate(inter, axis=1) * exp_a

        y = y * _silu(zs_ref[...])
        outs = []
        for g in range(M_GROUPS):
            yg = y[:, g * gw:(g + 1) * gw]
            outs.append(yg * lax.rsqrt(jnp.mean(yg * yg, axis=-1, keepdims=True) + M_NORM_EPS))
        yn = jnp.concatenate(outs, axis=1) * nw_ref[...]
        y_ref[r0:r0 + seq, :] = yn[0:seq, :]


def _ssd_sample(y_all, z, xbc, dt, s0, c0, prm, row0, seq, n_heads, layer, prev, *, nb=4):
    T, d_inner = z.shape
    conv_dim = xbc.shape[1]
    DB = s0.shape[1]
    assert seq + 3 <= SAMPLE_SLOT_ROWS and seq & (seq - 1) == 0
    rows = nb * seq
    blk0 = row0 // rows
    row = lambda i: (blk0 + i, 0)
    s_spec = pl.BlockSpec((None, nb, n_heads, M_HEADDIM, M_STATE), lambda i: (layer, i, 0, 0, 0))
    c_spec = pl.BlockSpec((None, nb, M_CONV_W - 1, conv_dim), lambda i: (layer, i, 0, 0))
    aliased, alias_specs, aliases = _alias_prev((y_all,) + tuple(prev or ()), first_out=0)
    kern = functools.partial(_ssd_sample_kernel, n_heads=n_heads, seq=seq, nb=nb, n_aliased=len(aliased))
    return pl.pallas_call(
        kern,
        out_shape=(jax.ShapeDtypeStruct((T, d_inner), F32),
                   jax.ShapeDtypeStruct(s0.shape, F32), jax.ShapeDtypeStruct(c0.shape, F32)),
        grid=(DB // nb,),
        in_specs=alias_specs + [
            pl.BlockSpec((rows, d_inner), row), pl.BlockSpec((rows, conv_dim), row),
            pl.BlockSpec((rows, LANES), row), s_spec, c_spec,
            _resident((M_CONV_W, conv_dim)), _resident((1, conv_dim)), _resident((1, LANES)),
            _resident((1, LANES)), _resident((1, d_inner)), _resident((1, d_inner)),
            _resident((3 * LANES, d_inner))],
        out_specs=(pl.BlockSpec((rows, d_inner), row), s_spec, c_spec),
        scratch_shapes=[pltpu.VMEM((8 + SAMPLE_SLOT_ROWS + 8, conv_dim), F32), pltpu.VMEM((SAMPLE_SLOT_ROWS, d_inner), F32),
                        pltpu.VMEM((SAMPLE_SLOT_ROWS, LANES), F32)],
        input_output_aliases=aliases,
        compiler_params=_cparams(1),
        name="ssd_sample",
    )(*aliased, z, xbc, dt, s0, c0, prm["conv_w"], prm["conv_b"], prm["dt_bias"], prm["a_log"], prm["d_full"],
      prm["norm_w"], prm["e64"])


def _r_proj_kernel(*refs, tm, blocks_per_seq, n_prompt_blocks, sample_seq, has_vres):
    if has_vres:
        (h_ref, g_ref, pshift_ref, srows_ref, mu_ref, wr_ref, wk_ref, wv_ref, w1_ref, w2_ref, a1_ref, a2_ref,
         g1_ref, g2_ref, w0_ref, a0_ref, kk_ref, ka_ref, vf_ref, v0_ref, v1_ref, v2_ref,
         r_o, k_o, v_o, kkr_o, a_o, w_o, g_o, u_o, sc_ref) = refs
    else:
        (h_ref, g_ref, pshift_ref, srows_ref, mu_ref, wr_ref, wk_ref, wv_ref, w1_ref, w2_ref, a1_ref, a2_ref,
         g1_ref, g2_ref, w0_ref, a0_ref, kk_ref, ka_ref,
         r_o, k_o, v_o, kkr_o, a_o, w_o, g_o, u_o, sc_ref) = refs
    i = pl.program_id(0)
    u = _rms(h_ref[...], g_ref[...])
    u_o[...] = u
    sc_ref[8:8 + tm, :] = u

    @pl.when(jnp.logical_and(i % blocks_per_seq == 0, i < n_prompt_blocks))
    def _():
        sc_ref[7:8, :] = pshift_ref[pl.ds(i // blocks_per_seq, 1), :]

    prev = sc_ref[7:7 + tm, :]
    row = lax.broadcasted_iota(jnp.int32, (tm, 1), 0)
    use_state = jnp.logical_and(i >= n_prompt_blocks, jnp.bitwise_and(row, sample_seq - 1) == 0)
    prev = jnp.where(use_state, srows_ref[...], prev)
    sc_ref[7:8, :] = sc_ref[7 + tm:8 + tm, :]

    xx = prev - u
    xr = u + xx * mu_ref[0:1, :]
    xw = u + xx * mu_ref[1:2, :]
    xk = u + xx * mu_ref[2:3, :]
    xv = u + xx * mu_ref[3:4, :]
    xa = u + xx * mu_ref[4:5, :]
    xg = u + xx * mu_ref[5:6, :]
    r = _dot(xr, wr_ref[...])
    k = _dot(xk, wk_ref[...])
    v = _dot(xv, wv_ref[...])
    w = -_softplus(-(w0_ref[...] + _dot(jnp.tanh(_dot(xw, w1_ref[...])), w2_ref[...]))) - 0.5
    a = jax.nn.sigmoid(a0_ref[...] + _dot(_dot(xa, a1_ref[...]), a2_ref[...]))
    g = _dot(jax.nn.sigmoid(_dot(xg, g1_ref[...])), g2_ref[...])
    npair = r_o.shape[0]
    if has_vres:
        vf = jnp.concatenate([vf_ref[p] for p in range(npair)], axis=1)
        v = v + (vf - v) * jax.nn.sigmoid(v0_ref[...] + _dot(_dot(xv, v1_ref[...]), v2_ref[...]))
    kkr = k * kk_ref[...]
    k = k * (1.0 + (a - 1.0) * ka_ref[...])
    for p in range(npair):
        sl = slice(p * LANES, (p + 1) * LANES)
        r_o[p] = r[:, sl]
        k_o[p] = k[:, sl]
        v_o[p] = v[:, sl]
        kkr_o[p] = kkr[:, sl]
        a_o[p] = a[:, sl]
        w_o[p] = w[:, sl]
        g_o[p] = g[:, sl]


def _pad_cols(w, n):
    return jnp.pad(w, ((0, 0), (0, n - w.shape[1])))


def _pad_rows(w, n):
    return jnp.pad(w, ((0, n - w.shape[0]), (0, 0)))


def _round_up(n, m):
    return -(-n // m) * m


def _r_proj(h, gain, pshift, srows, rp, v_first, L, n_prompt_rows, sample_seq, *, tm=256):
    T, D = h.shape
    npair = D // LANES
    has_vres = v_first is not None
    row2 = lambda i: (i, 0)
    npb = n_prompt_rows // tm
    lw = _round_up(rp["w1"].shape[1], LANES)
    la = _round_up(rp["a1"].shape[1], LANES)
    lg = _round_up(rp["g1"].shape[1], LANES)
    ops = [h, gain.reshape(1, D), pshift, srows, rp["mu"],
           rp["wr"].astype(BF16), rp["wk"].astype(BF16), rp["wv"].astype(BF16),
           _pad_cols(rp["w1"], lw).astype(BF16), _pad_rows(rp["w2"], lw).astype(BF16),
           _pad_cols(rp["a1"], la).astype(BF16), _pad_rows(rp["a2"], la).astype(BF16),
           _pad_cols(rp["g1"], lg).astype(BF16), _pad_rows(rp["g2"], lg).astype(BF16),
           rp["w0"].reshape(1, D), rp["a0"].reshape(1, D), rp["k_k"].reshape(1, D), rp["k_a"].reshape(1, D)]
    specs = [pl.BlockSpec((tm, D), row2), _resident((1, D)), _resident(pshift.shape),
             pl.BlockSpec((tm, D), lambda i: (jnp.maximum(i - npb, 0), 0)), _resident((6, D)),
             _resident((D, D)), _resident((D, D)), _resident((D, D)),
             _resident((D, lw)), _resident((lw, D)), _resident((D, la)), _resident((la, D)),
             _resident((D, lg)), _resident((lg, D)),
             _resident((1, D)), _resident((1, D)), _resident((1, D)), _resident((1, D))]
    pair_spec = pl.BlockSpec((npair, tm, LANES), lambda i: (0, i, 0))
    if has_vres:
        lv = _round_up(rp["v1"].shape[1], LANES)
        ops += [v_first, rp["v0"].reshape(1, D), _pad_cols(rp["v1"], lv).astype(BF16),
                _pad_rows(rp["v2"], lv).astype(BF16)]
        specs += [pair_spec, _resident((1, D)), _resident((D, lv)), _resident((lv, D))]
    pair_shape = jax.ShapeDtypeStruct((npair, T, LANES), F32)
    kern = functools.partial(_r_proj_kernel, tm=tm, blocks_per_seq=L // tm, n_prompt_blocks=npb,
                             sample_seq=sample_seq, has_vres=has_vres)
    return pl.pallas_call(
        kern,
        out_shape=(pair_shape,) * 7 + (jax.ShapeDtypeStruct((T, D), F32),),
        grid=(T // tm,),
        in_specs=specs,
        out_specs=(pair_spec,) * 7 + (pl.BlockSpec((tm, D), row2),),
        scratch_shapes=[pltpu.VMEM((tm + 8, D), F32)],
        compiler_params=_cparams(1),
        name="r_proj",
    )(*ops)


def _wkv_pair(r, k, v, kkr, a, w, g, rk, gnw, gnb, states, lseq):
    C = r.shape[0]
    ns = C // lseq
    lane = lax.broadcasted_iota(jnp.int32, (1, LANES), 1)
    m0 = lane < R_HEADDIM

    def headsum(x):
        s0 = jnp.sum(jnp.where(m0, x, 0.0), axis=-1, keepdims=True)
        s1 = jnp.sum(jnp.where(m0, 0.0, x), axis=-1, keepdims=True)
        return jnp.where(m0, s0, s1)

    kk = kkr * lax.rsqrt(jnp.maximum(headsum(kkr * kkr), 1e-24))
    alpha = -kk
    beta = kk * a
    logd = -jnp.exp(w)

    ti = lax.broadcasted_iota(jnp.int32, (C, C), 0)
    si = lax.broadcasted_iota(jnp.int32, (C, C), 1)
    sh = int(math.log2(lseq))
    same = jnp.right_shift(ti, sh) == jnp.right_shift(si, sh)
    incl = jnp.logical_and(same, si <= ti)
    strict = jnp.logical_and(same, si < ti)
    lg = _dot_sel_l(incl, logd)
    ltot = _dot_sel_l(same, logd)
    yield
    g_in = jnp.exp(lg)
    ginv = jnp.exp(-lg)
    gend = jnp.exp(ltot - lg)
    w0 = alpha * jnp.exp(lg - logd)
    rg = r * g_in
    bi = beta * ginv
    ki = k * ginv
    wr = jnp.concatenate([w0, rg], axis=0)
    rsel = jnp.right_shift(jnp.bitwise_and(lax.broadcasted_iota(jnp.int32, (2 * C, 1), 0), C - 1), sh)

    xy0 = None
    for s in range(ns):
        t = _dot_nt(wr, states[s])
        if ns > 1:
            t = jnp.where(rsel == s, t, 0.0)
        xy0 = t if xy0 is None else xy0 + t
    x0 = xy0[:C]
    y0 = xy0[C:]
    yield

    eye = (ti == si).astype(F32)
    heads = (m0, jnp.logical_not(m0))
    wrm = [jnp.where(mh, wr, 0.0) for mh in heads]
    gb = [_dot_nt(x, bi) for x in wrm]
    gk = [_dot_nt(x, ki) for x in wrm]
    yield
    a_ab = [jnp.where(strict, x[:C], 0.0) for x in gb]
    a_ak = [jnp.where(strict, x[:C], 0.0) for x in gk]
    b_rb = [jnp.where(incl, x[C:], 0.0) for x in gb]
    b_rk = [jnp.where(incl, x[C:], 0.0) for x in gk]
    tinv = [eye + x for x in a_ab]
    pw = a_ab
    for _ in range(int(math.log2(lseq)) - 1):
        pw = [_dot(x, x) for x in pw]
        yield
        tinv = [t + _dot(t, x) for t, x in zip(tinv, pw)]
    xh = [x0 + _dot(x, v) for x in a_ak]
    yv = [_dot(x, v) for x in b_rk]
    yield
    uh = [_dot(t, x) for t, x in zip(tinv, xh)]
    yield
    yh = [y0 + _dot(b, x) + c for b, x, c in zip(b_rb, uh, yv)]
    u = jnp.where(m0, uh[0], uh[1])
    y = jnp.where(m0, yh[0], yh[1])
    yield

    uvt = jnp.concatenate([u, v], axis=0).T
    bkd = jnp.concatenate([beta * gend, k * gend], axis=0)
    bi_r = lax.broadcasted_iota(jnp.int32, (LANES, LANES), 0) >= R_HEADDIM
    bi_c = lax.broadcasted_iota(jnp.int32, (LANES, LANES), 1) >= R_HEADDIM
    same_head = bi_r == bi_c
    new_states = []
    for s in range(ns):
        rhs = bkd if ns == 1 else jnp.where(rsel == s, bkd, 0.0)
        ds = jnp.where(same_head, _dot(uvt, rhs), 0.0)
        new_states.append(states[s] * jnp.exp(ltot[s * lseq:s * lseq + 1, :]) + ds)
    yield

    mean = headsum(y) * (1.0 / R_HEADDIM)
    d = y - mean
    var = headsum(d * d) * (1.0 / R_HEADDIM)
    yn = d * lax.rsqrt(var + R_GN_EPS) * gnw + gnb
    out = (yn + headsum(r * k * rk) * v) * g
    return out, new_states


def _wkv_pair_chunk(r, k, v, kkr, a, w, g, rk, gnw, gnb, state_ref, y_ref):
    C = r.shape[0]
    H = 2 * C
    lane = lax.broadcasted_iota(jnp.int32, (1, LANES), 1)
    m0 = lane < R_HEADDIM

    def headsum(x):
        s0 = jnp.sum(jnp.where(m0, x, 0.0), axis=-1, keepdims=True)
        s1 = jnp.sum(jnp.where(m0, 0.0, x), axis=-1, keepdims=True)
        return jnp.where(m0, s0, s1)

    def stack(x):
        return jnp.concatenate([jnp.where(m0, x, 0.0), jnp.where(m0, 0.0, x)], axis=0)

    kk = kkr * lax.rsqrt(jnp.maximum(headsum(kkr * kkr), 1e-24))
    beta = kk * a
    logd = -jnp.exp(w)
    ti = lax.broadcasted_iota(jnp.int32, (C, C), 0)
    si = lax.broadcasted_iota(jnp.int32, (C, C), 1)
    lg = _dot_sel_l(si <= ti, logd)
    yield
    ltot = lg[C - 1:C, :]
    ginv = jnp.exp(-lg)
    gend = jnp.exp(ltot - lg)
    w0s = stack(-kk * jnp.exp(lg - logd))
    rgs = stack(r * jnp.exp(lg))
    vs = stack(v)
    bds = stack(beta * gend)
    kds = stack(k * gend)
    gram = _dot_nt(jnp.concatenate([w0s, rgs], axis=0),
                   jnp.concatenate([stack(beta * ginv), stack(k * ginv)], axis=0))
    yield
    tr = jnp.bitwise_and(lax.broadcasted_iota(jnp.int32, (H, H), 0), C - 1)
    sc = jnp.bitwise_and(lax.broadcasted_iota(jnp.int32, (H, H), 1), C - 1)
    strict = sc < tr
    incl = sc <= tr
    a_ab = jnp.where(strict, gram[:H, :H], 0.0)
    a_ak = jnp.where(strict, gram[:H, H:], 0.0)
    b_rb = jnp.where(incl, gram[H:, :H], 0.0)
    b_rk = jnp.where(incl, gram[H:, H:], 0.0)
    eye = (lax.broadcasted_iota(jnp.int32, (H, H), 0) == lax.broadcasted_iota(jnp.int32, (H, H), 1)).astype(F32)
    tinv = eye + a_ab
    pw = _dot(a_ab, a_ab)
    av = _dot(a_ak, vs)
    bv = _dot(b_rk, vs)
    yield
    for _ in range(int(math.log2(C)) - 2):
        tinv = tinv + _dot(tinv, pw)
        pw = _dot(pw, pw)
        yield
    tinv = tinv + _dot(tinv, pw)
    yield
    tw = _dot(tinv, jnp.concatenate([w0s, av], axis=1))
    yield
    br = _dot(b_rb, tw)
    m_t = _dot(tw[:, :LANES].T, bds)
    z = _dot(jnp.concatenate([tw[:, LANES:], vs], axis=0).T, jnp.concatenate([bds, kds], axis=0))
    yield
    r_t = rgs + br[:, :LANES]
    y_t = br[:, LANES:] + bv
    state = state_ref[...]
    ys = _dot_nt(r_t, state) + y_t
    state_ref[...] = state * jnp.exp(ltot) + _dot(state, m_t) + z
    yield
    y = ys[:C] + ys[C:]
    mean = headsum(y) * (1.0 / R_HEADDIM)
    d = y - mean
    var = headsum(d * d) * (1.0 / R_HEADDIM)
    yn = d * lax.rsqrt(var + R_GN_EPS) * gnw + gnb
    y_ref[...] = ((yn + headsum(r * k * rk) * v) * g).astype(y_ref.dtype)


def _interleave(generators):
    live = list(generators)
    while live:
        nxt = []
        for gen in live:
            try:
                next(gen)
                nxt.append(gen)
            except StopIteration:
                pass
        live = nxt


def _wkv_prompt_kernel(*refs, chunks_per_seq, n_aliased):
    (r_ref, k_ref, v_ref, kkr_ref, a_ref, w_ref, g_ref, rk_ref, gnw_ref, gnb_ref,
     y_ref, s_out_ref, s_ref) = refs[n_aliased:]
    c = pl.program_id(0) % chunks_per_seq
    npair = r_ref.shape[0]

    @pl.when(c == 0)
    def _():
        s_ref[...] = jnp.zeros(s_ref.shape, F32)

    _interleave(_wkv_pair_chunk(r_ref[p], k_ref[p], v_ref[p], kkr_ref[p], a_ref[p], w_ref[p], g_ref[p],
                                rk_ref[p], gnw_ref[p], gnb_ref[p], s_ref.at[p], y_ref.at[p])
                for p in range(npair))

    @pl.when(c == chunks_per_seq - 1)
    def _():
        for p in range(npair):
            s = s_ref[p]
            s_out_ref[0, p] = s[:, :R_HEADDIM] + s[:, R_HEADDIM:]


def _wkv_prompt(streams, rk, gnw, gnb, B, L, layer, n_layers, prev):
    npair, T, _ = streams[0].shape
    C = R_CHUNK
    cps = L // C
    blk = pl.BlockSpec((npair, C, LANES), lambda i: (0, i, 0))
    vec = _resident((npair, 1, LANES))
    prev, prev_specs, aliases = _alias_prev(prev, first_out=1)
    return pl.pallas_call(
        functools.partial(_wkv_prompt_kernel, chunks_per_seq=cps, n_aliased=len(prev)),
        out_shape=(jax.ShapeDtypeStruct((npair, T, LANES), BF16),
                   jax.ShapeDtypeStruct((n_layers, B, npair, LANES, R_HEADDIM), F32)),
        grid=(B * cps,),
        in_specs=prev_specs + [blk] * 7 + [vec] * 3,
        out_specs=(blk, pl.BlockSpec((None, 1, npair, LANES, R_HEADDIM), lambda i: (layer, i // cps, 0, 0, 0))),
        scratch_shapes=[pltpu.VMEM((npair, LANES, LANES), F32)],
        input_output_aliases=aliases,
        compiler_params=_cparams(1),
        name="wkv_prompt",
    )(*prev, *streams, rk, gnw, gnb)


def _wkv_sample_kernel(*refs, seq, n_aliased):
    (r_ref, k_ref, v_ref, kkr_ref, a_ref, w_ref, g_ref, rk_ref, gnw_ref, gnb_ref,
     s0_ref, y_ref, s_out_ref) = refs[n_aliased:]
    npair = r_ref.shape[0]
    ns = R_CHUNK // seq
    lane = lax.broadcasted_iota(jnp.int32, (LANES, LANES), 1) >= R_HEADDIM
    rowh = lax.broadcasted_iota(jnp.int32, (LANES, LANES), 0) >= R_HEADDIM
    same_head = lane == rowh

    def pair(p):
        states = []
        for s in range(ns):
            s2 = s0_ref[s, p]
            states.append(jnp.where(same_head, jnp.concatenate([s2, s2], axis=1), 0.0))
        out, new = yield from _wkv_pair(r_ref[p], k_ref[p], v_ref[p], kkr_ref[p], a_ref[p], w_ref[p], g_ref[p],
                                        rk_ref[p], gnw_ref[p], gnb_ref[p], states, seq)
        y_ref[p] = out.astype(y_ref.dtype)
        for s in range(ns):
            s_out_ref[s, p] = new[s][:, :R_HEADDIM] + new[s][:, R_HEADDIM:]

    _interleave(pair(p) for p in range(npair))


def _wkv_sample(y_all, streams, rk, gnw, gnb, s0, row0, seq, layer, prev):
    npair, T, _ = streams[0].shape
    C = R_CHUNK
    ns = C // seq
    DB = s0.shape[1]
    blk0 = row0 // C
    blk = pl.BlockSpec((npair, C, LANES), lambda i: (0, blk0 + i, 0))
    vec = _resident((npair, 1, LANES))
    sblk = pl.BlockSpec((None, ns, npair, LANES, R_HEADDIM), lambda i: (layer, i, 0, 0, 0))
    aliased, alias_specs, aliases = _alias_prev((y_all,) + tuple(prev or ()), first_out=0)
    return pl.pallas_call(
        functools.partial(_wkv_sample_kernel, seq=seq, n_aliased=len(aliased)),
        out_shape=(jax.ShapeDtypeStruct((npair, T, LANES), BF16), jax.ShapeDtypeStruct(s0.shape, F32)),
        grid=(DB // ns,),
        in_specs=alias_specs + [blk] * 7 + [vec] * 3 + [sblk],
        out_specs=(blk, sblk),
        input_output_aliases=aliases,
        compiler_params=_cparams(1),
        name="wkv_sample",
    )(*aliased, *streams, rk, gnw, gnb, s0)


def _r_out_kernel(h_ref, y_ref, w_ref, o_ref):
    y = jnp.concatenate([y_ref[p] for p in range(y_ref.shape[0])], axis=1)
    o_ref[...] = h_ref[...] + jnp.dot(y, w_ref[...], preferred_element_type=F32)


def _r_out(h, y, w, *, tm=512):
    T, D = h.shape
    npair = y.shape[0]
    return pl.pallas_call(
        _r_out_kernel,
        out_shape=jax.ShapeDtypeStruct((T, D), F32),
        grid=(T // tm,),
        in_specs=[pl.BlockSpec((tm, D), lambda i: (i, 0)), pl.BlockSpec((npair, tm, LANES), lambda i: (0, i, 0)),
                  _resident((D, D))],
        out_specs=pl.BlockSpec((tm, D), lambda i: (i, 0)),
        compiler_params=_cparams(1),
        name="r_out",
    )(h, y, w.astype(BF16))


def kernel(x_prompt, x_sample, p_prompt, p_sample, state_ssm, state_conv, state_wkv, state_shift, norm_ffn1, ffn1_gate_up, ffn1_down, norm_mix, norm_ffn2, ffn2_gate_up, ffn2_down, norm_ple, ple_in, ple_gate, norm_final, m_in_proj, m_conv_w, m_conv_b, m_dt_bias, m_A_log, m_D, m_norm, m_out_proj, r_mu, r_wr, r_wk, r_wv, r_wo, r_w0, r_w1, r_w2, r_a0, r_a1, r_a2, r_g1, r_g2, r_k_k, r_k_a, r_r_k, r_gn_w, r_gn_b, r_v0, r_v1, r_v2):
    B, L, D = x_prompt.shape
    DB, DL, _ = x_sample.shape
    depth = norm_ffn1.shape[0]
    TP, TS = B * L, DB * DL
    n_heads = m_A_log.shape[1]
    d_inner = n_heads * M_HEADDIM
    conv_dim = m_conv_w.shape[2]
    r_heads = D // R_HEADDIM
    npair = D // LANES

    h = (x_prompt.reshape(TP, D), x_sample.reshape(TS, D))
    pp = p_prompt.reshape(depth, TP, -1)
    ps = p_sample.reshape(depth, TS, -1)
    e64, e128 = _head_expanders(n_heads)
    n_mamba = state_ssm.shape[0]
    n_rwkv = state_wkv.shape[0]
    wkv0 = state_wkv.reshape(n_rwkv, DB, npair, LANES, R_HEADDIM)

    m_prompt = m_sample = wkv_p = wkv_s = None
    shift_p, shift_s = [], []
    v_first = None
    for i in range(depth):
        j = i // 2
        h = _ffn(h, norm_ffn1[i], ffn1_gate_up[i], ffn1_down[i])
        if i % 2 == 0:
            z, xbc, dt = _m_in_proj(h, norm_mix[i], m_in_proj[j], d_inner, conv_dim, n_heads)
            prm = dict(conv_w=m_conv_w[j], conv_b=m_conv_b[j].reshape(1, conv_dim),
                       dt_bias=jnp.pad(m_dt_bias[j], (0, LANES - n_heads)).reshape(1, LANES),
                       a_log=jnp.pad(m_A_log[j], (0, LANES - n_heads)).reshape(1, LANES),
                       d_full=jnp.repeat(m_D[j], M_HEADDIM).reshape(1, d_inner),
                       norm_w=m_norm[j].reshape(1, d_inner), e64=e64, e128=e128)
            y, *m_prompt = _ssd_prompt(z, xbc, dt, prm, B, L, n_heads, j, n_mamba, m_prompt)
            y, *m_sample = _ssd_sample(y, z, xbc, dt, state_ssm, state_conv, prm, TP, DL, n_heads, j, m_sample)
            h = _out_proj(h, y, m_out_proj[j])
        else:
            rp = dict(mu=r_mu[j], wr=r_wr[j], wk=r_wk[j], wv=r_wv[j], w0=r_w0[j], w1=r_w1[j], w2=r_w2[j],
                      a0=r_a0[j], a1=r_a1[j], a2=r_a2[j], g1=r_g1[j], g2=r_g2[j], k_k=r_k_k[j], k_a=r_k_a[j])
            if j > 0:
                rp.update(v0=r_v0[j - 1], v1=r_v1[j - 1], v2=r_v2[j - 1])
            pshift = jnp.zeros((B, D), F32)
            srows = jnp.zeros((DB, DL, D), F32).at[:, 0].set(state_shift[j]).reshape(TS, D)
            outs = _r_proj(h, norm_mix[i], pshift, srows, rp, v_first if j > 0 else None, L, TP, DL)
            streams, u = outs[:7], outs[7]
            if j == 0:
                v_first = streams[2]
            rk = r_r_k[j].reshape(npair, 1, LANES)
            gnw = r_gn_w[j].reshape(npair, 1, LANES)
            gnb = r_gn_b[j].reshape(npair, 1, LANES)
            y, wkv_p = _wkv_prompt(streams, rk, gnw, gnb, B, L, j, n_rwkv, None if wkv_p is None else (wkv_p,))
            y, wkv_s = _wkv_sample(y, streams, rk, gnw, gnb, wkv0, TP, DL, j, None if wkv_s is None else (wkv_s,))
            shift_p.append(u[:TP].reshape(B, L, D)[:, L - 1])
            shift_s.append(u[TP:].reshape(DB, DL, D)[:, DL - 1])
            h = _r_out(h, y, r_wo[j])
        h = _ffn(h, norm_ffn2[i], ffn2_gate_up[i], ffn2_down[i])
        h = _ple(h, pp, ps, i, norm_ple[i], ple_in[i], ple_gate[i], norm_final, final=(i == depth - 1))

    y_prompt, y_sample = h
    return (y_prompt.reshape(B, L, D), y_sample.reshape(DB, DL, D),
            m_prompt[0], m_prompt[1], wkv_p.reshape(n_rwkv, B, r_heads, R_HEADDIM, R_HEADDIM), jnp.stack(shift_p),
            m_sample[0], m_sample[1], wkv_s.reshape(n_rwkv, DB, r_heads, R_HEADDIM, R_HEADDIM), jnp.stack(shift_s))
```

```python
import functools
import math

import jax
import jax.numpy as jnp
from jax import lax
from jax.experimental import pallas as pl
from jax.experimental.pallas import tpu as pltpu

F32 = jnp.float32
BF16 = jnp.bfloat16

NORM_EPS = 1e-6
M_NORM_EPS = 1e-5
R_GN_EPS = 64e-5

LANES = 128
M_HEADDIM = 64
M_STATE = 128
M_GROUPS = 8
M_CHUNK = 128
M_CONV_W = 4
R_HEADDIM = 64
R_CHUNK = 64
SAMPLE_SLOT_ROWS = 16
VMEM_LIMIT_BYTES = 56 * 1024 * 1024


def _cparams(n_axes):
    return pltpu.CompilerParams(dimension_semantics=("arbitrary",) * n_axes,
                                vmem_limit_bytes=VMEM_LIMIT_BYTES)


def _resident(shape):
    zeros = (0,) * len(shape)
    return pl.BlockSpec(shape, lambda *_: zeros, pipeline_mode=pl.Buffered(1))


def _dot(a, b):
    return jnp.dot(a.astype(BF16), b.astype(BF16), preferred_element_type=F32)


def _dot_nt(a, b):
    return lax.dot_general(a.astype(BF16), b.astype(BF16), (((1,), (1,)), ((), ())),
                           preferred_element_type=F32)


def _split3(x):
    hi = x.astype(BF16)
    r1 = x - hi.astype(F32)
    mid = r1.astype(BF16)
    lo = (r1 - mid.astype(F32)).astype(BF16)
    return hi, mid, lo


def _dot_sel_l(sel3, x):
    return jnp.dot(sel3, jnp.concatenate(_split3(x), axis=0), preferred_element_type=F32)


def _dot_sel_r(x, sel3):
    return jnp.dot(jnp.concatenate(_split3(x), axis=1), sel3, preferred_element_type=F32)


def _within_sequence_selectors(rows, seq):
    t = jnp.arange(rows)[:, None]
    s = jnp.arange(rows)[None, :]
    same = (t // seq) == (s // seq)
    incl = jnp.logical_and(same, s <= t)
    return jnp.tile(incl.astype(BF16), (1, 3)), jnp.tile(same.astype(BF16), (1, 3))


def _rms(x, g):
    ms = jnp.mean(x * x, axis=-1, keepdims=True)
    return (x * lax.rsqrt(ms + NORM_EPS)) * g


def _silu(x):
    return x * jax.nn.sigmoid(x)


def _softplus(x):
    return jnp.maximum(x, 0.0) + jnp.log1p(jnp.exp(-jnp.abs(x)))


def _split_rows_specs(tm, width, n_first_blocks, lead=()):
    none = (None,) * len(lead)
    first = pl.BlockSpec(none + (tm, width), lambda i: lead + (jnp.minimum(i, n_first_blocks - 1), 0))
    second = pl.BlockSpec(none + (tm, width), lambda i: lead + (jnp.maximum(i - n_first_blocks, 0), 0))
    return first, second


def _pick_rows(first_ref, second_ref, n_first_blocks):
    return jnp.where(pl.program_id(0) < n_first_blocks, first_ref[...], second_ref[...])


def _ffn_kernel(*refs, fc, n_first_blocks):
    if n_first_blocks is None:
        h_ref, g_ref, wgu_ref, wd_ref, o_ref, ub_ref = refs
        x = h_ref[...]
    else:
        hp_ref, hs_ref, g_ref, wgu_ref, wd_ref, o_ref, ub_ref = refs
        x = _pick_rows(hp_ref, hs_ref, n_first_blocks)
    ub_ref[...] = _rms(x, g_ref[...]).astype(BF16)
    F = wd_ref.shape[0]
    acc = x
    for c in range(F // fc):
        ub = ub_ref[...]
        gate = jnp.dot(ub, wgu_ref[:, c * fc:(c + 1) * fc], preferred_element_type=F32)
        up = jnp.dot(ub, wgu_ref[:, F + c * fc:F + (c + 1) * fc], preferred_element_type=F32)
        act = (_silu(gate) * up).astype(BF16)
        acc = acc + 0.5 * jnp.dot(act, wd_ref[c * fc:(c + 1) * fc, :], preferred_element_type=F32)
    o_ref[...] = acc


def _ffn(h, gain, w_gate_up, w_down, *, tm=512, fc=256):
    parts = h if isinstance(h, tuple) else (h,)
    T = sum(p.shape[0] for p in parts)
    D = parts[0].shape[1]
    F = w_down.shape[0]
    if len(parts) == 2:
        nfb = parts[0].shape[0] // tm
        h_specs = list(_split_rows_specs(tm, D, nfb))
    else:
        nfb = None
        h_specs = [pl.BlockSpec((tm, D), lambda i: (i, 0))]
    return pl.pallas_call(
        functools.partial(_ffn_kernel, fc=fc, n_first_blocks=nfb),
        out_shape=jax.ShapeDtypeStruct((T, D), F32),
        grid=(T // tm,),
        in_specs=h_specs + [_resident((1, D)), _resident((D, 2 * F)), _resident((F, D))],
        out_specs=pl.BlockSpec((tm, D), lambda i: (i, 0)),
        scratch_shapes=[pltpu.VMEM((tm, D), BF16)],
        compiler_params=_cparams(1),
        name="ffn",
    )(*parts, gain.reshape(1, D), w_gate_up.astype(BF16), w_down.astype(BF16))


def _ple_kernel(*refs, final, n_first_blocks):
    if final:
        h_ref, pp_ref, ps_ref, g_ref, wi_ref, wg_ref, gf_ref, op_ref, os_ref = refs
    else:
        h_ref, pp_ref, ps_ref, g_ref, wi_ref, wg_ref, o_ref = refs
    x = h_ref[...]
    gate = jax.nn.sigmoid(_dot(_rms(x, g_ref[...]), wg_ref[...]))
    y = x + _dot(_pick_rows(pp_ref, ps_ref, n_first_blocks), wi_ref[...]) * gate
    if not final:
        o_ref[...] = y
        return
    y = _rms(y, gf_ref[...])
    i = pl.program_id(0)

    @pl.when(i < n_first_blocks)
    def _():
        op_ref[...] = y

    @pl.when(i >= n_first_blocks)
    def _():
        os_ref[...] = y


def _ple(h, p_first, p_second, layer, gain, w_in, w_gate, gain_final, *, final, tm=512):
    T, D = h.shape
    P = p_first.shape[2]
    nfb = p_first.shape[1] // tm
    row = pl.BlockSpec((tm, D), lambda i: (i, 0))
    ops = [h, p_first, p_second, gain.reshape(1, D), w_in.astype(BF16), w_gate.astype(BF16)]
    specs = [row, *_split_rows_specs(tm, P, nfb, lead=(layer,)), _resident((1, D)), _resident((P, D)),
             _resident((D, D))]
    if final:
        ops.append(gain_final.reshape(1, D))
        specs.append(_resident((1, D)))
        out_shape = (jax.ShapeDtypeStruct((p_first.shape[1], D), F32), jax.ShapeDtypeStruct((p_second.shape[1], D), F32))
        out_specs = _split_rows_specs(tm, D, nfb)
    else:
        out_shape = jax.ShapeDtypeStruct((T, D), F32)
        out_specs = row
    return pl.pallas_call(
        functools.partial(_ple_kernel, final=final, n_first_blocks=nfb),
        out_shape=out_shape,
        grid=(T // tm,),
        in_specs=specs,
        out_specs=out_specs,
        compiler_params=_cparams(1),
        name="ple",
    )(*ops)


def _out_proj_kernel(h_ref, y_ref, w_ref, o_ref):
    o_ref[...] = h_ref[...] + _dot(y_ref[...], w_ref[...])


def _out_proj(h, y, w, *, tm=512):
    T, D = h.shape
    K = y.shape[1]
    return pl.pallas_call(
        _out_proj_kernel,
        out_shape=jax.ShapeDtypeStruct((T, D), F32),
        grid=(T // tm,),
        in_specs=[pl.BlockSpec((tm, D), lambda i: (i, 0)), pl.BlockSpec((tm, K), lambda i: (i, 0)),
                  _resident((K, D))],
        out_specs=pl.BlockSpec((tm, D), lambda i: (i, 0)),
        compiler_params=_cparams(1),
        name="out_proj",
    )(h, y, w.astype(BF16))


def _m_in_kernel(h_ref, g_ref, wz_ref, wx_ref, wdt_ref, z_ref, xbc_ref, dt_ref):
    u = _rms(h_ref[...], g_ref[...]).astype(BF16)
    z_ref[...] = jnp.dot(u, wz_ref[...], preferred_element_type=F32)
    xbc_ref[...] = jnp.dot(u, wx_ref[...], preferred_element_type=F32)
    dt_ref[...] = jnp.dot(u, wdt_ref[...], preferred_element_type=F32)


def _m_in_proj(h, gain, w_in, d_inner, conv_dim, n_heads, *, tm=256):
    T, D = h.shape
    wz = w_in[:, :d_inner].astype(BF16)
    wx = w_in[:, d_inner:d_inner + conv_dim].astype(BF16)
    wdt = jnp.pad(w_in[:, d_inner + conv_dim:], ((0, 0), (0, LANES - n_heads))).astype(BF16)
    return pl.pallas_call(
        _m_in_kernel,
        out_shape=(jax.ShapeDtypeStruct((T, d_inner), F32), jax.ShapeDtypeStruct((T, conv_dim), F32),
                   jax.ShapeDtypeStruct((T, LANES), F32)),
        grid=(T // tm,),
        in_specs=[pl.BlockSpec((tm, D), lambda i: (i, 0)), _resident((1, D)),
                  _resident((D, d_inner)), _resident((D, conv_dim)), _resident((D, LANES))],
        out_specs=(pl.BlockSpec((tm, d_inner), lambda i: (i, 0)), pl.BlockSpec((tm, conv_dim), lambda i: (i, 0)),
                   pl.BlockSpec((tm, LANES), lambda i: (i, 0))),
        compiler_params=_cparams(1),
        name="m_in_proj",
    )(h, gain.reshape(1, D), wz, wx, wdt)


def _head_expanders(n_heads):
    hh = jnp.arange(LANES)[:, None]
    e64 = (hh == (jnp.arange(n_heads * M_HEADDIM)[None, :] // M_HEADDIM)).astype(BF16)
    e128 = (hh == (jnp.arange(n_heads * LANES)[None, :] // LANES)).astype(BF16)
    return jnp.tile(e64, (3, 1)), jnp.tile(e128, (3, 1))


def _ssd_prompt_kernel(*refs, n_heads, n_aliased):
    (z_ref, xbc_ref, dt_ref, cw_ref, cb_ref, dtb_ref, alog_ref, dsk_ref, nw_ref,
     e64_ref, e128_ref, tri_ref, y_ref, ssm_ref, conv_ref, xpad_ref, s_ref) = refs[n_aliased:]
    c = pl.program_id(1)
    nc = pl.num_programs(1)
    Q = M_CHUNK
    d_inner = n_heads * M_HEADDIM
    gw = d_inner // M_GROUPS
    bc0 = d_inner
    cc0 = d_inner + M_GROUPS * M_STATE

    @pl.when(c == 0)
    def _():
        xpad_ref[:, 0:8, :] = jnp.zeros((xpad_ref.shape[0], 8, LANES), F32)
        s_ref[...] = jnp.zeros(s_ref.shape, F32)

    slabs = []
    for s in range(xpad_ref.shape[0]):
        lanes = slice(s * LANES, (s + 1) * LANES)
        xpad_ref[s, 8:8 + Q, :] = xbc_ref[:, lanes]
        xc = None
        for k in range(M_CONV_W):
            lo = 8 - (M_CONV_W - 1) + k
            term = xpad_ref[s, lo:lo + Q, :] * cw_ref[k:k + 1, lanes]
            xc = term if xc is None else xc + term
        slabs.append(_silu(xc + cb_ref[:, lanes]))
        xpad_ref[s, 0:8, :] = xpad_ref[s, Q:Q + 8, :]
    xs = jnp.concatenate(slabs, axis=1)
    xh = xs[:, :d_inner]

    dt = _softplus(dt_ref[...] + dtb_ref[...])
    dta = dt * (-jnp.exp(alog_ref[...]))
    ri = lax.broadcasted_iota(jnp.int32, (Q, Q), 0)
    ci = lax.broadcasted_iota(jnp.int32, (Q, Q), 1)
    causal = ri >= ci
    acum = _dot_sel_l(tri_ref[...], dta)
    a_full = _dot_sel_r(acum, e64_ref[...])
    dt_full = _dot_sel_r(dt, e64_ref[...])
    a_col = _dot_sel_r(acum, e128_ref[...])
    acum_t = acum.T
    a_last = a_full[Q - 1:Q, :]
    exp_a = jnp.exp(a_full)
    xdt = xh * dt_full
    xw = xdt * jnp.exp(a_last - a_full)
    chunk_decay = jnp.exp(a_last)
    lane = lax.broadcasted_iota(jnp.int32, (Q, LANES), 1)
    first_head = lane < M_HEADDIM
    hpg = n_heads // M_GROUPS

    for g in range(M_GROUPS):
        bg = xs[:, bc0 + g * M_STATE: bc0 + (g + 1) * M_STATE]
        cg = xs[:, cc0 + g * M_STATE: cc0 + (g + 1) * M_STATE]
        cb = _dot_nt(cg, bg)
        sl = slice(g * gw, (g + 1) * gw)
        s_g = s_ref[:, sl]
        y_g = _dot(cg, s_g) * exp_a[:, sl]
        parts = []
        for q in range(hpg // 2):
            h0 = g * hpg + 2 * q
            xp = xdt[:, h0 * M_HEADDIM: h0 * M_HEADDIM + LANES]
            ys = []
            for hh in (h0, h0 + 1):
                seg = a_col[:, hh * LANES:(hh + 1) * LANES] - acum_t[hh:hh + 1, :]
                m = cb * jnp.exp(jnp.where(causal, seg, -jnp.inf))
                ys.append(_dot(m, xp))
            parts.append(jnp.where(first_head, ys[0], ys[1]))
        y_g = y_g + jnp.concatenate(parts, axis=1) + dsk_ref[:, sl] * xh[:, sl]
        y_g = y_g * _silu(z_ref[:, sl])
        y_g = y_g * lax.rsqrt(jnp.mean(y_g * y_g, axis=-1, keepdims=True) + M_NORM_EPS)
        y_ref[:, sl] = y_g * nw_ref[:, sl]
        s_ref[:, sl] = s_g * chunk_decay[:, sl] + _dot(bg.T, xw[:, sl])

    @pl.when(c == nc - 1)
    def _():
        ssm_ref[0] = s_ref[...].T.reshape(n_heads, M_HEADDIM, M_STATE)
        conv_ref[0] = xbc_ref[Q - (M_CONV_W - 1):Q, :]


def _alias_prev(prev, first_out):
    prev = tuple(prev or ())
    specs = [pl.BlockSpec(memory_space=pl.ANY)] * len(prev)
    return prev, specs, {k: first_out + k for k in range(len(prev))}


def _ssd_prompt(z, xbc, dt, prm, B, L, n_heads, layer, n_layers, prev):
    T, d_inner = z.shape
    conv_dim = xbc.shape[1]
    Q = M_CHUNK
    nc = L // Q
    row = lambda b, c: (b * nc + c, 0)
    prev, prev_specs, aliases = _alias_prev(prev, first_out=1)
    kern = functools.partial(_ssd_prompt_kernel, n_heads=n_heads, n_aliased=len(prev))
    return pl.pallas_call(
        kern,
        out_shape=(jax.ShapeDtypeStruct((T, d_inner), F32),
                   jax.ShapeDtypeStruct((n_layers, B, n_heads, M_HEADDIM, M_STATE), F32),
                   jax.ShapeDtypeStruct((n_layers, B, M_CONV_W - 1, conv_dim), F32)),
        grid=(B, nc),
        in_specs=prev_specs + [
            pl.BlockSpec((Q, d_inner), row), pl.BlockSpec((Q, conv_dim), row), pl.BlockSpec((Q, LANES), row),
            _resident((M_CONV_W, conv_dim)), _resident((1, conv_dim)),
            _resident((1, LANES)), _resident((1, LANES)), _resident((1, d_inner)), _resident((1, d_inner)),
            _resident((3 * LANES, d_inner)), _resident((3 * LANES, n_heads * LANES)), _resident((Q, 3 * Q))],
        out_specs=(pl.BlockSpec((Q, d_inner), row),
                   pl.BlockSpec((None, 1, n_heads, M_HEADDIM, M_STATE), lambda b, c: (layer, b, 0, 0, 0)),
                   pl.BlockSpec((None, 1, M_CONV_W - 1, conv_dim), lambda b, c: (layer, b, 0, 0))),
        scratch_shapes=[pltpu.VMEM((conv_dim // LANES, Q + 8, LANES), F32), pltpu.VMEM((M_STATE, d_inner), F32)],
        input_output_aliases=aliases,
        compiler_params=_cparams(2),
        name="ssd_prompt",
    )(*prev, z, xbc, dt, prm["conv_w"], prm["conv_b"], prm["dt_bias"], prm["a_log"], prm["d_full"], prm["norm_w"],
      prm["e64"], prm["e128"], prm["tri"])


def _ssd_sample_kernel(*refs, n_heads, seq, nb, n_aliased):
    (z_ref, xbc_ref, dt_ref, s0_ref, c0_ref, cw_ref, cb_ref, dtb_ref, alog_ref,
     dsk_ref, nw_ref, e64_ref, y_ref, ssm_ref, conv_ref, xpad_ref, zs_ref, dts_ref) = refs[n_aliased:]
    d_inner = n_heads * M_HEADDIM
    gw = d_inner // M_GROUPS
    bc0 = d_inner
    cc0 = d_inner + M_GROUPS * M_STATE
    hpg = n_heads // M_GROUPS
    R = SAMPLE_SLOT_ROWS
    x0 = 8
    row = lax.broadcasted_iota(jnp.int32, (R, 1), 0)
    valid = row < seq
    eye = (lax.broadcasted_iota(jnp.int32, (LANES, LANES), 0)
           == lax.broadcasted_iota(jnp.int32, (LANES, LANES), 1)).astype(BF16)

    xpad_ref[...] = jnp.zeros(xpad_ref.shape, F32)
    zs_ref[...] = jnp.zeros(zs_ref.shape, F32)
    dts_ref[...] = jnp.zeros(dts_ref.shape, F32)

    def shift_rows(x, s):
        return jnp.where(row >= s, pltpu.roll(x, s, 0), 0.0)

    for bi in range(nb):
        r0 = bi * seq
        xpad_ref[x0 - (M_CONV_W - 1):x0, :] = c0_ref[bi]
        xpad_ref[x0:x0 + seq, :] = xbc_ref[r0:r0 + seq, :]
        zs_ref[0:seq, :] = z_ref[r0:r0 + seq, :]
        dts_ref[0:seq, :] = dt_ref[r0:r0 + seq, :]
        xc = None
        for k in range(M_CONV_W):
            lo = x0 - (M_CONV_W - 1) + k
            term = xpad_ref[lo:lo + R, :] * cw_ref[k:k + 1, :]
            xc = term if xc is None else xc + term
        xc = xc + cb_ref[...]
        conv_ref[bi] = xpad_ref[x0 + seq - (M_CONV_W - 1):x0 + seq, :]
        xs = _silu(xc)
        xh = xs[:, :d_inner]

        dt = jnp.where(valid, _softplus(dts_ref[...] + dtb_ref[...]), 0.0)
        dta = dt * (-jnp.exp(alog_ref[...]))
        acum = dta
        s = 1
        while s < seq:
            acum = acum + shift_rows(acum, s)
            s *= 2
        a_full = _dot_sel_r(acum, e64_ref[...])
        dt_full = _dot_sel_r(dt, e64_ref[...])
        a_last = a_full[seq - 1:seq, :]
        exp_a = jnp.exp(a_full)
        xdt = xh * dt_full
        aux = jnp.where(valid, xdt * jnp.exp(a_last - a_full), 0.0)
        for k, piece in enumerate(_split3(jnp.exp(a_last))):
            aux = jnp.where(row == seq + k, piece.astype(F32), aux)
        col = lax.broadcasted_iota(jnp.int32, (1, R), 1)

        y = dsk_ref[...] * xh
        for j in range(seq):
            lj = jnp.where(row >= j, jnp.exp(a_full - a_full[j:j + 1, :]), 0.0)
            wj = lj * xdt[j:j + 1, :]
            parts = []
            for g in range(M_GROUPS):
                cgb = (xs[:, cc0 + g * M_STATE: cc0 + (g + 1) * M_STATE]
                       * xs[j:j + 1, bc0 + g * M_STATE: bc0 + (g + 1) * M_STATE])
                parts.append(jnp.sum(cgb, axis=-1, keepdims=True) * wj[:, g * gw:(g + 1) * gw])
            y = y + jnp.concatenate(parts, axis=1)

        inter = []
        for pr in range(n_heads // 2):
            g = (2 * pr) // hpg
            cg = xs[:, cc0 + g * M_STATE: cc0 + (g + 1) * M_STATE]
            bg = xs[:, bc0 + g * M_STATE: bc0 + (g + 1) * M_STATE]
            h0 = s0_ref[bi, 2 * pr:2 * pr + 2].reshape(2 * M_HEADDIM, M_STATE)
            inter.append(_dot_nt(cg, h0))
            aux_t = _dot_nt(eye, aux[:, pr * LANES:(pr + 1) * LANES])
            dec = aux_t[:, seq:seq + 1] + aux_t[:, seq + 1:seq + 2] + aux_t[:, seq + 2:seq + 3]
            upd = _dot(jnp.where(col < seq, aux_t, 0.0), bg)
            ssm_ref[bi, 2 * pr:2 * pr + 2] = (h0 * dec + upd).reshape(2, M_HEADDIM, M_STATE)
        y = y + jnp.concatenate(inter, axis=1) * exp_a

        y = y * _silu(zs_ref[...])
        outs = []
        for g in range(M_GROUPS):
            yg = y[:, g * gw:(g + 1) * gw]
            outs.append(yg * lax.rsqrt(jnp.mean(yg * yg, axis=-1, keepdims=True) + M_NORM_EPS))
        yn = jnp.concatenate(outs, axis=1) * nw_ref[...]
        y_ref[r0:r0 + seq, :] = yn[0:seq, :]


def _ssd_sample(y_all, z, xbc, dt, s0, c0, prm, row0, seq, n_heads, layer, prev, *, nb=4):
    T, d_inner = z.shape
    conv_dim = xbc.shape[1]
    DB = s0.shape[1]
    assert seq + 3 <= SAMPLE_SLOT_ROWS and seq & (seq - 1) == 0
    rows = nb * seq
    blk0 = row0 // rows
    row = lambda i: (blk0 + i, 0)
    s_spec = pl.BlockSpec((None, nb, n_heads, M_HEADDIM, M_STATE), lambda i: (layer, i, 0, 0, 0))
    c_spec = pl.BlockSpec((None, nb, M_CONV_W - 1, conv_dim), lambda i: (layer, i, 0, 0))
    aliased, alias_specs, aliases = _alias_prev((y_all,) + tuple(prev or ()), first_out=0)
    kern = functools.partial(_ssd_sample_kernel, n_heads=n_heads, seq=seq, nb=nb, n_aliased=len(aliased))
    return pl.pallas_call(
        kern,
        out_shape=(jax.ShapeDtypeStruct((T, d_inner), F32),
                   jax.ShapeDtypeStruct(s0.shape, F32), jax.ShapeDtypeStruct(c0.shape, F32)),
        grid=(DB // nb,),
        in_specs=alias_specs + [
            pl.BlockSpec((rows, d_inner), row), pl.BlockSpec((rows, conv_dim), row),
            pl.BlockSpec((rows, LANES), row), s_spec, c_spec,
            _resident((M_CONV_W, conv_dim)), _resident((1, conv_dim)), _resident((1, LANES)),
            _resident((1, LANES)), _resident((1, d_inner)), _resident((1, d_inner)),
            _resident((3 * LANES, d_inner))],
        out_specs=(pl.BlockSpec((rows, d_inner), row), s_spec, c_spec),
        scratch_shapes=[pltpu.VMEM((8 + SAMPLE_SLOT_ROWS + 8, conv_dim), F32), pltpu.VMEM((SAMPLE_SLOT_ROWS, d_inner), F32),
                        pltpu.VMEM((SAMPLE_SLOT_ROWS, LANES), F32)],
        input_output_aliases=aliases,
        compiler_params=_cparams(1),
        name="ssd_sample",
    )(*aliased, z, xbc, dt, s0, c0, prm["conv_w"], prm["conv_b"], prm["dt_bias"], prm["a_log"], prm["d_full"],
      prm["norm_w"], prm["e64"])


def _r_proj_kernel(*refs, tm, blocks_per_seq, n_prompt_blocks, sample_seq, has_vres):
    if has_vres:
        (h_ref, g_ref, pshift_ref, srows_ref, mu_ref, wr_ref, wk_ref, wv_ref, w1_ref, w2_ref, a1_ref, a2_ref,
         g1_ref, g2_ref, w0_ref, a0_ref, kk_ref, ka_ref, vf_ref, v0_ref, v1_ref, v2_ref,
         r_o, k_o, v_o, kkr_o, a_o, w_o, g_o, up_o, us_o, sc_ref) = refs
    else:
        (h_ref, g_ref, pshift_ref, srows_ref, mu_ref, wr_ref, wk_ref, wv_ref, w1_ref, w2_ref, a1_ref, a2_ref,
         g1_ref, g2_ref, w0_ref, a0_ref, kk_ref, ka_ref,
         r_o, k_o, v_o, kkr_o, a_o, w_o, g_o, up_o, us_o, sc_ref) = refs
    i = pl.program_id(0)
    u = _rms(h_ref[...], g_ref[...])
    sc_ref[8:8 + tm, :] = u

    @pl.when(jnp.logical_and(i % blocks_per_seq == blocks_per_seq - 1, i < n_prompt_blocks))
    def _():
        up_o[0] = u[tm - 8:tm, :]

    @pl.when(i >= n_prompt_blocks)
    def _():
        us_o[...] = u

    @pl.when(jnp.logical_and(i % blocks_per_seq == 0, i < n_prompt_blocks))
    def _():
        sc_ref[7:8, :] = pshift_ref[pl.ds(i // blocks_per_seq, 1), :]

    prev = sc_ref[7:7 + tm, :]
    row = lax.broadcasted_iota(jnp.int32, (tm, 1), 0)
    use_state = jnp.logical_and(i >= n_prompt_blocks, jnp.bitwise_and(row, sample_seq - 1) == 0)
    prev = jnp.where(use_state, srows_ref[...], prev)
    sc_ref[7:8, :] = sc_ref[7 + tm:8 + tm, :]

    xx = prev - u
    xr = u + xx * mu_ref[0:1, :]
    xw = u + xx * mu_ref[1:2, :]
    xk = u + xx * mu_ref[2:3, :]
    xv = u + xx * mu_ref[3:4, :]
    xa = u + xx * mu_ref[4:5, :]
    xg = u + xx * mu_ref[5:6, :]
    r = _dot(xr, wr_ref[...])
    k = _dot(xk, wk_ref[...])
    v = _dot(xv, wv_ref[...])
    w = -_softplus(-(w0_ref[...] + _dot(jnp.tanh(_dot(xw, w1_ref[...])), w2_ref[...]))) - 0.5
    a = jax.nn.sigmoid(a0_ref[...] + _dot(_dot(xa, a1_ref[...]), a2_ref[...]))
    g = _dot(jax.nn.sigmoid(_dot(xg, g1_ref[...])), g2_ref[...])
    npair = r_o.shape[0]
    if has_vres:
        vf = jnp.concatenate([vf_ref[p] for p in range(npair)], axis=1)
        v = v + (vf - v) * jax.nn.sigmoid(v0_ref[...] + _dot(_dot(xv, v1_ref[...]), v2_ref[...]))
    kkr = k * kk_ref[...]
    k = k * (1.0 + (a - 1.0) * ka_ref[...])
    for p in range(npair):
        sl = slice(p * LANES, (p + 1) * LANES)
        r_o[p] = r[:, sl]
        k_o[p] = k[:, sl]
        v_o[p] = v[:, sl]
        kkr_o[p] = kkr[:, sl]
        a_o[p] = a[:, sl]
        w_o[p] = w[:, sl]
        g_o[p] = g[:, sl]


def _pad_cols(w, n):
    return jnp.pad(w, ((0, 0), (0, n - w.shape[1])))


def _pad_rows(w, n):
    return jnp.pad(w, ((0, n - w.shape[0]), (0, 0)))


def _round_up(n, m):
    return -(-n // m) * m


def _r_proj(h, gain, pshift, srows, rp, v_first, L, n_prompt_rows, sample_seq, *, tm=256):
    T, D = h.shape
    npair = D // LANES
    has_vres = v_first is not None
    row2 = lambda i: (i, 0)
    npb = n_prompt_rows // tm
    lw = _round_up(rp["w1"].shape[1], LANES)
    la = _round_up(rp["a1"].shape[1], LANES)
    lg = _round_up(rp["g1"].shape[1], LANES)
    ops = [h, gain.reshape(1, D), pshift, srows, rp["mu"],
           rp["wr"].astype(BF16), rp["wk"].astype(BF16), rp["wv"].astype(BF16),
           _pad_cols(rp["w1"], lw).astype(BF16), _pad_rows(rp["w2"], lw).astype(BF16),
           _pad_cols(rp["a1"], la).astype(BF16), _pad_rows(rp["a2"], la).astype(BF16),
           _pad_cols(rp["g1"], lg).astype(BF16), _pad_rows(rp["g2"], lg).astype(BF16),
           rp["w0"].reshape(1, D), rp["a0"].reshape(1, D), rp["k_k"].reshape(1, D), rp["k_a"].reshape(1, D)]
    specs = [pl.BlockSpec((tm, D), row2), _resident((1, D)), _resident(pshift.shape),
             pl.BlockSpec((tm, D), lambda i: (jnp.maximum(i - npb, 0), 0)), _resident((6, D)),
             _resident((D, D)), _resident((D, D)), _resident((D, D)),
             _resident((D, lw)), _resident((lw, D)), _resident((D, la)), _resident((la, D)),
             _resident((D, lg)), _resident((lg, D)),
             _resident((1, D)), _resident((1, D)), _resident((1, D)), _resident((1, D))]
    pair_spec = pl.BlockSpec((npair, tm, LANES), lambda i: (0, i, 0))
    if has_vres:
        lv = _round_up(rp["v1"].shape[1], LANES)
        ops += [v_first, rp["v0"].reshape(1, D), _pad_cols(rp["v1"], lv).astype(BF16),
                _pad_rows(rp["v2"], lv).astype(BF16)]
        specs += [pair_spec, _resident((1, D)), _resident((D, lv)), _resident((lv, D))]
    pair_shape = jax.ShapeDtypeStruct((npair, T, LANES), F32)
    bps = L // tm
    n_seq = n_prompt_rows // L
    kern = functools.partial(_r_proj_kernel, tm=tm, blocks_per_seq=bps, n_prompt_blocks=npb,
                             sample_seq=sample_seq, has_vres=has_vres)
    return pl.pallas_call(
        kern,
        out_shape=(pair_shape,) * 7 + (jax.ShapeDtypeStruct((n_seq, 8, D), F32),
                                       jax.ShapeDtypeStruct((T - n_prompt_rows, D), F32)),
        grid=(T // tm,),
        in_specs=specs,
        out_specs=(pair_spec,) * 7 + (
            pl.BlockSpec((1, 8, D), lambda i: (jnp.minimum(i // bps, n_seq - 1), 0, 0)),
            pl.BlockSpec((tm, D), lambda i: (jnp.maximum(i - npb, 0), 0))),
        scratch_shapes=[pltpu.VMEM((tm + 8, D), F32)],
        compiler_params=_cparams(1),
        name="r_proj",
    )(*ops)


def _wkv_pair(r, k, v, kkr, a, w, g, rk, gnw, gnb, incl3, same3, states, lseq):
    C = r.shape[0]
    ns = C // lseq
    lane = lax.broadcasted_iota(jnp.int32, (1, LANES), 1)
    m0 = lane < R_HEADDIM

    def headsum(x):
        s0 = jnp.sum(jnp.where(m0, x, 0.0), axis=-1, keepdims=True)
        s1 = jnp.sum(jnp.where(m0, 0.0, x), axis=-1, keepdims=True)
        return jnp.where(m0, s0, s1)

    kk = kkr * lax.rsqrt(jnp.maximum(headsum(kkr * kkr), 1e-24))
    alpha = -kk
    beta = kk * a
    logd = -jnp.exp(w)

    ti = lax.broadcasted_iota(jnp.int32, (C, C), 0)
    si = lax.broadcasted_iota(jnp.int32, (C, C), 1)
    sh = int(math.log2(lseq))
    same = jnp.right_shift(ti, sh) == jnp.right_shift(si, sh)
    incl = jnp.logical_and(same, si <= ti)
    strict = jnp.logical_and(same, si < ti)
    lg = _dot_sel_l(incl3, logd)
    ltot = _dot_sel_l(same3, logd)
    yield
    g_in = jnp.exp(lg)
    ginv = jnp.exp(-lg)
    gend = jnp.exp(ltot - lg)
    w0 = alpha * jnp.exp(lg - logd)
    rg = r * g_in
    bi = beta * ginv
    ki = k * ginv
    wr = jnp.concatenate([w0, rg], axis=0)
    rsel = jnp.right_shift(jnp.bitwise_and(lax.broadcasted_iota(jnp.int32, (2 * C, 1), 0), C - 1), sh)

    xy0 = None
    for s in range(ns):
        t = _dot_nt(wr, states[s])
        if ns > 1:
            t = jnp.where(rsel == s, t, 0.0)
        xy0 = t if xy0 is None else xy0 + t
    x0 = xy0[:C]
    y0 = xy0[C:]
    yield

    eye = (ti == si).astype(F32)
    heads = (m0, jnp.logical_not(m0))
    wrm = [jnp.where(mh, wr, 0.0) for mh in heads]
    gb = [_dot_nt(x, bi) for x in wrm]
    gk = [_dot_nt(x, ki) for x in wrm]
    yield
    a_ab = [jnp.where(strict, x[:C], 0.0) for x in gb]
    a_ak = [jnp.where(strict, x[:C], 0.0) for x in gk]
    b_rb = [jnp.where(incl, x[C:], 0.0) for x in gb]
    b_rk = [jnp.where(incl, x[C:], 0.0) for x in gk]
    tinv = [eye + x for x in a_ab]
    pw = a_ab
    for _ in range(int(math.log2(lseq)) - 1):
        pw = [_dot(x, x) for x in pw]
        yield
        tinv = [t + _dot(t, x) for t, x in zip(tinv, pw)]
    xh = [x0 + _dot(x, v) for x in a_ak]
    yv = [_dot(x, v) for x in b_rk]
    yield
    uh = [_dot(t, x) for t, x in zip(tinv, xh)]
    yield
    yh = [y0 + _dot(b, x) + c for b, x, c in zip(b_rb, uh, yv)]
    u = jnp.where(m0, uh[0], uh[1])
    y = jnp.where(m0, yh[0], yh[1])
    yield

    uvt = jnp.concatenate([u, v], axis=0).T
    bkd = jnp.concatenate([beta * gend, k * gend], axis=0)
    bi_r = lax.broadcasted_iota(jnp.int32, (LANES, LANES), 0) >= R_HEADDIM
    bi_c = lax.broadcasted_iota(jnp.int32, (LANES, LANES), 1) >= R_HEADDIM
    same_head = bi_r == bi_c
    new_states = []
    for s in range(ns):
        rhs = bkd if ns == 1 else jnp.where(rsel == s, bkd, 0.0)
        ds = jnp.where(same_head, _dot(uvt, rhs), 0.0)
        new_states.append(states[s] * jnp.exp(ltot[s * lseq:s * lseq + 1, :]) + ds)
    yield

    mean = headsum(y) * (1.0 / R_HEADDIM)
    d = y - mean
    var = headsum(d * d) * (1.0 / R_HEADDIM)
    yn = d * lax.rsqrt(var + R_GN_EPS) * gnw + gnb
    out = (yn + headsum(r * k * rk) * v) * g
    return out, new_states


def _wkv_pair_chunk(r, k, v, kkr, a, w, g, rk, gnw, gnb, incl3, state_ref, y_ref):
    C = r.shape[0]
    H = 2 * C
    lane = lax.broadcasted_iota(jnp.int32, (1, LANES), 1)
    m0 = lane < R_HEADDIM

    def headsum(x):
        s0 = jnp.sum(jnp.where(m0, x, 0.0), axis=-1, keepdims=True)
        s1 = jnp.sum(jnp.where(m0, 0.0, x), axis=-1, keepdims=True)
        return jnp.where(m0, s0, s1)

    def stack(x):
        return jnp.concatenate([jnp.where(m0, x, 0.0), jnp.where(m0, 0.0, x)], axis=0)

    kk = kkr * lax.rsqrt(jnp.maximum(headsum(kkr * kkr), 1e-24))
    beta = kk * a
    logd = -jnp.exp(w)
    lg = _dot_sel_l(incl3, logd)
    yield
    ltot = lg[C - 1:C, :]
    ginv = jnp.exp(-lg)
    gend = jnp.exp(ltot - lg)
    w0s = stack(-kk * jnp.exp(lg - logd))
    rgs = stack(r * jnp.exp(lg))
    vs = stack(v)
    bds = stack(beta * gend)
    kds = stack(k * gend)
    gram = _dot_nt(jnp.concatenate([w0s, rgs], axis=0),
                   jnp.concatenate([stack(beta * ginv), stack(k * ginv)], axis=0))
    yield
    tr = jnp.bitwise_and(lax.broadcasted_iota(jnp.int32, (H, H), 0), C - 1)
    sc = jnp.bitwise_and(lax.broadcasted_iota(jnp.int32, (H, H), 1), C - 1)
    strict = sc < tr
    incl = sc <= tr
    a_ab = jnp.where(strict, gram[:H, :H], 0.0)
    a_ak = jnp.where(strict, gram[:H, H:], 0.0)
    b_rb = jnp.where(incl, gram[H:, :H], 0.0)
    b_rk = jnp.where(incl, gram[H:, H:], 0.0)
    eye = (lax.broadcasted_iota(jnp.int32, (H, H), 0) == lax.broadcasted_iota(jnp.int32, (H, H), 1)).astype(F32)
    tinv = eye + a_ab
    pw = _dot(a_ab, a_ab)
    av = _dot(a_ak, vs)
    bv = _dot(b_rk, vs)
    yield
    for _ in range(int(math.log2(C)) - 2):
        tinv = tinv + _dot(tinv, pw)
        pw = _dot(pw, pw)
        yield
    tinv = tinv + _dot(tinv, pw)
    yield
    tw = _dot(tinv, jnp.concatenate([w0s, av], axis=1))
    yield
    br = _dot(b_rb, tw)
    m_t = _dot(tw[:, :LANES].T, bds)
    z = _dot(jnp.concatenate([tw[:, LANES:], vs], axis=0).T, jnp.concatenate([bds, kds], axis=0))
    yield
    r_t = rgs + br[:, :LANES]
    y_t = br[:, LANES:] + bv
    state = state_ref[...]
    ys = _dot_nt(r_t, state) + y_t
    state_ref[...] = state * jnp.exp(ltot) + _dot(state, m_t) + z
    yield
    y = ys[:C] + ys[C:]
    mean = headsum(y) * (1.0 / R_HEADDIM)
    d = y - mean
    var = headsum(d * d) * (1.0 / R_HEADDIM)
    yn = d * lax.rsqrt(var + R_GN_EPS) * gnw + gnb
    y_ref[...] = ((yn + headsum(r * k * rk) * v) * g).astype(y_ref.dtype)


def _interleave(generators):
    live = list(generators)
    while live:
        nxt = []
        for gen in live:
            try:
                next(gen)
                nxt.append(gen)
            except StopIteration:
                pass
        live = nxt


def _wkv_prompt_kernel(*refs, chunks_per_seq, n_aliased):
    (r_ref, k_ref, v_ref, kkr_ref, a_ref, w_ref, g_ref, rk_ref, gnw_ref, gnb_ref, incl_ref,
     y_ref, s_out_ref, s_ref) = refs[n_aliased:]
    c = pl.program_id(0) % chunks_per_seq
    npair = r_ref.shape[0]

    @pl.when(c == 0)
    def _():
        s_ref[...] = jnp.zeros(s_ref.shape, F32)

    _interleave(_wkv_pair_chunk(r_ref[p], k_ref[p], v_ref[p], kkr_ref[p], a_ref[p], w_ref[p], g_ref[p],
                                rk_ref[p], gnw_ref[p], gnb_ref[p], incl_ref[...], s_ref.at[p], y_ref.at[p])
                for p in range(npair))

    @pl.when(c == chunks_per_seq - 1)
    def _():
        for p in range(npair):
            s = s_ref[p]
            s_out_ref[0, p] = s[:, :R_HEADDIM] + s[:, R_HEADDIM:]


def _wkv_prompt(streams, rk, gnw, gnb, B, L, layer, n_layers, prev):
    npair, T, _ = streams[0].shape
    C = R_CHUNK
    cps = L // C
    blk = pl.BlockSpec((npair, C, LANES), lambda i: (0, i, 0))
    vec = _resident((npair, 1, LANES))
    prev, prev_specs, aliases = _alias_prev(prev, first_out=1)
    return pl.pallas_call(
        functools.partial(_wkv_prompt_kernel, chunks_per_seq=cps, n_aliased=len(prev)),
        out_shape=(jax.ShapeDtypeStruct((npair, T, LANES), BF16),
                   jax.ShapeDtypeStruct((n_layers, B, npair, LANES, R_HEADDIM), F32)),
        grid=(B * cps,),
        in_specs=prev_specs + [blk] * 7 + [vec] * 3 + [_resident((C, 3 * C))],
        out_specs=(blk, pl.BlockSpec((None, 1, npair, LANES, R_HEADDIM), lambda i: (layer, i // cps, 0, 0, 0))),
        scratch_shapes=[pltpu.VMEM((npair, LANES, LANES), F32)],
        input_output_aliases=aliases,
        compiler_params=_cparams(1),
        name="wkv_prompt",
    )(*prev, *streams, rk, gnw, gnb, _within_sequence_selectors(C, C)[0])


def _wkv_sample_kernel(*refs, seq, n_aliased):
    (r_ref, k_ref, v_ref, kkr_ref, a_ref, w_ref, g_ref, rk_ref, gnw_ref, gnb_ref, incl_ref, same_ref,
     s0_ref, y_ref, s_out_ref) = refs[n_aliased:]
    npair = r_ref.shape[0]
    ns = R_CHUNK // seq
    lane = lax.broadcasted_iota(jnp.int32, (LANES, LANES), 1) >= R_HEADDIM
    rowh = lax.broadcasted_iota(jnp.int32, (LANES, LANES), 0) >= R_HEADDIM
    same_head = lane == rowh

    def pair(p):
        states = []
        for s in range(ns):
            s2 = s0_ref[s, p]
            states.append(jnp.where(same_head, jnp.concatenate([s2, s2], axis=1), 0.0))
        out, new = yield from _wkv_pair(r_ref[p], k_ref[p], v_ref[p], kkr_ref[p], a_ref[p], w_ref[p], g_ref[p],
                                        rk_ref[p], gnw_ref[p], gnb_ref[p], incl_ref[...], same_ref[...],
                                        states, seq)
        y_ref[p] = out.astype(y_ref.dtype)
        for s in range(ns):
            s_out_ref[s, p] = new[s][:, :R_HEADDIM] + new[s][:, R_HEADDIM:]

    _interleave(pair(p) for p in range(npair))


def _wkv_sample(y_all, streams, rk, gnw, gnb, s0, row0, seq, layer, prev):
    npair, T, _ = streams[0].shape
    C = R_CHUNK
    ns = C // seq
    DB = s0.shape[1]
    blk0 = row0 // C
    blk = pl.BlockSpec((npair, C, LANES), lambda i: (0, blk0 + i, 0))
    vec = _resident((npair, 1, LANES))
    sblk = pl.BlockSpec((None, ns, npair, LANES, R_HEADDIM), lambda i: (layer, i, 0, 0, 0))
    aliased, alias_specs, aliases = _alias_prev((y_all,) + tuple(prev or ()), first_out=0)
    return pl.pallas_call(
        functools.partial(_wkv_sample_kernel, seq=seq, n_aliased=len(aliased)),
        out_shape=(jax.ShapeDtypeStruct((npair, T, LANES), BF16), jax.ShapeDtypeStruct(s0.shape, F32)),
        grid=(DB // ns,),
        in_specs=alias_specs + [blk] * 7 + [vec] * 3 + [_resident((C, 3 * C))] * 2 + [sblk],
        out_specs=(blk, sblk),
        input_output_aliases=aliases,
        compiler_params=_cparams(1),
        name="wkv_sample",
    )(*aliased, *streams, rk, gnw, gnb, *_within_sequence_selectors(C, seq), s0)


def _r_out_kernel(h_ref, y_ref, w_ref, o_ref):
    y = jnp.concatenate([y_ref[p] for p in range(y_ref.shape[0])], axis=1)
    o_ref[...] = h_ref[...] + jnp.dot(y, w_ref[...], preferred_element_type=F32)


def _r_out(h, y, w, *, tm=512):
    T, D = h.shape
    npair = y.shape[0]
    return pl.pallas_call(
        _r_out_kernel,
        out_shape=jax.ShapeDtypeStruct((T, D), F32),
        grid=(T // tm,),
        in_specs=[pl.BlockSpec((tm, D), lambda i: (i, 0)), pl.BlockSpec((npair, tm, LANES), lambda i: (0, i, 0)),
                  _resident((D, D))],
        out_specs=pl.BlockSpec((tm, D), lambda i: (i, 0)),
        compiler_params=_cparams(1),
        name="r_out",
    )(h, y, w.astype(BF16))


def kernel(x_prompt, x_sample, p_prompt, p_sample, state_ssm, state_conv, state_wkv, state_shift, norm_ffn1, ffn1_gate_up, ffn1_down, norm_mix, norm_ffn2, ffn2_gate_up, ffn2_down, norm_ple, ple_in, ple_gate, norm_final, m_in_proj, m_conv_w, m_conv_b, m_dt_bias, m_A_log, m_D, m_norm, m_out_proj, r_mu, r_wr, r_wk, r_wv, r_wo, r_w0, r_w1, r_w2, r_a0, r_a1, r_a2, r_g1, r_g2, r_k_k, r_k_a, r_r_k, r_gn_w, r_gn_b, r_v0, r_v1, r_v2):
    B, L, D = x_prompt.shape
    DB, DL, _ = x_sample.shape
    depth = norm_ffn1.shape[0]
    TP, TS = B * L, DB * DL
    n_heads = m_A_log.shape[1]
    d_inner = n_heads * M_HEADDIM
    conv_dim = m_conv_w.shape[2]
    r_heads = D // R_HEADDIM
    npair = D // LANES

    h = (x_prompt.reshape(TP, D), x_sample.reshape(TS, D))
    pp = p_prompt.reshape(depth, TP, -1)
    ps = p_sample.reshape(depth, TS, -1)
    e64, e128 = _head_expanders(n_heads)
    n_mamba = state_ssm.shape[0]
    n_rwkv = state_wkv.shape[0]
    wkv0 = state_wkv.reshape(n_rwkv, DB, npair, LANES, R_HEADDIM)

    tri = _within_sequence_selectors(M_CHUNK, M_CHUNK)[0]

    m_prompt = m_sample = wkv_p = wkv_s = None
    shift_p, shift_s = [], []
    v_first = None
    for i in range(depth):
        j = i // 2
        h = _ffn(h, norm_ffn1[i], ffn1_gate_up[i], ffn1_down[i])
        if i % 2 == 0:
            prm = dict(conv_w=m_conv_w[j], conv_b=m_conv_b[j].reshape(1, conv_dim),
                       dt_bias=jnp.pad(m_dt_bias[j], (0, LANES - n_heads)).reshape(1, LANES),
                       a_log=jnp.pad(m_A_log[j], (0, LANES - n_heads)).reshape(1, LANES),
                       d_full=jnp.repeat(m_D[j], M_HEADDIM).reshape(1, d_inner),
                       norm_w=m_norm[j].reshape(1, d_inner), e64=e64, e128=e128, tri=tri)
            z, xbc, dt = _m_in_proj(h, norm_mix[i], m_in_proj[j], d_inner, conv_dim, n_heads)
            y, *m_prompt = _ssd_prompt(z, xbc, dt, prm, B, L, n_heads, j, n_mamba, m_prompt)
            y, *m_sample = _ssd_sample(y, z, xbc, dt, state_ssm, state_conv, prm, TP, DL, n_heads, j, m_sample)
            h = _out_proj(h, y, m_out_proj[j])
        else:
            rp = dict(mu=r_mu[j], wr=r_wr[j], wk=r_wk[j], wv=r_wv[j], w0=r_w0[j], w1=r_w1[j], w2=r_w2[j],
                      a0=r_a0[j], a1=r_a1[j], a2=r_a2[j], g1=r_g1[j], g2=r_g2[j], k_k=r_k_k[j], k_a=r_k_a[j])
            if j > 0:
                rp.update(v0=r_v0[j - 1], v1=r_v1[j - 1], v2=r_v2[j - 1])
            pshift = jnp.zeros((B, D), F32)
            srows = jnp.zeros((DB, DL, D), F32).at[:, 0].set(state_shift[j]).reshape(TS, D)
            outs = _r_proj(h, norm_mix[i], pshift, srows, rp, v_first if j > 0 else None, L, TP, DL)
            streams, u_tail, u_sample = outs[:7], outs[7], outs[8]
            if j == 0:
                v_first = streams[2]
            rk = r_r_k[j].reshape(npair, 1, LANES)
            gnw = r_gn_w[j].reshape(npair, 1, LANES)
            gnb = r_gn_b[j].reshape(npair, 1, LANES)
            y, wkv_p = _wkv_prompt(streams, rk, gnw, gnb, B, L, j, n_rwkv, None if wkv_p is None else (wkv_p,))
            y, wkv_s = _wkv_sample(y, streams, rk, gnw, gnb, wkv0, TP, DL, j, None if wkv_s is None else (wkv_s,))
            shift_p.append(u_tail[:, 7])
            shift_s.append(u_sample.reshape(DB, DL, D)[:, DL - 1])
            h = _r_out(h, y, r_wo[j])
        h = _ffn(h, norm_ffn2[i], ffn2_gate_up[i], ffn2_down[i])
        h = _ple(h, pp, ps, i, norm_ple[i], ple_in[i], ple_gate[i], norm_final, final=(i == depth - 1))

    y_prompt, y_sample = h
    return (y_prompt.reshape(B, L, D), y_sample.reshape(DB, DL, D),
            m_prompt[0], m_prompt[1], wkv_p.reshape(n_rwkv, B, r_heads, R_HEADDIM, R_HEADDIM), jnp.stack(shift_p),
            m_sample[0], m_sample[1], wkv_s.reshape(n_rwkv, DB, r_heads, R_HEADDIM, R_HEADDIM), jnp.stack(shift_s))
```

```python
import functools
import math

import jax
import jax.numpy as jnp
from jax import lax
from jax.experimental import pallas as pl
from jax.experimental.pallas import tpu as pltpu

F32 = jnp.float32
BF16 = jnp.bfloat16

NORM_EPS = 1e-6
M_NORM_EPS = 1e-5
R_GN_EPS = 64e-5

LANES = 128
M_HEADDIM = 64
M_STATE = 128
M_GROUPS = 8
M_CHUNK = 128
M_CONV_W = 4
R_HEADDIM = 64
R_CHUNK = 64
VMEM_LIMIT_BYTES = 56 * 1024 * 1024


def _cparams(n_axes):
    return pltpu.CompilerParams(dimension_semantics=("arbitrary",) * n_axes,
                                vmem_limit_bytes=VMEM_LIMIT_BYTES)


def _resident(shape):
    zeros = (0,) * len(shape)
    return pl.BlockSpec(shape, lambda *_: zeros, pipeline_mode=pl.Buffered(1))


def _dot(a, b):
    return jnp.dot(a.astype(BF16), b.astype(BF16), preferred_element_type=F32)


def _dot_nt(a, b):
    return lax.dot_general(a.astype(BF16), b.astype(BF16), (((1,), (1,)), ((), ())),
                           preferred_element_type=F32)


def _split3(x):
    hi = x.astype(BF16)
    r1 = x - hi.astype(F32)
    mid = r1.astype(BF16)
    lo = (r1 - mid.astype(F32)).astype(BF16)
    return hi, mid, lo


def _dot_sel_l(sel3, x):
    return jnp.dot(sel3, jnp.concatenate(_split3(x), axis=0), preferred_element_type=F32)


def _dot_sel_r(x, sel3):
    return jnp.dot(jnp.concatenate(_split3(x), axis=1), sel3, preferred_element_type=F32)


def _within_sequence_selectors(rows, seq):
    t = jnp.arange(rows)[:, None]
    s = jnp.arange(rows)[None, :]
    same = (t // seq) == (s // seq)
    incl = jnp.logical_and(same, s <= t)
    return jnp.tile(incl.astype(BF16), (1, 3)), jnp.tile(same.astype(BF16), (1, 3))


def _rms(x, g):
    ms = jnp.mean(x * x, axis=-1, keepdims=True)
    return (x * lax.rsqrt(ms + NORM_EPS)) * g


def _silu(x):
    return x * jax.nn.sigmoid(x)


def _softplus(x):
    return jnp.maximum(x, 0.0) + jnp.log1p(jnp.exp(-jnp.abs(x)))


def _split_rows_specs(tm, width, n_first_blocks, lead=()):
    none = (None,) * len(lead)
    first = pl.BlockSpec(none + (tm, width), lambda i: lead + (jnp.minimum(i, n_first_blocks - 1), 0))
    second = pl.BlockSpec(none + (tm, width), lambda i: lead + (jnp.maximum(i - n_first_blocks, 0), 0))
    return first, second


def _pick_rows(first_ref, second_ref, n_first_blocks):
    return jnp.where(pl.program_id(0) < n_first_blocks, first_ref[...], second_ref[...])


def _ffn_kernel(*refs, fc, n_first_blocks):
    if n_first_blocks is None:
        h_ref, g_ref, wgu_ref, wd_ref, o_ref, ub_ref = refs
        x = h_ref[...]
    else:
        hp_ref, hs_ref, g_ref, wgu_ref, wd_ref, o_ref, ub_ref = refs
        x = _pick_rows(hp_ref, hs_ref, n_first_blocks)
    ub_ref[...] = _rms(x, g_ref[...]).astype(BF16)
    F = wd_ref.shape[0]
    acc = x
    for c in range(F // fc):
        ub = ub_ref[...]
        gate = jnp.dot(ub, wgu_ref[:, c * fc:(c + 1) * fc], preferred_element_type=F32)
        up = jnp.dot(ub, wgu_ref[:, F + c * fc:F + (c + 1) * fc], preferred_element_type=F32)
        act = (_silu(gate) * up).astype(BF16)
        acc = acc + 0.5 * jnp.dot(act, wd_ref[c * fc:(c + 1) * fc, :], preferred_element_type=F32)
    o_ref[...] = acc


def _ffn(h, gain, w_gate_up, w_down, *, tm=512, fc=256):
    parts = h if isinstance(h, tuple) else (h,)
    T = sum(p.shape[0] for p in parts)
    D = parts[0].shape[1]
    F = w_down.shape[0]
    if len(parts) == 2:
        nfb = parts[0].shape[0] // tm
        h_specs = list(_split_rows_specs(tm, D, nfb))
    else:
        nfb = None
        h_specs = [pl.BlockSpec((tm, D), lambda i: (i, 0))]
    return pl.pallas_call(
        functools.partial(_ffn_kernel, fc=fc, n_first_blocks=nfb),
        out_shape=jax.ShapeDtypeStruct((T, D), F32),
        grid=(T // tm,),
        in_specs=h_specs + [_resident((1, D)), _resident((D, 2 * F)), _resident((F, D))],
        out_specs=pl.BlockSpec((tm, D), lambda i: (i, 0)),
        scratch_shapes=[pltpu.VMEM((tm, D), BF16)],
        compiler_params=_cparams(1),
        name="ffn",
    )(*parts, gain.reshape(1, D), w_gate_up.astype(BF16), w_down.astype(BF16))


def _ple_kernel(*refs, final, n_first_blocks):
    if final:
        h_ref, pp_ref, ps_ref, g_ref, wi_ref, wg_ref, gf_ref, op_ref, os_ref = refs
    else:
        h_ref, pp_ref, ps_ref, g_ref, wi_ref, wg_ref, o_ref = refs
    x = h_ref[...]
    gate = jax.nn.sigmoid(_dot(_rms(x, g_ref[...]), wg_ref[...]))
    y = x + _dot(_pick_rows(pp_ref, ps_ref, n_first_blocks), wi_ref[...]) * gate
    if not final:
        o_ref[...] = y
        return
    y = _rms(y, gf_ref[...])
    i = pl.program_id(0)

    @pl.when(i < n_first_blocks)
    def _():
        op_ref[...] = y

    @pl.when(i >= n_first_blocks)
    def _():
        os_ref[...] = y


def _ple(h, p_first, p_second, layer, gain, w_in, w_gate, gain_final, *, final, tm=512):
    T, D = h.shape
    P = p_first.shape[2]
    nfb = p_first.shape[1] // tm
    row = pl.BlockSpec((tm, D), lambda i: (i, 0))
    ops = [h, p_first, p_second, gain.reshape(1, D), w_in.astype(BF16), w_gate.astype(BF16)]
    specs = [row, *_split_rows_specs(tm, P, nfb, lead=(layer,)), _resident((1, D)), _resident((P, D)),
             _resident((D, D))]
    if final:
        ops.append(gain_final.reshape(1, D))
        specs.append(_resident((1, D)))
        out_shape = (jax.ShapeDtypeStruct((p_first.shape[1], D), F32), jax.ShapeDtypeStruct((p_second.shape[1], D), F32))
        out_specs = _split_rows_specs(tm, D, nfb)
    else:
        out_shape = jax.ShapeDtypeStruct((T, D), F32)
        out_specs = row
    return pl.pallas_call(
        functools.partial(_ple_kernel, final=final, n_first_blocks=nfb),
        out_shape=out_shape,
        grid=(T // tm,),
        in_specs=specs,
        out_specs=out_specs,
        compiler_params=_cparams(1),
        name="ple",
    )(*ops)


def _out_proj_kernel(h_ref, y_ref, w_ref, o_ref):
    o_ref[...] = h_ref[...] + _dot(y_ref[...], w_ref[...])


def _out_proj(h, y, w, *, tm=512):
    T, D = h.shape
    K = y.shape[1]
    return pl.pallas_call(
        _out_proj_kernel,
        out_shape=jax.ShapeDtypeStruct((T, D), F32),
        grid=(T // tm,),
        in_specs=[pl.BlockSpec((tm, D), lambda i: (i, 0)), pl.BlockSpec((tm, K), lambda i: (i, 0)),
                  _resident((K, D))],
        out_specs=pl.BlockSpec((tm, D), lambda i: (i, 0)),
        compiler_params=_cparams(1),
        name="out_proj",
    )(h, y, w.astype(BF16))


def _m_in_kernel(h_ref, g_ref, wz_ref, wx_ref, wdt_ref, z_ref, xbc_ref, dt_ref):
    u = _rms(h_ref[...], g_ref[...]).astype(BF16)
    z_ref[...] = jnp.dot(u, wz_ref[...], preferred_element_type=F32)
    xbc_ref[...] = jnp.dot(u, wx_ref[...], preferred_element_type=F32)
    dt_ref[...] = jnp.dot(u, wdt_ref[...], preferred_element_type=F32)


def _m_in_proj(h, gain, w_in, d_inner, conv_dim, n_heads, *, tm=256):
    T, D = h.shape
    wz = w_in[:, :d_inner].astype(BF16)
    wx = w_in[:, d_inner:d_inner + conv_dim].astype(BF16)
    wdt = jnp.pad(w_in[:, d_inner + conv_dim:], ((0, 0), (0, LANES - n_heads))).astype(BF16)
    return pl.pallas_call(
        _m_in_kernel,
        out_shape=(jax.ShapeDtypeStruct((T, d_inner), F32), jax.ShapeDtypeStruct((T, conv_dim), F32),
                   jax.ShapeDtypeStruct((T, LANES), F32)),
        grid=(T // tm,),
        in_specs=[pl.BlockSpec((tm, D), lambda i: (i, 0)), _resident((1, D)),
                  _resident((D, d_inner)), _resident((D, conv_dim)), _resident((D, LANES))],
        out_specs=(pl.BlockSpec((tm, d_inner), lambda i: (i, 0)), pl.BlockSpec((tm, conv_dim), lambda i: (i, 0)),
                   pl.BlockSpec((tm, LANES), lambda i: (i, 0))),
        compiler_params=_cparams(1),
        name="m_in_proj",
    )(h, gain.reshape(1, D), wz, wx, wdt)


def _head_expanders(n_heads):
    hh = jnp.arange(LANES)[:, None]
    e64 = (hh == (jnp.arange(n_heads * M_HEADDIM)[None, :] // M_HEADDIM)).astype(BF16)
    e128 = (hh == (jnp.arange(n_heads * LANES)[None, :] // LANES)).astype(BF16)
    return jnp.tile(e64, (3, 1)), jnp.tile(e128, (3, 1))


def _ssd_prompt_kernel(*refs, n_heads, n_aliased):
    (z_ref, xbc_ref, dt_ref, cw_ref, cb_ref, dtb_ref, alog_ref, dsk_ref, nw_ref,
     e64_ref, e128_ref, tri_ref, y_ref, ssm_ref, conv_ref, xpad_ref, s_ref) = refs[n_aliased:]
    c = pl.program_id(1)
    nc = pl.num_programs(1)
    Q = M_CHUNK
    d_inner = n_heads * M_HEADDIM
    gw = d_inner // M_GROUPS
    bc0 = d_inner
    cc0 = d_inner + M_GROUPS * M_STATE

    @pl.when(c == 0)
    def _():
        xpad_ref[:, 0:8, :] = jnp.zeros((xpad_ref.shape[0], 8, LANES), F32)
        s_ref[...] = jnp.zeros(s_ref.shape, F32)

    slabs = []
    for s in range(xpad_ref.shape[0]):
        lanes = slice(s * LANES, (s + 1) * LANES)
        xpad_ref[s, 8:8 + Q, :] = xbc_ref[:, lanes]
        xc = None
        for k in range(M_CONV_W):
            lo = 8 - (M_CONV_W - 1) + k
            term = xpad_ref[s, lo:lo + Q, :] * cw_ref[k:k + 1, lanes]
            xc = term if xc is None else xc + term
        slabs.append(_silu(xc + cb_ref[:, lanes]))
        xpad_ref[s, 0:8, :] = xpad_ref[s, Q:Q + 8, :]
    xs = jnp.concatenate(slabs, axis=1)
    xh = xs[:, :d_inner]

    dt = _softplus(dt_ref[...] + dtb_ref[...])
    dta = dt * (-jnp.exp(alog_ref[...]))
    ri = lax.broadcasted_iota(jnp.int32, (Q, Q), 0)
    ci = lax.broadcasted_iota(jnp.int32, (Q, Q), 1)
    causal = ri >= ci
    acum = _dot_sel_l(tri_ref[...], dta)
    a_full = _dot_sel_r(acum, e64_ref[...])
    dt_full = _dot_sel_r(dt, e64_ref[...])
    a_col = _dot_sel_r(acum, e128_ref[...])
    acum_t = acum.T
    a_last = a_full[Q - 1:Q, :]
    exp_a = jnp.exp(a_full)
    xdt = xh * dt_full
    xw = xdt * jnp.exp(a_last - a_full)
    chunk_decay = jnp.exp(a_last)
    lane = lax.broadcasted_iota(jnp.int32, (Q, LANES), 1)
    first_head = lane < M_HEADDIM
    hpg = n_heads // M_GROUPS

    for g in range(M_GROUPS):
        bg = xs[:, bc0 + g * M_STATE: bc0 + (g + 1) * M_STATE]
        cg = xs[:, cc0 + g * M_STATE: cc0 + (g + 1) * M_STATE]
        cb = _dot_nt(cg, bg)
        sl = slice(g * gw, (g + 1) * gw)
        s_g = s_ref[:, sl]
        y_g = _dot(cg, s_g) * exp_a[:, sl]
        parts = []
        for q in range(hpg // 2):
            h0 = g * hpg + 2 * q
            xp = xdt[:, h0 * M_HEADDIM: h0 * M_HEADDIM + LANES]
            ys = []
            for hh in (h0, h0 + 1):
                seg = a_col[:, hh * LANES:(hh + 1) * LANES] - acum_t[hh:hh + 1, :]
                m = cb * jnp.exp(jnp.where(causal, seg, -jnp.inf))
                ys.append(_dot(m, xp))
            parts.append(jnp.where(first_head, ys[0], ys[1]))
        y_g = y_g + jnp.concatenate(parts, axis=1) + dsk_ref[:, sl] * xh[:, sl]
        y_g = y_g * _silu(z_ref[:, sl])
        y_g = y_g * lax.rsqrt(jnp.mean(y_g * y_g, axis=-1, keepdims=True) + M_NORM_EPS)
        y_ref[:, sl] = (y_g * nw_ref[:, sl]).astype(y_ref.dtype)
        s_ref[:, sl] = s_g * chunk_decay[:, sl] + _dot(bg.T, xw[:, sl])

    @pl.when(c == nc - 1)
    def _():
        ssm_ref[0] = s_ref[...].T.reshape(n_heads, M_HEADDIM, M_STATE)
        conv_ref[0] = xbc_ref[Q - (M_CONV_W - 1):Q, :]


def _alias_prev(prev, first_out):
    prev = tuple(prev or ())
    specs = [pl.BlockSpec(memory_space=pl.ANY)] * len(prev)
    return prev, specs, {k: first_out + k for k in range(len(prev))}


def _ssd_prompt(z, xbc, dt, prm, B, L, n_heads, layer, n_layers, prev):
    T, d_inner = z.shape
    conv_dim = xbc.shape[1]
    Q = M_CHUNK
    nc = L // Q
    row = lambda b, c: (b * nc + c, 0)
    prev, prev_specs, aliases = _alias_prev(prev, first_out=1)
    kern = functools.partial(_ssd_prompt_kernel, n_heads=n_heads, n_aliased=len(prev))
    return pl.pallas_call(
        kern,
        out_shape=(jax.ShapeDtypeStruct((T, d_inner), BF16),
                   jax.ShapeDtypeStruct((n_layers, B, n_heads, M_HEADDIM, M_STATE), F32),
                   jax.ShapeDtypeStruct((n_layers, B, M_CONV_W - 1, conv_dim), F32)),
        grid=(B, nc),
        in_specs=prev_specs + [
            pl.BlockSpec((Q, d_inner), row), pl.BlockSpec((Q, conv_dim), row), pl.BlockSpec((Q, LANES), row),
            _resident((M_CONV_W, conv_dim)), _resident((1, conv_dim)),
            _resident((1, LANES)), _resident((1, LANES)), _resident((1, d_inner)), _resident((1, d_inner)),
            _resident((3 * LANES, d_inner)), _resident((3 * LANES, n_heads * LANES)), _resident((Q, 3 * Q))],
        out_specs=(pl.BlockSpec((Q, d_inner), row),
                   pl.BlockSpec((None, 1, n_heads, M_HEADDIM, M_STATE), lambda b, c: (layer, b, 0, 0, 0)),
                   pl.BlockSpec((None, 1, M_CONV_W - 1, conv_dim), lambda b, c: (layer, b, 0, 0))),
        scratch_shapes=[pltpu.VMEM((conv_dim // LANES, Q + 8, LANES), F32), pltpu.VMEM((M_STATE, d_inner), F32)],
        input_output_aliases=aliases,
        compiler_params=_cparams(2),
        name="ssd_prompt",
    )(*prev, z, xbc, dt, prm["conv_w"], prm["conv_b"], prm["dt_bias"], prm["a_log"], prm["d_full"], prm["norm_w"],
      prm["e64"], prm["e128"], prm["tri"])


def _ssd_sample_kernel(*refs, n_heads, seq, nb, n_aliased):
    (z_ref, xbc_ref, dt_ref, s0_ref, c0_ref, cw_ref, cb_ref, dtb_ref, alog_ref,
     dsk_ref, nw_ref, e64_ref, e128_ref, rowsel_ref, y_ref, ssm_ref, conv_ref, hist_ref) = refs[n_aliased:]
    d_inner = n_heads * M_HEADDIM
    gw = d_inner // M_GROUPS
    bc0 = d_inner
    cc0 = d_inner + M_GROUPS * M_STATE
    hpg = n_heads // M_GROUPS
    rows = nb * seq
    row = lax.broadcasted_iota(jnp.int32, (rows, 1), 0)
    tok = jnp.bitwise_and(row, seq - 1)
    seq_of_row = jnp.right_shift(row, int(math.log2(seq)))
    seq_of_col = jnp.right_shift(lax.broadcasted_iota(jnp.int32, (1, rows), 1), int(math.log2(seq)))
    eye = (lax.broadcasted_iota(jnp.int32, (LANES, LANES), 0)
           == lax.broadcasted_iota(jnp.int32, (LANES, LANES), 1)).astype(BF16)
    top_head = lax.broadcasted_iota(jnp.int32, (LANES, 1), 0) < M_HEADDIM

    @pl.when(pl.program_id(0) == 0)
    def _():
        hist_ref[...] = jnp.zeros(hist_ref.shape, F32)

    x = xbc_ref[...]
    for s in range(1, M_CONV_W):
        for bi in range(nb):
            hist_ref[s - 1, bi * seq:bi * seq + s, :] = c0_ref[bi, M_CONV_W - 1 - s:M_CONV_W - 1, :]
    xc = None
    for k in range(M_CONV_W):
        s = M_CONV_W - 1 - k
        tap = x if s == 0 else jnp.where(tok >= s, pltpu.roll(x, s, 0), hist_ref[s - 1])
        term = tap * cw_ref[k:k + 1, :]
        xc = term if xc is None else xc + term
    xs = _silu(xc + cb_ref[...])
    xh = xs[:, :d_inner]
    for bi in range(nb):
        conv_ref[bi] = xbc_ref[(bi + 1) * seq - (M_CONV_W - 1):(bi + 1) * seq, :]

    dt = _softplus(dt_ref[...] + dtb_ref[...])
    acum = dt * (-jnp.exp(alog_ref[...]))
    s = 1
    while s < seq:
        acum = acum + jnp.where(tok >= s, pltpu.roll(acum, s, 0), 0.0)
        s *= 2
    a_full = _dot_sel_r(acum, e64_ref[...])
    dt_full = _dot_sel_r(dt, e64_ref[...])
    a_col = _dot_sel_r(acum, e128_ref[...])
    exp_a = jnp.exp(a_full)
    xdt = xh * dt_full
    packed = jnp.concatenate([a_full, xdt, xs[:, bc0:cc0]], axis=1)
    from_tok = [_dot_sel_l(rowsel_ref[j], packed) for j in range(seq)]
    aux = xdt * jnp.exp(from_tok[seq - 1][:, :d_inner] - a_full)

    y = dsk_ref[...] * xh
    for j in range(seq):
        a_j = from_tok[j][:, :d_inner]
        wj = jnp.where(tok >= j, jnp.exp(a_full - a_j), 0.0) * from_tok[j][:, d_inner:2 * d_inner]
        parts = []
        for g in range(M_GROUPS):
            cgb = (xs[:, cc0 + g * M_STATE: cc0 + (g + 1) * M_STATE]
                   * from_tok[j][:, 2 * d_inner + g * M_STATE: 2 * d_inner + (g + 1) * M_STATE])
            parts.append(jnp.sum(cgb, axis=-1, keepdims=True) * wj[:, g * gw:(g + 1) * gw])
        y = y + jnp.concatenate(parts, axis=1)

    inter = []
    for pr in range(n_heads // 2):
        g = (2 * pr) // hpg
        cg = xs[:, cc0 + g * M_STATE: cc0 + (g + 1) * M_STATE]
        bg = xs[:, bc0 + g * M_STATE: bc0 + (g + 1) * M_STATE]
        aux_t = _dot_nt(eye, aux[:, pr * LANES:(pr + 1) * LANES])
        acc = None
        for bi in range(nb):
            h0 = s0_ref[bi, 2 * pr:2 * pr + 2].reshape(2 * M_HEADDIM, M_STATE)
            t = jnp.where(seq_of_row == bi, _dot_nt(cg, h0), 0.0)
            acc = t if acc is None else acc + t
            last = (bi + 1) * seq - 1
            dec = jnp.where(top_head, jnp.exp(a_col[last:last + 1, (2 * pr) * LANES:(2 * pr + 1) * LANES]),
                            jnp.exp(a_col[last:last + 1, (2 * pr + 1) * LANES:(2 * pr + 2) * LANES]))
            upd = _dot(jnp.where(seq_of_col == bi, aux_t, 0.0), bg)
            ssm_ref[bi, 2 * pr:2 * pr + 2] = (h0 * dec + upd).reshape(2, M_HEADDIM, M_STATE)
        inter.append(acc)
    y = y + jnp.concatenate(inter, axis=1) * exp_a

    y = y * _silu(z_ref[...])
    outs = []
    for g in range(M_GROUPS):
        yg = y[:, g * gw:(g + 1) * gw]
        outs.append(yg * lax.rsqrt(jnp.mean(yg * yg, axis=-1, keepdims=True) + M_NORM_EPS))
    y_ref[...] = (jnp.concatenate(outs, axis=1) * nw_ref[...]).astype(y_ref.dtype)


def _ssd_sample(y_all, z, xbc, dt, s0, c0, prm, row0, seq, n_heads, layer, prev, *, nb=4):
    T, d_inner = z.shape
    conv_dim = xbc.shape[1]
    DB = s0.shape[1]
    assert seq & (seq - 1) == 0 and seq >= M_CONV_W - 1
    rows = nb * seq
    r = jnp.arange(rows)
    rowsel = jnp.stack([jnp.tile((r[None, :] == ((r // seq) * seq + j)[:, None]).astype(BF16), (1, 3))
                        for j in range(seq)])
    blk0 = row0 // rows
    row = lambda i: (blk0 + i, 0)
    s_spec = pl.BlockSpec((None, nb, n_heads, M_HEADDIM, M_STATE), lambda i: (layer, i, 0, 0, 0))
    c_spec = pl.BlockSpec((None, nb, M_CONV_W - 1, conv_dim), lambda i: (layer, i, 0, 0))
    aliased, alias_specs, aliases = _alias_prev((y_all,) + tuple(prev or ()), first_out=0)
    kern = functools.partial(_ssd_sample_kernel, n_heads=n_heads, seq=seq, nb=nb, n_aliased=len(aliased))
    return pl.pallas_call(
        kern,
        out_shape=(jax.ShapeDtypeStruct((T, d_inner), y_all.dtype),
                   jax.ShapeDtypeStruct(s0.shape, F32), jax.ShapeDtypeStruct(c0.shape, F32)),
        grid=(DB // nb,),
        in_specs=alias_specs + [
            pl.BlockSpec((rows, d_inner), row), pl.BlockSpec((rows, conv_dim), row),
            pl.BlockSpec((rows, LANES), row), s_spec, c_spec,
            _resident((M_CONV_W, conv_dim)), _resident((1, conv_dim)), _resident((1, LANES)),
            _resident((1, LANES)), _resident((1, d_inner)), _resident((1, d_inner)),
            _resident((3 * LANES, d_inner)), _resident((3 * LANES, n_heads * LANES)),
            _resident((seq, rows, 3 * rows))],
        out_specs=(pl.BlockSpec((rows, d_inner), row), s_spec, c_spec),
        scratch_shapes=[pltpu.VMEM((M_CONV_W - 1, rows, conv_dim), F32)],
        input_output_aliases=aliases,
        compiler_params=_cparams(1),
        name="ssd_sample",
    )(*aliased, z, xbc, dt, s0, c0, prm["conv_w"], prm["conv_b"], prm["dt_bias"], prm["a_log"], prm["d_full"],
      prm["norm_w"], prm["e64"], prm["e128"], rowsel)


def _r_proj_kernel(*refs, tm, blocks_per_seq, n_prompt_blocks, sample_seq, has_vres):
    if has_vres:
        (h_ref, g_ref, pshift_ref, srows_ref, mu_ref, wr_ref, wk_ref, wv_ref, w1_ref, w2_ref, a1_ref, a2_ref,
         g1_ref, g2_ref, w0_ref, a0_ref, kk_ref, ka_ref, vf_ref, v0_ref, v1_ref, v2_ref,
         r_o, k_o, v_o, kkr_o, a_o, w_o, g_o, up_o, us_o, sc_ref, xx_ref) = refs
    else:
        (h_ref, g_ref, pshift_ref, srows_ref, mu_ref, wr_ref, wk_ref, wv_ref, w1_ref, w2_ref, a1_ref, a2_ref,
         g1_ref, g2_ref, w0_ref, a0_ref, kk_ref, ka_ref,
         r_o, k_o, v_o, kkr_o, a_o, w_o, g_o, up_o, us_o, sc_ref, xx_ref) = refs
    i = pl.program_id(0)
    u = _rms(h_ref[...], g_ref[...])
    sc_ref[8:8 + tm, :] = u

    @pl.when(jnp.logical_and(i % blocks_per_seq == blocks_per_seq - 1, i < n_prompt_blocks))
    def _():
        up_o[0] = u[tm - 8:tm, :]

    @pl.when(i >= n_prompt_blocks)
    def _():
        us_o[...] = u

    @pl.when(jnp.logical_and(i % blocks_per_seq == 0, i < n_prompt_blocks))
    def _():
        sc_ref[7:8, :] = pshift_ref[pl.ds(i // blocks_per_seq, 1), :]

    prev = sc_ref[7:7 + tm, :]
    row = lax.broadcasted_iota(jnp.int32, (tm, 1), 0)
    use_state = jnp.logical_and(i >= n_prompt_blocks, jnp.bitwise_and(row, sample_seq - 1) == 0)
    prev = jnp.where(use_state, srows_ref[...], prev)
    sc_ref[7:8, :] = sc_ref[7 + tm:8 + tm, :]

    xx_ref[...] = prev - u
    npair = r_o.shape[0]

    def rows(r0, n):
        u_h = sc_ref[8 + r0:8 + r0 + n, :]
        xx_h = xx_ref[r0:r0 + n, :]

        def mix(m):
            return u_h + xx_h * mu_ref[m:m + 1, :]

        xv = mix(3)
        r = _dot(mix(0), wr_ref[...])
        k = _dot(mix(2), wk_ref[...])
        v = _dot(xv, wv_ref[...])
        yield
        lw = _dot(mix(1), w1_ref[...])
        la = _dot(mix(4), a1_ref[...])
        lg = _dot(mix(5), g1_ref[...])
        lv = _dot(xv, v1_ref[...]) if has_vres else None
        yield
        w = -_softplus(-(w0_ref[...] + _dot(jnp.tanh(lw), w2_ref[...]))) - 0.5
        a = jax.nn.sigmoid(a0_ref[...] + _dot(la, a2_ref[...]))
        g = _dot(jax.nn.sigmoid(lg), g2_ref[...])
        if has_vres:
            vf = jnp.concatenate([vf_ref[p, r0:r0 + n, :] for p in range(npair)], axis=1)
            v = v + (vf - v) * jax.nn.sigmoid(v0_ref[...] + _dot(lv, v2_ref[...]))
        yield
        kkr = k * kk_ref[...]
        k = k * (1.0 + (a - 1.0) * ka_ref[...])
        for p in range(npair):
            sl = slice(p * LANES, (p + 1) * LANES)
            r_o[p, r0:r0 + n, :] = r[:, sl]
            k_o[p, r0:r0 + n, :] = k[:, sl]
            v_o[p, r0:r0 + n, :] = v[:, sl]
            kkr_o[p, r0:r0 + n, :] = kkr[:, sl]
            a_o[p, r0:r0 + n, :] = a[:, sl]
            w_o[p, r0:r0 + n, :] = w[:, sl]
            g_o[p, r0:r0 + n, :] = g[:, sl]

    n_groups = 2
    _interleave((rows(q * (tm // n_groups), tm // n_groups) for q in range(n_groups)), skew=1)


def _pad_cols(w, n):
    return jnp.pad(w, ((0, 0), (0, n - w.shape[1])))


def _pad_rows(w, n):
    return jnp.pad(w, ((0, n - w.shape[0]), (0, 0)))


def _round_up(n, m):
    return -(-n // m) * m


def _r_proj(h, gain, pshift, srows, rp, v_first, L, n_prompt_rows, sample_seq, *, tm=256):
    T, D = h.shape
    npair = D // LANES
    has_vres = v_first is not None
    row2 = lambda i: (i, 0)
    npb = n_prompt_rows // tm
    lw = _round_up(rp["w1"].shape[1], LANES)
    la = _round_up(rp["a1"].shape[1], LANES)
    lg = _round_up(rp["g1"].shape[1], LANES)
    ops = [h, gain.reshape(1, D), pshift, srows, rp["mu"],
           rp["wr"].astype(BF16), rp["wk"].astype(BF16), rp["wv"].astype(BF16),
           _pad_cols(rp["w1"], lw).astype(BF16), _pad_rows(rp["w2"], lw).astype(BF16),
           _pad_cols(rp["a1"], la).astype(BF16), _pad_rows(rp["a2"], la).astype(BF16),
           _pad_cols(rp["g1"], lg).astype(BF16), _pad_rows(rp["g2"], lg).astype(BF16),
           rp["w0"].reshape(1, D), rp["a0"].reshape(1, D), rp["k_k"].reshape(1, D), rp["k_a"].reshape(1, D)]
    specs = [pl.BlockSpec((tm, D), row2), _resident((1, D)), _resident(pshift.shape),
             pl.BlockSpec((tm, D), lambda i: (jnp.maximum(i - npb, 0), 0)), _resident((6, D)),
             _resident((D, D)), _resident((D, D)), _resident((D, D)),
             _resident((D, lw)), _resident((lw, D)), _resident((D, la)), _resident((la, D)),
             _resident((D, lg)), _resident((lg, D)),
             _resident((1, D)), _resident((1, D)), _resident((1, D)), _resident((1, D))]
    pair_spec = pl.BlockSpec((npair, tm, LANES), lambda i: (0, i, 0))
    if has_vres:
        lv = _round_up(rp["v1"].shape[1], LANES)
        ops += [v_first, rp["v0"].reshape(1, D), _pad_cols(rp["v1"], lv).astype(BF16),
                _pad_rows(rp["v2"], lv).astype(BF16)]
        specs += [pair_spec, _resident((1, D)), _resident((D, lv)), _resident((lv, D))]
    pair_shape = jax.ShapeDtypeStruct((npair, T, LANES), F32)
    bps = L // tm
    n_seq = n_prompt_rows // L
    kern = functools.partial(_r_proj_kernel, tm=tm, blocks_per_seq=bps, n_prompt_blocks=npb,
                             sample_seq=sample_seq, has_vres=has_vres)
    return pl.pallas_call(
        kern,
        out_shape=(pair_shape,) * 7 + (jax.ShapeDtypeStruct((n_seq, 8, D), F32),
                                       jax.ShapeDtypeStruct((T - n_prompt_rows, D), F32)),
        grid=(T // tm,),
        in_specs=specs,
        out_specs=(pair_spec,) * 7 + (
            pl.BlockSpec((1, 8, D), lambda i: (jnp.minimum(i // bps, n_seq - 1), 0, 0)),
            pl.BlockSpec((tm, D), lambda i: (jnp.maximum(i - npb, 0), 0))),
        scratch_shapes=[pltpu.VMEM((tm + 8, D), F32), pltpu.VMEM((tm, D), F32)],
        compiler_params=_cparams(1),
        name="r_proj",
    )(*ops)


def _wkv_pair(r, k, v, kkr, a, w, g, rk, gnw, gnb, incl3, same3, states, lseq):
    C = r.shape[0]
    ns = C // lseq
    lane = lax.broadcasted_iota(jnp.int32, (1, LANES), 1)
    m0 = lane < R_HEADDIM

    def headsum(x):
        s0 = jnp.sum(jnp.where(m0, x, 0.0), axis=-1, keepdims=True)
        s1 = jnp.sum(jnp.where(m0, 0.0, x), axis=-1, keepdims=True)
        return jnp.where(m0, s0, s1)

    kk = kkr * lax.rsqrt(jnp.maximum(headsum(kkr * kkr), 1e-24))
    alpha = -kk
    beta = kk * a
    logd = -jnp.exp(w)

    ti = lax.broadcasted_iota(jnp.int32, (C, C), 0)
    si = lax.broadcasted_iota(jnp.int32, (C, C), 1)
    sh = int(math.log2(lseq))
    same = jnp.right_shift(ti, sh) == jnp.right_shift(si, sh)
    incl = jnp.logical_and(same, si <= ti)
    strict = jnp.logical_and(same, si < ti)
    lg = _dot_sel_l(incl3, logd)
    ltot = _dot_sel_l(same3, logd)
    yield
    g_in = jnp.exp(lg)
    ginv = jnp.exp(-lg)
    gend = jnp.exp(ltot - lg)
    w0 = alpha * jnp.exp(lg - logd)
    rg = r * g_in
    bi = beta * ginv
    ki = k * ginv
    wr = jnp.concatenate([w0, rg], axis=0)
    rsel = jnp.right_shift(jnp.bitwise_and(lax.broadcasted_iota(jnp.int32, (2 * C, 1), 0), C - 1), sh)

    xy0 = None
    for s in range(ns):
        t = _dot_nt(wr, states[s])
        if ns > 1:
            t = jnp.where(rsel == s, t, 0.0)
        xy0 = t if xy0 is None else xy0 + t
    x0 = xy0[:C]
    y0 = xy0[C:]
    yield

    eye = (ti == si).astype(F32)
    heads = (m0, jnp.logical_not(m0))
    wrm = [jnp.where(mh, wr, 0.0) for mh in heads]
    gb = [_dot_nt(x, bi) for x in wrm]
    gk = [_dot_nt(x, ki) for x in wrm]
    yield
    a_ab = [jnp.where(strict, x[:C], 0.0) for x in gb]
    a_ak = [jnp.where(strict, x[:C], 0.0) for x in gk]
    b_rb = [jnp.where(incl, x[C:], 0.0) for x in gb]
    b_rk = [jnp.where(incl, x[C:], 0.0) for x in gk]
    tinv = [eye + x for x in a_ab]
    pw = a_ab
    for _ in range(int(math.log2(lseq)) - 1):
        pw = [_dot(x, x) for x in pw]
        yield
        tinv = [t + _dot(t, x) for t, x in zip(tinv, pw)]
    xh = [x0 + _dot(x, v) for x in a_ak]
    yv = [_dot(x, v) for x in b_rk]
    yield
    uh = [_dot(t, x) for t, x in zip(tinv, xh)]
    yield
    yh = [y0 + _dot(b, x) + c for b, x, c in zip(b_rb, uh, yv)]
    u = jnp.where(m0, uh[0], uh[1])
    y = jnp.where(m0, yh[0], yh[1])
    yield

    uvt = jnp.concatenate([u, v], axis=0).T
    bkd = jnp.concatenate([beta * gend, k * gend], axis=0)
    bi_r = lax.broadcasted_iota(jnp.int32, (LANES, LANES), 0) >= R_HEADDIM
    bi_c = lax.broadcasted_iota(jnp.int32, (LANES, LANES), 1) >= R_HEADDIM
    same_head = bi_r == bi_c
    new_states = []
    for s in range(ns):
        rhs = bkd if ns == 1 else jnp.where(rsel == s, bkd, 0.0)
        ds = jnp.where(same_head, _dot(uvt, rhs), 0.0)
        new_states.append(states[s] * jnp.exp(ltot[s * lseq:s * lseq + 1, :]) + ds)
    yield

    mean = headsum(y) * (1.0 / R_HEADDIM)
    d = y - mean
    var = headsum(d * d) * (1.0 / R_HEADDIM)
    yn = d * lax.rsqrt(var + R_GN_EPS) * gnw + gnb
    out = (yn + headsum(r * k * rk) * v) * g
    return out, new_states


def _wkv_pair_chunk(r, k, v, kkr, a, w, g, rk, gnw, gnb, incl3, state_ref, y_ref):
    C = r.shape[0]
    H = 2 * C
    lane = lax.broadcasted_iota(jnp.int32, (1, LANES), 1)
    m0 = lane < R_HEADDIM

    def headsum(x):
        s0 = jnp.sum(jnp.where(m0, x, 0.0), axis=-1, keepdims=True)
        s1 = jnp.sum(jnp.where(m0, 0.0, x), axis=-1, keepdims=True)
        return jnp.where(m0, s0, s1)

    def stack(x):
        return jnp.concatenate([jnp.where(m0, x, 0.0), jnp.where(m0, 0.0, x)], axis=0)

    kk = kkr * lax.rsqrt(jnp.maximum(headsum(kkr * kkr), 1e-24))
    beta = kk * a
    logd = -jnp.exp(w)
    lg = _dot_sel_l(incl3, logd)
    yield
    ltot = lg[C - 1:C, :]
    ginv = jnp.exp(-lg)
    gend = jnp.exp(ltot - lg)
    w0s = stack(-kk * jnp.exp(lg - logd))
    rgs = stack(r * jnp.exp(lg))
    vs = stack(v)
    bds = stack(beta * gend)
    kds = stack(k * gend)
    gram = _dot_nt(jnp.concatenate([w0s, rgs], axis=0),
                   jnp.concatenate([stack(beta * ginv), stack(k * ginv)], axis=0))
    yield
    tr = jnp.bitwise_and(lax.broadcasted_iota(jnp.int32, (H, H), 0), C - 1)
    sc = jnp.bitwise_and(lax.broadcasted_iota(jnp.int32, (H, H), 1), C - 1)
    strict = sc < tr
    incl = sc <= tr
    a_ab = jnp.where(strict, gram[:H, :H], 0.0)
    a_ak = jnp.where(strict, gram[:H, H:], 0.0)
    b_rb = jnp.where(incl, gram[H:, :H], 0.0)
    b_rk = jnp.where(incl, gram[H:, H:], 0.0)
    eye = (lax.broadcasted_iota(jnp.int32, (H, H), 0) == lax.broadcasted_iota(jnp.int32, (H, H), 1)).astype(F32)
    tinv = eye + a_ab
    pw = _dot(a_ab, a_ab)
    av = _dot(a_ak, vs)
    bv = _dot(b_rk, vs)
    yield
    for _ in range(int(math.log2(C)) - 2):
        tinv = tinv + _dot(tinv, pw)
        pw = _dot(pw, pw)
        yield
    tinv = tinv + _dot(tinv, pw)
    yield
    tw = _dot(tinv, jnp.concatenate([w0s, av], axis=1))
    yield
    br = _dot(b_rb, tw)
    m_t = _dot(tw[:, :LANES].T, bds)
    z = _dot(jnp.concatenate([tw[:, LANES:], vs], axis=0).T, jnp.concatenate([bds, kds], axis=0))
    yield
    r_t = rgs + br[:, :LANES]
    y_t = br[:, LANES:] + bv
    state = state_ref[...]
    ys = _dot_nt(r_t, state) + y_t
    state_ref[...] = state * jnp.exp(ltot) + _dot(state, m_t) + z
    yield
    y = ys[:C] + ys[C:]
    mean = headsum(y) * (1.0 / R_HEADDIM)
    d = y - mean
    var = headsum(d * d) * (1.0 / R_HEADDIM)
    yn = d * lax.rsqrt(var + R_GN_EPS) * gnw + gnb
    y_ref[...] = ((yn + headsum(r * k * rk) * v) * g).astype(y_ref.dtype)


def _interleave(generators, skew=0):
    waiting = list(enumerate(generators))
    live = []
    rnd = 0
    while waiting or live:
        live += [gen for q, gen in waiting if q * skew <= rnd]
        waiting = [(q, gen) for q, gen in waiting if q * skew > rnd]
        nxt = []
        for gen in live:
            try:
                next(gen)
                nxt.append(gen)
            except StopIteration:
                pass
        live = nxt
        rnd += 1


def _wkv_prompt_kernel(*refs, chunks_per_seq, n_aliased):
    (r_ref, k_ref, v_ref, kkr_ref, a_ref, w_ref, g_ref, rk_ref, gnw_ref, gnb_ref, incl_ref,
     y_ref, s_out_ref, s_ref) = refs[n_aliased:]
    c = pl.program_id(0) % chunks_per_seq
    npair = r_ref.shape[0]

    @pl.when(c == 0)
    def _():
        s_ref[...] = jnp.zeros(s_ref.shape, F32)

    _interleave(_wkv_pair_chunk(r_ref[p], k_ref[p], v_ref[p], kkr_ref[p], a_ref[p], w_ref[p], g_ref[p],
                                rk_ref[p], gnw_ref[p], gnb_ref[p], incl_ref[...], s_ref.at[p], y_ref.at[p])
                for p in range(npair))

    @pl.when(c == chunks_per_seq - 1)
    def _():
        for p in range(npair):
            s = s_ref[p]
            s_out_ref[0, p] = s[:, :R_HEADDIM] + s[:, R_HEADDIM:]


def _wkv_prompt(streams, rk, gnw, gnb, B, L, layer, n_layers, prev):
    npair, T, _ = streams[0].shape
    C = R_CHUNK
    cps = L // C
    blk = pl.BlockSpec((npair, C, LANES), lambda i: (0, i, 0))
    vec = _resident((npair, 1, LANES))
    prev, prev_specs, aliases = _alias_prev(prev, first_out=1)
    return pl.pallas_call(
        functools.partial(_wkv_prompt_kernel, chunks_per_seq=cps, n_aliased=len(prev)),
        out_shape=(jax.ShapeDtypeStruct((npair, T, LANES), BF16),
                   jax.ShapeDtypeStruct((n_layers, B, npair, LANES, R_HEADDIM), F32)),
        grid=(B * cps,),
        in_specs=prev_specs + [blk] * 7 + [vec] * 3 + [_resident((C, 3 * C))],
        out_specs=(blk, pl.BlockSpec((None, 1, npair, LANES, R_HEADDIM), lambda i: (layer, i // cps, 0, 0, 0))),
        scratch_shapes=[pltpu.VMEM((npair, LANES, LANES), F32)],
        input_output_aliases=aliases,
        compiler_params=_cparams(1),
        name="wkv_prompt",
    )(*prev, *streams, rk, gnw, gnb, _within_sequence_selectors(C, C)[0])


def _wkv_sample_kernel(*refs, seq, n_aliased):
    (r_ref, k_ref, v_ref, kkr_ref, a_ref, w_ref, g_ref, rk_ref, gnw_ref, gnb_ref, incl_ref, same_ref,
     s0_ref, y_ref, s_out_ref) = refs[n_aliased:]
    npair = r_ref.shape[0]
    ns = R_CHUNK // seq
    lane = lax.broadcasted_iota(jnp.int32, (LANES, LANES), 1) >= R_HEADDIM
    rowh = lax.broadcasted_iota(jnp.int32, (LANES, LANES), 0) >= R_HEADDIM
    same_head = lane == rowh

    def pair(p):
        states = []
        for s in range(ns):
            s2 = s0_ref[s, p]
            states.append(jnp.where(same_head, jnp.concatenate([s2, s2], axis=1), 0.0))
        out, new = yield from _wkv_pair(r_ref[p], k_ref[p], v_ref[p], kkr_ref[p], a_ref[p], w_ref[p], g_ref[p],
                                        rk_ref[p], gnw_ref[p], gnb_ref[p], incl_ref[...], same_ref[...],
                                        states, seq)
        y_ref[p] = out.astype(y_ref.dtype)
        for s in range(ns):
            s_out_ref[s, p] = new[s][:, :R_HEADDIM] + new[s][:, R_HEADDIM:]

    _interleave(pair(p) for p in range(npair))


def _wkv_sample(y_all, streams, rk, gnw, gnb, s0, row0, seq, layer, prev):
    npair, T, _ = streams[0].shape
    C = R_CHUNK
    ns = C // seq
    DB = s0.shape[1]
    blk0 = row0 // C
    blk = pl.BlockSpec((npair, C, LANES), lambda i: (0, blk0 + i, 0))
    vec = _resident((npair, 1, LANES))
    sblk = pl.BlockSpec((None, ns, npair, LANES, R_HEADDIM), lambda i: (layer, i, 0, 0, 0))
    aliased, alias_specs, aliases = _alias_prev((y_all,) + tuple(prev or ()), first_out=0)
    return pl.pallas_call(
        functools.partial(_wkv_sample_kernel, seq=seq, n_aliased=len(aliased)),
        out_shape=(jax.ShapeDtypeStruct((npair, T, LANES), BF16), jax.ShapeDtypeStruct(s0.shape, F32)),
        grid=(DB // ns,),
        in_specs=alias_specs + [blk] * 7 + [vec] * 3 + [_resident((C, 3 * C))] * 2 + [sblk],
        out_specs=(blk, sblk),
        input_output_aliases=aliases,
        compiler_params=_cparams(1),
        name="wkv_sample",
    )(*aliased, *streams, rk, gnw, gnb, *_within_sequence_selectors(C, seq), s0)


def _r_out_kernel(h_ref, y_ref, w_ref, o_ref):
    y = jnp.concatenate([y_ref[p] for p in range(y_ref.shape[0])], axis=1)
    o_ref[...] = h_ref[...] + jnp.dot(y, w_ref[...], preferred_element_type=F32)


def _r_out(h, y, w, *, tm=512):
    T, D = h.shape
    npair = y.shape[0]
    return pl.pallas_call(
        _r_out_kernel,
        out_shape=jax.ShapeDtypeStruct((T, D), F32),
        grid=(T // tm,),
        in_specs=[pl.BlockSpec((tm, D), lambda i: (i, 0)), pl.BlockSpec((npair, tm, LANES), lambda i: (0, i, 0)),
                  _resident((D, D))],
        out_specs=pl.BlockSpec((tm, D), lambda i: (i, 0)),
        compiler_params=_cparams(1),
        name="r_out",
    )(h, y, w.astype(BF16))


def kernel(x_prompt, x_sample, p_prompt, p_sample, state_ssm, state_conv, state_wkv, state_shift, norm_ffn1, ffn1_gate_up, ffn1_down, norm_mix, norm_ffn2, ffn2_gate_up, ffn2_down, norm_ple, ple_in, ple_gate, norm_final, m_in_proj, m_conv_w, m_conv_b, m_dt_bias, m_A_log, m_D, m_norm, m_out_proj, r_mu, r_wr, r_wk, r_wv, r_wo, r_w0, r_w1, r_w2, r_a0, r_a1, r_a2, r_g1, r_g2, r_k_k, r_k_a, r_r_k, r_gn_w, r_gn_b, r_v0, r_v1, r_v2):
    B, L, D = x_prompt.shape
    DB, DL, _ = x_sample.shape
    depth = norm_ffn1.shape[0]
    TP, TS = B * L, DB * DL
    n_heads = m_A_log.shape[1]
    d_inner = n_heads * M_HEADDIM
    conv_dim = m_conv_w.shape[2]
    r_heads = D // R_HEADDIM
    npair = D // LANES

    h = (x_prompt.reshape(TP, D), x_sample.reshape(TS, D))
    pp = p_prompt.reshape(depth, TP, -1)
    ps = p_sample.reshape(depth, TS, -1)
    e64, e128 = _head_expanders(n_heads)
    n_mamba = state_ssm.shape[0]
    n_rwkv = state_wkv.shape[0]
    wkv0 = state_wkv.reshape(n_rwkv, DB, npair, LANES, R_HEADDIM)

    tri = _within_sequence_selectors(M_CHUNK, M_CHUNK)[0]

    m_prompt = m_sample = wkv_p = wkv_s = None
    shift_p, shift_s = [], []
    v_first = None
    for i in range(depth):
        j = i // 2
        h = _ffn(h, norm_ffn1[i], ffn1_gate_up[i], ffn1_down[i])
        if i % 2 == 0:
            prm = dict(conv_w=m_conv_w[j], conv_b=m_conv_b[j].reshape(1, conv_dim),
                       dt_bias=jnp.pad(m_dt_bias[j], (0, LANES - n_heads)).reshape(1, LANES),
                       a_log=jnp.pad(m_A_log[j], (0, LANES - n_heads)).reshape(1, LANES),
                       d_full=jnp.repeat(m_D[j], M_HEADDIM).reshape(1, d_inner),
                       norm_w=m_norm[j].reshape(1, d_inner), e64=e64, e128=e128, tri=tri)
            z, xbc, dt = _m_in_proj(h, norm_mix[i], m_in_proj[j], d_inner, conv_dim, n_heads)
            y, *m_prompt = _ssd_prompt(z, xbc, dt, prm, B, L, n_heads, j, n_mamba, m_prompt)
            y, *m_sample = _ssd_sample(y, z, xbc, dt, state_ssm, state_conv, prm, TP, DL, n_heads, j, m_sample)
            h = _out_proj(h, y, m_out_proj[j])
        else:
            rp = dict(mu=r_mu[j], wr=r_wr[j], wk=r_wk[j], wv=r_wv[j], w0=r_w0[j], w1=r_w1[j], w2=r_w2[j],
                      a0=r_a0[j], a1=r_a1[j], a2=r_a2[j], g1=r_g1[j], g2=r_g2[j], k_k=r_k_k[j], k_a=r_k_a[j])
            if j > 0:
                rp.update(v0=r_v0[j - 1], v1=r_v1[j - 1], v2=r_v2[j - 1])
            pshift = jnp.zeros((B, D), F32)
            srows = jnp.zeros((DB, DL, D), F32).at[:, 0].set(state_shift[j]).reshape(TS, D)
            outs = _r_proj(h, norm_mix[i], pshift, srows, rp, v_first if j > 0 else None, L, TP, DL)
            streams, u_tail, u_sample = outs[:7], outs[7], outs[8]
            if j == 0:
                v_first = streams[2]
            rk = r_r_k[j].reshape(npair, 1, LANES)
            gnw = r_gn_w[j].reshape(npair, 1, LANES)
            gnb = r_gn_b[j].reshape(npair, 1, LANES)
            y, wkv_p = _wkv_prompt(streams, rk, gnw, gnb, B, L, j, n_rwkv, None if wkv_p is None else (wkv_p,))
            y, wkv_s = _wkv_sample(y, streams, rk, gnw, gnb, wkv0, TP, DL, j, None if wkv_s is None else (wkv_s,))
            shift_p.append(u_tail[:, 7])
            shift_s.append(u_sample.reshape(DB, DL, D)[:, DL - 1])
            h = _r_out(h, y, r_wo[j])
        h = _ffn(h, norm_ffn2[i], ffn2_gate_up[i], ffn2_down[i])
        h = _ple(h, pp, ps, i, norm_ple[i], ple_in[i], ple_gate[i], norm_final, final=(i == depth - 1))

    y_prompt, y_sample = h
    return (y_prompt.reshape(B, L, D), y_sample.reshape(DB, DL, D),
            m_prompt[0], m_prompt[1], wkv_p.reshape(n_rwkv, B, r_heads, R_HEADDIM, R_HEADDIM), jnp.stack(shift_p),
            m_sample[0], m_sample[1], wkv_s.reshape(n_rwkv, DB, r_heads, R_HEADDIM, R_HEADDIM), jnp.stack(shift_s))
```

```python
import functools
import math

import jax
import jax.numpy as jnp
from jax import lax
from jax.experimental import pallas as pl
from jax.experimental.pallas import tpu as pltpu

F32 = jnp.float32
BF16 = jnp.bfloat16

NORM_EPS = 1e-6
M_NORM_EPS = 1e-5
R_GN_EPS = 64e-5

LANES = 128
M_HEADDIM = 64
M_STATE = 128
M_GROUPS = 8
M_CHUNK = 128
M_CONV_W = 4
R_HEADDIM = 64
R_CHUNK = 64
VMEM_LIMIT_BYTES = 56 * 1024 * 1024


def _cparams(n_axes):
    return pltpu.CompilerParams(dimension_semantics=("arbitrary",) * n_axes,
                                vmem_limit_bytes=VMEM_LIMIT_BYTES)


def _resident(shape):
    zeros = (0,) * len(shape)
    return pl.BlockSpec(shape, lambda *_: zeros, pipeline_mode=pl.Buffered(1))


def _dot(a, b):
    return jnp.dot(a.astype(BF16), b.astype(BF16), preferred_element_type=F32)


def _dot_nt(a, b):
    return lax.dot_general(a.astype(BF16), b.astype(BF16), (((1,), (1,)), ((), ())),
                           preferred_element_type=F32)


def _split3(x):
    hi = x.astype(BF16)
    r1 = x - hi.astype(F32)
    mid = r1.astype(BF16)
    lo = (r1 - mid.astype(F32)).astype(BF16)
    return hi, mid, lo


def _dot_sel_l(sel3, x):
    return jnp.dot(sel3, jnp.concatenate(_split3(x), axis=0), preferred_element_type=F32)


def _dot_sel_r(x, sel3):
    return jnp.dot(jnp.concatenate(_split3(x), axis=1), sel3, preferred_element_type=F32)


def _within_sequence_selectors(rows, seq):
    t = jnp.arange(rows)[:, None]
    s = jnp.arange(rows)[None, :]
    same = (t // seq) == (s // seq)
    incl = jnp.logical_and(same, s <= t)
    return jnp.tile(incl.astype(BF16), (1, 3)), jnp.tile(same.astype(BF16), (1, 3))


def _rms(x, g):
    ms = jnp.mean(x * x, axis=-1, keepdims=True)
    return (x * lax.rsqrt(ms + NORM_EPS)) * g


def _silu(x):
    return x * jax.nn.sigmoid(x)


def _softplus(x):
    return jnp.maximum(x, 0.0) + jnp.log1p(jnp.exp(-jnp.abs(x)))


def _split_rows_specs(tm, width, n_first_blocks, lead=()):
    none = (None,) * len(lead)
    first = pl.BlockSpec(none + (tm, width), lambda i: lead + (jnp.minimum(i, n_first_blocks - 1), 0))
    second = pl.BlockSpec(none + (tm, width), lambda i: lead + (jnp.maximum(i - n_first_blocks, 0), 0))
    return first, second


def _pick_rows(first_ref, second_ref, n_first_blocks):
    return jnp.where(pl.program_id(0) < n_first_blocks, first_ref[...], second_ref[...])


def _ffn_rows(x, g_ref, wgu_ref, wd_ref, ub_ref, fc):
    ub_ref[...] = _rms(x, g_ref[...]).astype(BF16)
    F = wd_ref.shape[0]
    acc = x
    for c in range(F // fc):
        ub = ub_ref[...]
        gate = jnp.dot(ub, wgu_ref[:, c * fc:(c + 1) * fc], preferred_element_type=F32)
        up = jnp.dot(ub, wgu_ref[:, F + c * fc:F + (c + 1) * fc], preferred_element_type=F32)
        act = (_silu(gate) * up).astype(BF16)
        acc = acc + 0.5 * jnp.dot(act, wd_ref[c * fc:(c + 1) * fc, :], preferred_element_type=F32)
    return acc


def _ffn_kernel(*refs, fc, n_first_blocks):
    if n_first_blocks is None:
        h_ref, g_ref, wgu_ref, wd_ref, o_ref, ub_ref = refs
        x = h_ref[...]
    else:
        hp_ref, hs_ref, g_ref, wgu_ref, wd_ref, o_ref, ub_ref = refs
        x = _pick_rows(hp_ref, hs_ref, n_first_blocks)
    o_ref[...] = _ffn_rows(x, g_ref, wgu_ref, wd_ref, ub_ref, fc)


def _ffn(h, gain, w_gate_up, w_down, *, tm=512, fc=256):
    parts = h if isinstance(h, tuple) else (h,)
    T = sum(p.shape[0] for p in parts)
    D = parts[0].shape[1]
    F = w_down.shape[0]
    if len(parts) == 2:
        nfb = parts[0].shape[0] // tm
        h_specs = list(_split_rows_specs(tm, D, nfb))
    else:
        nfb = None
        h_specs = [pl.BlockSpec((tm, D), lambda i: (i, 0))]
    return pl.pallas_call(
        functools.partial(_ffn_kernel, fc=fc, n_first_blocks=nfb),
        out_shape=jax.ShapeDtypeStruct((T, D), F32),
        grid=(T // tm,),
        in_specs=h_specs + [_resident((1, D)), _resident((D, 2 * F)), _resident((F, D))],
        out_specs=pl.BlockSpec((tm, D), lambda i: (i, 0)),
        scratch_shapes=[pltpu.VMEM((tm, D), BF16)],
        compiler_params=_cparams(1),
        name="ffn",
    )(*parts, gain.reshape(1, D), w_gate_up.astype(BF16), w_down.astype(BF16))


def _tail_kernel(*refs, fc, final, n_first_blocks, y_pairs):
    (h_ref, y_ref, wm_ref, g2_ref, wgu_ref, wd_ref, pp_ref, ps_ref, gp_ref, wi_ref, wg_ref) = refs[:11]
    if final:
        gf_ref, op_ref, os_ref, ub_ref = refs[11:]
    else:
        o_ref, ub_ref = refs[11:]
    y = jnp.concatenate([y_ref[p] for p in range(y_pairs)], axis=1) if y_pairs else y_ref[...]
    x = h_ref[...] + jnp.dot(y, wm_ref[...], preferred_element_type=F32)
    x = _ffn_rows(x, g2_ref, wgu_ref, wd_ref, ub_ref, fc)
    gate = jax.nn.sigmoid(_dot(_rms(x, gp_ref[...]), wg_ref[...]))
    x = x + _dot(_pick_rows(pp_ref, ps_ref, n_first_blocks), wi_ref[...]) * gate
    if not final:
        o_ref[...] = x
        return
    x = _rms(x, gf_ref[...])
    i = pl.program_id(0)

    @pl.when(i < n_first_blocks)
    def _():
        op_ref[...] = x

    @pl.when(i >= n_first_blocks)
    def _():
        os_ref[...] = x


def _layer_tail(h, y, w_mix, gain_ffn, w_gate_up, w_down, p_first, p_second, layer, gain_ple, w_in, w_gate,
                gain_final, *, final, tm=512, fc=256):
    T, D = h.shape
    F = w_down.shape[0]
    P = p_first.shape[2]
    nfb = p_first.shape[1] // tm
    row = pl.BlockSpec((tm, D), lambda i: (i, 0))
    y_pairs = y.shape[0] if y.ndim == 3 else 0
    y_spec = (pl.BlockSpec((y_pairs, tm, LANES), lambda i: (0, i, 0)) if y_pairs
              else pl.BlockSpec((tm, y.shape[1]), lambda i: (i, 0)))
    ops = [h, y, w_mix.astype(BF16), gain_ffn.reshape(1, D), w_gate_up.astype(BF16), w_down.astype(BF16),
           p_first, p_second, gain_ple.reshape(1, D), w_in.astype(BF16), w_gate.astype(BF16)]
    specs = [row, y_spec, _resident(w_mix.shape), _resident((1, D)), _resident((D, 2 * F)), _resident((F, D)),
             *_split_rows_specs(tm, P, nfb, lead=(layer,)), _resident((1, D)), _resident((P, D)), _resident((D, D))]
    if final:
        ops.append(gain_final.reshape(1, D))
        specs.append(_resident((1, D)))
        out_shape = (jax.ShapeDtypeStruct((p_first.shape[1], D), F32), jax.ShapeDtypeStruct((p_second.shape[1], D), F32))
        out_specs = _split_rows_specs(tm, D, nfb)
    else:
        out_shape = jax.ShapeDtypeStruct((T, D), F32)
        out_specs = row
    return pl.pallas_call(
        functools.partial(_tail_kernel, fc=fc, final=final, n_first_blocks=nfb, y_pairs=y_pairs),
        out_shape=out_shape,
        grid=(T // tm,),
        in_specs=specs,
        out_specs=out_specs,
        scratch_shapes=[pltpu.VMEM((tm, D), BF16)],
        compiler_params=_cparams(1),
        name="layer_tail",
    )(*ops)


def _m_in_kernel(h_ref, g_ref, wz_ref, wx_ref, wdt_ref, z_ref, xbc_ref, dt_ref):
    u = _rms(h_ref[...], g_ref[...]).astype(BF16)
    z_ref[...] = jnp.dot(u, wz_ref[...], preferred_element_type=F32)
    xbc_ref[...] = jnp.dot(u, wx_ref[...], preferred_element_type=F32)
    dt_ref[...] = jnp.dot(u, wdt_ref[...], preferred_element_type=F32)


def _m_in_proj(h, gain, w_in, d_inner, conv_dim, n_heads, *, tm=256):
    T, D = h.shape
    wz = w_in[:, :d_inner].astype(BF16)
    wx = w_in[:, d_inner:d_inner + conv_dim].astype(BF16)
    wdt = jnp.pad(w_in[:, d_inner + conv_dim:], ((0, 0), (0, LANES - n_heads))).astype(BF16)
    return pl.pallas_call(
        _m_in_kernel,
        out_shape=(jax.ShapeDtypeStruct((T, d_inner), F32), jax.ShapeDtypeStruct((T, conv_dim), F32),
                   jax.ShapeDtypeStruct((T, LANES), F32)),
        grid=(T // tm,),
        in_specs=[pl.BlockSpec((tm, D), lambda i: (i, 0)), _resident((1, D)),
                  _resident((D, d_inner)), _resident((D, conv_dim)), _resident((D, LANES))],
        out_specs=(pl.BlockSpec((tm, d_inner), lambda i: (i, 0)), pl.BlockSpec((tm, conv_dim), lambda i: (i, 0)),
                   pl.BlockSpec((tm, LANES), lambda i: (i, 0))),
        compiler_params=_cparams(1),
        name="m_in_proj",
    )(h, gain.reshape(1, D), wz, wx, wdt)


def _head_expanders(n_heads):
    hh = jnp.arange(LANES)[:, None]
    e64 = (hh == (jnp.arange(n_heads * M_HEADDIM)[None, :] // M_HEADDIM)).astype(BF16)
    e128 = (hh == (jnp.arange(n_heads * LANES)[None, :] // LANES)).astype(BF16)
    return jnp.tile(e64, (3, 1)), jnp.tile(e128, (3, 1))


def _ssd_prompt_kernel(*refs, n_heads, n_aliased):
    (z_ref, xbc_ref, dt_ref, cw_ref, cb_ref, dtb_ref, alog_ref, dsk_ref, nw_ref,
     e64_ref, e128_ref, tri_ref, y_ref, ssm_ref, conv_ref, xpad_ref, s_ref) = refs[n_aliased:]
    c = pl.program_id(1)
    nc = pl.num_programs(1)
    Q = M_CHUNK
    d_inner = n_heads * M_HEADDIM
    gw = d_inner // M_GROUPS
    bc0 = d_inner
    cc0 = d_inner + M_GROUPS * M_STATE

    @pl.when(c == 0)
    def _():
        xpad_ref[:, 0:8, :] = jnp.zeros((xpad_ref.shape[0], 8, LANES), F32)
        s_ref[...] = jnp.zeros(s_ref.shape, F32)

    slabs = []
    for s in range(xpad_ref.shape[0]):
        lanes = slice(s * LANES, (s + 1) * LANES)
        xpad_ref[s, 8:8 + Q, :] = xbc_ref[:, lanes]
        xc = None
        for k in range(M_CONV_W):
            lo = 8 - (M_CONV_W - 1) + k
            term = xpad_ref[s, lo:lo + Q, :] * cw_ref[k:k + 1, lanes]
            xc = term if xc is None else xc + term
        slabs.append(_silu(xc + cb_ref[:, lanes]))
        xpad_ref[s, 0:8, :] = xpad_ref[s, Q:Q + 8, :]
    xs = jnp.concatenate(slabs, axis=1)
    xh = xs[:, :d_inner]

    dt = _softplus(dt_ref[...] + dtb_ref[...])
    dta = dt * (-jnp.exp(alog_ref[...]))
    ri = lax.broadcasted_iota(jnp.int32, (Q, Q), 0)
    ci = lax.broadcasted_iota(jnp.int32, (Q, Q), 1)
    causal = ri >= ci
    acum = _dot_sel_l(tri_ref[...], dta)
    a_full = _dot_sel_r(acum, e64_ref[...])
    dt_full = _dot_sel_r(dt, e64_ref[...])
    a_col = _dot_sel_r(acum, e128_ref[...])
    acum_t = acum.T
    a_last = a_full[Q - 1:Q, :]
    exp_a = jnp.exp(a_full)
    xdt = xh * dt_full
    xw = xdt * jnp.exp(a_last - a_full)
    chunk_decay = jnp.exp(a_last)
    lane = lax.broadcasted_iota(jnp.int32, (Q, LANES), 1)
    first_head = lane < M_HEADDIM
    hpg = n_heads // M_GROUPS

    for g in range(M_GROUPS):
        bg = xs[:, bc0 + g * M_STATE: bc0 + (g + 1) * M_STATE]
        cg = xs[:, cc0 + g * M_STATE: cc0 + (g + 1) * M_STATE]
        cb = _dot_nt(cg, bg)
        sl = slice(g * gw, (g + 1) * gw)
        s_g = s_ref[:, sl]
        y_g = _dot(cg, s_g) * exp_a[:, sl]
        parts = []
        for q in range(hpg // 2):
            h0 = g * hpg + 2 * q
            xp = xdt[:, h0 * M_HEADDIM: h0 * M_HEADDIM + LANES]
            ys = []
            for hh in (h0, h0 + 1):
                seg = a_col[:, hh * LANES:(hh + 1) * LANES] - acum_t[hh:hh + 1, :]
                m = cb * jnp.exp(jnp.where(causal, seg, -jnp.inf))
                ys.append(_dot(m, xp))
            parts.append(jnp.where(first_head, ys[0], ys[1]))
        y_g = y_g + jnp.concatenate(parts, axis=1) + dsk_ref[:, sl] * xh[:, sl]
        y_g = y_g * _silu(z_ref[:, sl])
        y_g = y_g * lax.rsqrt(jnp.mean(y_g * y_g, axis=-1, keepdims=True) + M_NORM_EPS)
        y_ref[:, sl] = (y_g * nw_ref[:, sl]).astype(y_ref.dtype)
        s_ref[:, sl] = s_g * chunk_decay[:, sl] + _dot(bg.T, xw[:, sl])

    @pl.when(c == nc - 1)
    def _():
        ssm_ref[0] = s_ref[...].T.reshape(n_heads, M_HEADDIM, M_STATE)
        conv_ref[0] = xbc_ref[Q - (M_CONV_W - 1):Q, :]


def _alias_prev(prev, first_out):
    prev = tuple(prev or ())
    specs = [pl.BlockSpec(memory_space=pl.ANY)] * len(prev)
    return prev, specs, {k: first_out + k for k in range(len(prev))}


def _ssd_prompt(z, xbc, dt, prm, B, L, n_heads, layer, n_layers, prev):
    T, d_inner = z.shape
    conv_dim = xbc.shape[1]
    Q = M_CHUNK
    nc = L // Q
    row = lambda b, c: (b * nc + c, 0)
    prev, prev_specs, aliases = _alias_prev(prev, first_out=1)
    kern = functools.partial(_ssd_prompt_kernel, n_heads=n_heads, n_aliased=len(prev))
    return pl.pallas_call(
        kern,
        out_shape=(jax.ShapeDtypeStruct((T, d_inner), BF16),
                   jax.ShapeDtypeStruct((n_layers, B, n_heads, M_HEADDIM, M_STATE), F32),
                   jax.ShapeDtypeStruct((n_layers, B, M_CONV_W - 1, conv_dim), F32)),
        grid=(B, nc),
        in_specs=prev_specs + [
            pl.BlockSpec((Q, d_inner), row), pl.BlockSpec((Q, conv_dim), row), pl.BlockSpec((Q, LANES), row),
            _resident((M_CONV_W, conv_dim)), _resident((1, conv_dim)),
            _resident((1, LANES)), _resident((1, LANES)), _resident((1, d_inner)), _resident((1, d_inner)),
            _resident((3 * LANES, d_inner)), _resident((3 * LANES, n_heads * LANES)), _resident((Q, 3 * Q))],
        out_specs=(pl.BlockSpec((Q, d_inner), row),
                   pl.BlockSpec((None, 1, n_heads, M_HEADDIM, M_STATE), lambda b, c: (layer, b, 0, 0, 0)),
                   pl.BlockSpec((None, 1, M_CONV_W - 1, conv_dim), lambda b, c: (layer, b, 0, 0))),
        scratch_shapes=[pltpu.VMEM((conv_dim // LANES, Q + 8, LANES), F32), pltpu.VMEM((M_STATE, d_inner), F32)],
        input_output_aliases=aliases,
        compiler_params=_cparams(2),
        name="ssd_prompt",
    )(*prev, z, xbc, dt, prm["conv_w"], prm["conv_b"], prm["dt_bias"], prm["a_log"], prm["d_full"], prm["norm_w"],
      prm["e64"], prm["e128"], prm["tri"])


def _ssd_sample_kernel(*refs, n_heads, seq, nb, n_aliased):
    (z_ref, xbc_ref, dt_ref, s0_ref, c0_ref, cw_ref, cb_ref, dtb_ref, alog_ref,
     dsk_ref, nw_ref, e64_ref, e128_ref, rowsel_ref, y_ref, ssm_ref, conv_ref, hist_ref) = refs[n_aliased:]
    d_inner = n_heads * M_HEADDIM
    gw = d_inner // M_GROUPS
    bc0 = d_inner
    cc0 = d_inner + M_GROUPS * M_STATE
    hpg = n_heads // M_GROUPS
    rows = nb * seq
    row = lax.broadcasted_iota(jnp.int32, (rows, 1), 0)
    tok = jnp.bitwise_and(row, seq - 1)
    seq_of_row = jnp.right_shift(row, int(math.log2(seq)))
    seq_of_col = jnp.right_shift(lax.broadcasted_iota(jnp.int32, (1, rows), 1), int(math.log2(seq)))
    eye = (lax.broadcasted_iota(jnp.int32, (LANES, LANES), 0)
           == lax.broadcasted_iota(jnp.int32, (LANES, LANES), 1)).astype(BF16)
    top_head = lax.broadcasted_iota(jnp.int32, (LANES, 1), 0) < M_HEADDIM

    @pl.when(pl.program_id(0) == 0)
    def _():
        hist_ref[...] = jnp.zeros(hist_ref.shape, F32)

    x = xbc_ref[...]
    for s in range(1, M_CONV_W):
        for bi in range(nb):
            hist_ref[s - 1, bi * seq:bi * seq + s, :] = c0_ref[bi, M_CONV_W - 1 - s:M_CONV_W - 1, :]
    xc = None
    for k in range(M_CONV_W):
        s = M_CONV_W - 1 - k
        tap = x if s == 0 else jnp.where(tok >= s, pltpu.roll(x, s, 0), hist_ref[s - 1])
        term = tap * cw_ref[k:k + 1, :]
        xc = term if xc is None else xc + term
    xs = _silu(xc + cb_ref[...])
    xh = xs[:, :d_inner]
    for bi in range(nb):
        conv_ref[bi] = xbc_ref[(bi + 1) * seq - (M_CONV_W - 1):(bi + 1) * seq, :]

    dt = _softplus(dt_ref[...] + dtb_ref[...])
    acum = dt * (-jnp.exp(alog_ref[...]))
    s = 1
    while s < seq:
        acum = acum + jnp.where(tok >= s, pltpu.roll(acum, s, 0), 0.0)
        s *= 2
    a_full = _dot_sel_r(acum, e64_ref[...])
    dt_full = _dot_sel_r(dt, e64_ref[...])
    a_col = _dot_sel_r(acum, e128_ref[...])
    exp_a = jnp.exp(a_full)
    xdt = xh * dt_full
    packed = jnp.concatenate([a_full, xdt, xs[:, bc0:cc0]], axis=1)
    from_tok = [_dot_sel_l(rowsel_ref[j], packed) for j in range(seq)]
    aux = xdt * jnp.exp(from_tok[seq - 1][:, :d_inner] - a_full)

    y = dsk_ref[...] * xh
    for j in range(seq):
        a_j = from_tok[j][:, :d_inner]
        wj = jnp.where(tok >= j, jnp.exp(a_full - a_j), 0.0) * from_tok[j][:, d_inner:2 * d_inner]
        parts = []
        for g in range(M_GROUPS):
            cgb = (xs[:, cc0 + g * M_STATE: cc0 + (g + 1) * M_STATE]
                   * from_tok[j][:, 2 * d_inner + g * M_STATE: 2 * d_inner + (g + 1) * M_STATE])
            parts.append(jnp.sum(cgb, axis=-1, keepdims=True) * wj[:, g * gw:(g + 1) * gw])
        y = y + jnp.concatenate(parts, axis=1)

    inter = []
    for pr in range(n_heads // 2):
        g = (2 * pr) // hpg
        cg = xs[:, cc0 + g * M_STATE: cc0 + (g + 1) * M_STATE]
        bg = xs[:, bc0 + g * M_STATE: bc0 + (g + 1) * M_STATE]
        aux_t = _dot_nt(eye, aux[:, pr * LANES:(pr + 1) * LANES])
        acc = None
        for bi in range(nb):
            h0 = s0_ref[bi, 2 * pr:2 * pr + 2].reshape(2 * M_HEADDIM, M_STATE)
            t = jnp.where(seq_of_row == bi, _dot_nt(cg, h0), 0.0)
            acc = t if acc is None else acc + t
            last = (bi + 1) * seq - 1
            dec = jnp.where(top_head, jnp.exp(a_col[last:last + 1, (2 * pr) * LANES:(2 * pr + 1) * LANES]),
                            jnp.exp(a_col[last:last + 1, (2 * pr + 1) * LANES:(2 * pr + 2) * LANES]))
            upd = _dot(jnp.where(seq_of_col == bi, aux_t, 0.0), bg)
            ssm_ref[bi, 2 * pr:2 * pr + 2] = (h0 * dec + upd).reshape(2, M_HEADDIM, M_STATE)
        inter.append(acc)
    y = y + jnp.concatenate(inter, axis=1) * exp_a

    y = y * _silu(z_ref[...])
    outs = []
    for g in range(M_GROUPS):
        yg = y[:, g * gw:(g + 1) * gw]
        outs.append(yg * lax.rsqrt(jnp.mean(yg * yg, axis=-1, keepdims=True) + M_NORM_EPS))
    y_ref[...] = (jnp.concatenate(outs, axis=1) * nw_ref[...]).astype(y_ref.dtype)


def _ssd_sample(y_all, z, xbc, dt, s0, c0, prm, row0, seq, n_heads, layer, prev, *, nb=4):
    T, d_inner = z.shape
    conv_dim = xbc.shape[1]
    DB = s0.shape[1]
    assert seq & (seq - 1) == 0 and seq >= M_CONV_W - 1
    rows = nb * seq
    r = jnp.arange(rows)
    rowsel = jnp.stack([jnp.tile((r[None, :] == ((r // seq) * seq + j)[:, None]).astype(BF16), (1, 3))
                        for j in range(seq)])
    blk0 = row0 // rows
    row = lambda i: (blk0 + i, 0)
    s_spec = pl.BlockSpec((None, nb, n_heads, M_HEADDIM, M_STATE), lambda i: (layer, i, 0, 0, 0))
    c_spec = pl.BlockSpec((None, nb, M_CONV_W - 1, conv_dim), lambda i: (layer, i, 0, 0))
    aliased, alias_specs, aliases = _alias_prev((y_all,) + tuple(prev or ()), first_out=0)
    kern = functools.partial(_ssd_sample_kernel, n_heads=n_heads, seq=seq, nb=nb, n_aliased=len(aliased))
    return pl.pallas_call(
        kern,
        out_shape=(jax.ShapeDtypeStruct((T, d_inner), y_all.dtype),
                   jax.ShapeDtypeStruct(s0.shape, F32), jax.ShapeDtypeStruct(c0.shape, F32)),
        grid=(DB // nb,),
        in_specs=alias_specs + [
            pl.BlockSpec((rows, d_inner), row), pl.BlockSpec((rows, conv_dim), row),
            pl.BlockSpec((rows, LANES), row), s_spec, c_spec,
            _resident((M_CONV_W, conv_dim)), _resident((1, conv_dim)), _resident((1, LANES)),
            _resident((1, LANES)), _resident((1, d_inner)), _resident((1, d_inner)),
            _resident((3 * LANES, d_inner)), _resident((3 * LANES, n_heads * LANES)),
            _resident((seq, rows, 3 * rows))],
        out_specs=(pl.BlockSpec((rows, d_inner), row), s_spec, c_spec),
        scratch_shapes=[pltpu.VMEM((M_CONV_W - 1, rows, conv_dim), F32)],
        input_output_aliases=aliases,
        compiler_params=_cparams(1),
        name="ssd_sample",
    )(*aliased, z, xbc, dt, s0, c0, prm["conv_w"], prm["conv_b"], prm["dt_bias"], prm["a_log"], prm["d_full"],
      prm["norm_w"], prm["e64"], prm["e128"], rowsel)


def _r_proj_kernel(*refs, tm, blocks_per_seq, n_prompt_blocks, sample_seq, has_vres):
    if has_vres:
        (h_ref, g_ref, pshift_ref, srows_ref, mu_ref, wr_ref, wk_ref, wv_ref, w1_ref, w2_ref, a1_ref, a2_ref,
         g1_ref, g2_ref, w0_ref, a0_ref, kk_ref, ka_ref, vf_ref, v0_ref, v1_ref, v2_ref,
         r_o, k_o, v_o, kkr_o, a_o, w_o, g_o, up_o, us_o, sc_ref, xx_ref) = refs
    else:
        (h_ref, g_ref, pshift_ref, srows_ref, mu_ref, wr_ref, wk_ref, wv_ref, w1_ref, w2_ref, a1_ref, a2_ref,
         g1_ref, g2_ref, w0_ref, a0_ref, kk_ref, ka_ref,
         r_o, k_o, v_o, kkr_o, a_o, w_o, g_o, up_o, us_o, sc_ref, xx_ref) = refs
    i = pl.program_id(0)
    u = _rms(h_ref[...], g_ref[...])
    sc_ref[8:8 + tm, :] = u

    @pl.when(jnp.logical_and(i % blocks_per_seq == blocks_per_seq - 1, i < n_prompt_blocks))
    def _():
        up_o[0] = u[tm - 8:tm, :]

    @pl.when(i >= n_prompt_blocks)
    def _():
        us_o[...] = u

    @pl.when(jnp.logical_and(i % blocks_per_seq == 0, i < n_prompt_blocks))
    def _():
        sc_ref[7:8, :] = pshift_ref[pl.ds(i // blocks_per_seq, 1), :]

    prev = sc_ref[7:7 + tm, :]
    row = lax.broadcasted_iota(jnp.int32, (tm, 1), 0)
    use_state = jnp.logical_and(i >= n_prompt_blocks, jnp.bitwise_and(row, sample_seq - 1) == 0)
    prev = jnp.where(use_state, srows_ref[...], prev)
    sc_ref[7:8, :] = sc_ref[7 + tm:8 + tm, :]

    xx_ref[...] = prev - u
    npair = r_o.shape[0]

    def rows(r0, n):
        u_h = sc_ref[8 + r0:8 + r0 + n, :]
        xx_h = xx_ref[r0:r0 + n, :]

        def mix(m):
            return u_h + xx_h * mu_ref[m:m + 1, :]

        xv = mix(3)
        r = _dot(mix(0), wr_ref[...])
        k = _dot(mix(2), wk_ref[...])
        v = _dot(xv, wv_ref[...])
        yield
        lw = _dot(mix(1), w1_ref[...])
        la = _dot(mix(4), a1_ref[...])
        lg = _dot(mix(5), g1_ref[...])
        lv = _dot(xv, v1_ref[...]) if has_vres else None
        yield
        w = -_softplus(-(w0_ref[...] + _dot(jnp.tanh(lw), w2_ref[...]))) - 0.5
        a = jax.nn.sigmoid(a0_ref[...] + _dot(la, a2_ref[...]))
        g = _dot(jax.nn.sigmoid(lg), g2_ref[...])
        if has_vres:
            vf = jnp.concatenate([vf_ref[p, r0:r0 + n, :] for p in range(npair)], axis=1)
            v = v + (vf - v) * jax.nn.sigmoid(v0_ref[...] + _dot(lv, v2_ref[...]))
        yield
        kkr = k * kk_ref[...]
        k = k * (1.0 + (a - 1.0) * ka_ref[...])
        for p in range(npair):
            sl = slice(p * LANES, (p + 1) * LANES)
            r_o[p, r0:r0 + n, :] = r[:, sl]
            k_o[p, r0:r0 + n, :] = k[:, sl]
            v_o[p, r0:r0 + n, :] = v[:, sl]
            kkr_o[p, r0:r0 + n, :] = kkr[:, sl]
            a_o[p, r0:r0 + n, :] = a[:, sl]
            w_o[p, r0:r0 + n, :] = w[:, sl]
            g_o[p, r0:r0 + n, :] = g[:, sl]

    n_groups = 2
    _interleave((rows(q * (tm // n_groups), tm // n_groups) for q in range(n_groups)), skew=1)


def _pad_cols(w, n):
    return jnp.pad(w, ((0, 0), (0, n - w.shape[1])))


def _pad_rows(w, n):
    return jnp.pad(w, ((0, n - w.shape[0]), (0, 0)))


def _round_up(n, m):
    return -(-n // m) * m


def _r_proj(h, gain, pshift, srows, rp, v_first, L, n_prompt_rows, sample_seq, *, tm=256):
    T, D = h.shape
    npair = D // LANES
    has_vres = v_first is not None
    row2 = lambda i: (i, 0)
    npb = n_prompt_rows // tm
    lw = _round_up(rp["w1"].shape[1], LANES)
    la = _round_up(rp["a1"].shape[1], LANES)
    lg = _round_up(rp["g1"].shape[1], LANES)
    ops = [h, gain.reshape(1, D), pshift, srows, rp["mu"],
           rp["wr"].astype(BF16), rp["wk"].astype(BF16), rp["wv"].astype(BF16),
           _pad_cols(rp["w1"], lw).astype(BF16), _pad_rows(rp["w2"], lw).astype(BF16),
           _pad_cols(rp["a1"], la).astype(BF16), _pad_rows(rp["a2"], la).astype(BF16),
           _pad_cols(rp["g1"], lg).astype(BF16), _pad_rows(rp["g2"], lg).astype(BF16),
           rp["w0"].reshape(1, D), rp["a0"].reshape(1, D), rp["k_k"].reshape(1, D), rp["k_a"].reshape(1, D)]
    specs = [pl.BlockSpec((tm, D), row2), _resident((1, D)), _resident(pshift.shape),
             pl.BlockSpec((tm, D), lambda i: (jnp.maximum(i - npb, 0), 0)), _resident((6, D)),
             _resident((D, D)), _resident((D, D)), _resident((D, D)),
             _resident((D, lw)), _resident((lw, D)), _resident((D, la)), _resident((la, D)),
             _resident((D, lg)), _resident((lg, D)),
             _resident((1, D)), _resident((1, D)), _resident((1, D)), _resident((1, D))]
    pair_spec = pl.BlockSpec((npair, tm, LANES), lambda i: (0, i, 0))
    if has_vres:
        lv = _round_up(rp["v1"].shape[1], LANES)
        ops += [v_first, rp["v0"].reshape(1, D), _pad_cols(rp["v1"], lv).astype(BF16),
                _pad_rows(rp["v2"], lv).astype(BF16)]
        specs += [pair_spec, _resident((1, D)), _resident((D, lv)), _resident((lv, D))]
    pair_shape = jax.ShapeDtypeStruct((npair, T, LANES), F32)
    bps = L // tm
    n_seq = n_prompt_rows // L
    kern = functools.partial(_r_proj_kernel, tm=tm, blocks_per_seq=bps, n_prompt_blocks=npb,
                             sample_seq=sample_seq, has_vres=has_vres)
    return pl.pallas_call(
        kern,
        out_shape=(pair_shape,) * 7 + (jax.ShapeDtypeStruct((n_seq, 8, D), F32),
                                       jax.ShapeDtypeStruct((T - n_prompt_rows, D), F32)),
        grid=(T // tm,),
        in_specs=specs,
        out_specs=(pair_spec,) * 7 + (
            pl.BlockSpec((1, 8, D), lambda i: (jnp.minimum(i // bps, n_seq - 1), 0, 0)),
            pl.BlockSpec((tm, D), lambda i: (jnp.maximum(i - npb, 0), 0))),
        scratch_shapes=[pltpu.VMEM((tm + 8, D), F32), pltpu.VMEM((tm, D), F32)],
        compiler_params=_cparams(1),
        name="r_proj",
    )(*ops)


def _wkv_pair(r, k, v, kkr, a, w, g, rk, gnw, gnb, incl3, same3, states, lseq):
    C = r.shape[0]
    ns = C // lseq
    lane = lax.broadcasted_iota(jnp.int32, (1, LANES), 1)
    m0 = lane < R_HEADDIM

    def headsum(x):
        s0 = jnp.sum(jnp.where(m0, x, 0.0), axis=-1, keepdims=True)
        s1 = jnp.sum(jnp.where(m0, 0.0, x), axis=-1, keepdims=True)
        return jnp.where(m0, s0, s1)

    kk = kkr * lax.rsqrt(jnp.maximum(headsum(kkr * kkr), 1e-24))
    alpha = -kk
    beta = kk * a
    logd = -jnp.exp(w)

    ti = lax.broadcasted_iota(jnp.int32, (C, C), 0)
    si = lax.broadcasted_iota(jnp.int32, (C, C), 1)
    sh = int(math.log2(lseq))
    same = jnp.right_shift(ti, sh) == jnp.right_shift(si, sh)
    incl = jnp.logical_and(same, si <= ti)
    strict = jnp.logical_and(same, si < ti)
    lg = _dot_sel_l(incl3, logd)
    ltot = _dot_sel_l(same3, logd)
    yield
    g_in = jnp.exp(lg)
    ginv = jnp.exp(-lg)
    gend = jnp.exp(ltot - lg)
    w0 = alpha * jnp.exp(lg - logd)
    rg = r * g_in
    bi = beta * ginv
    ki = k * ginv
    wr = jnp.concatenate([w0, rg], axis=0)
    rsel = jnp.right_shift(jnp.bitwise_and(lax.broadcasted_iota(jnp.int32, (2 * C, 1), 0), C - 1), sh)

    xy0 = None
    for s in range(ns):
        t = _dot_nt(wr, states[s])
        if ns > 1:
            t = jnp.where(rsel == s, t, 0.0)
        xy0 = t if xy0 is None else xy0 + t
    x0 = xy0[:C]
    y0 = xy0[C:]
    yield

    eye = (ti == si).astype(F32)
    heads = (m0, jnp.logical_not(m0))
    wrm = [jnp.where(mh, wr, 0.0) for mh in heads]
    gb = [_dot_nt(x, bi) for x in wrm]
    gk = [_dot_nt(x, ki) for x in wrm]
    yield
    a_ab = [jnp.where(strict, x[:C], 0.0) for x in gb]
    a_ak = [jnp.where(strict, x[:C], 0.0) for x in gk]
    b_rb = [jnp.where(incl, x[C:], 0.0) for x in gb]
    b_rk = [jnp.where(incl, x[C:], 0.0) for x in gk]
    tinv = [eye + x for x in a_ab]
    pw = a_ab
    for _ in range(int(math.log2(lseq)) - 1):
        pw = [_dot(x, x) for x in pw]
        yield
        tinv = [t + _dot(t, x) for t, x in zip(tinv, pw)]
    xh = [x0 + _dot(x, v) for x in a_ak]
    yv = [_dot(x, v) for x in b_rk]
    yield
    uh = [_dot(t, x) for t, x in zip(tinv, xh)]
    yield
    yh = [y0 + _dot(b, x) + c for b, x, c in zip(b_rb, uh, yv)]
    u = jnp.where(m0, uh[0], uh[1])
    y = jnp.where(m0, yh[0], yh[1])
    yield

    uvt = jnp.concatenate([u, v], axis=0).T
    bkd = jnp.concatenate([beta * gend, k * gend], axis=0)
    bi_r = lax.broadcasted_iota(jnp.int32, (LANES, LANES), 0) >= R_HEADDIM
    bi_c = lax.broadcasted_iota(jnp.int32, (LANES, LANES), 1) >= R_HEADDIM
    same_head = bi_r == bi_c
    new_states = []
    for s in range(ns):
        rhs = bkd if ns == 1 else jnp.where(rsel == s, bkd, 0.0)
        ds = jnp.where(same_head, _dot(uvt, rhs), 0.0)
        new_states.append(states[s] * jnp.exp(ltot[s * lseq:s * lseq + 1, :]) + ds)
    yield

    mean = headsum(y) * (1.0 / R_HEADDIM)
    d = y - mean
    var = headsum(d * d) * (1.0 / R_HEADDIM)
    yn = d * lax.rsqrt(var + R_GN_EPS) * gnw + gnb
    out = (yn + headsum(r * k * rk) * v) * g
    return out, new_states


def _wkv_pair_chunk(r, k, v, kkr, a, w, g, rk, gnw, gnb, incl3, state_ref, y_ref):
    C = r.shape[0]
    H = 2 * C
    lane = lax.broadcasted_iota(jnp.int32, (1, LANES), 1)
    m0 = lane < R_HEADDIM

    def headsum(x):
        s0 = jnp.sum(jnp.where(m0, x, 0.0), axis=-1, keepdims=True)
        s1 = jnp.sum(jnp.where(m0, 0.0, x), axis=-1, keepdims=True)
        return jnp.where(m0, s0, s1)

    def stack(x):
        return jnp.concatenate([jnp.where(m0, x, 0.0), jnp.where(m0, 0.0, x)], axis=0)

    kk = kkr * lax.rsqrt(jnp.maximum(headsum(kkr * kkr), 1e-24))
    beta = kk * a
    logd = -jnp.exp(w)
    lg = _dot_sel_l(incl3, logd)
    yield
    ltot = lg[C - 1:C, :]
    ginv = jnp.exp(-lg)
    gend = jnp.exp(ltot - lg)
    w0s = stack(-kk * jnp.exp(lg - logd))
    rgs = stack(r * jnp.exp(lg))
    vs = stack(v)
    bds = stack(beta * gend)
    kds = stack(k * gend)
    gram = _dot_nt(jnp.concatenate([w0s, rgs], axis=0),
                   jnp.concatenate([stack(beta * ginv), stack(k * ginv)], axis=0))
    yield
    tr = jnp.bitwise_and(lax.broadcasted_iota(jnp.int32, (H, H), 0), C - 1)
    sc = jnp.bitwise_and(lax.broadcasted_iota(jnp.int32, (H, H), 1), C - 1)
    strict = sc < tr
    incl = sc <= tr
    a_ab = jnp.where(strict, gram[:H, :H], 0.0)
    a_ak = jnp.where(strict, gram[:H, H:], 0.0)
    b_rb = jnp.where(incl, gram[H:, :H], 0.0)
    b_rk = jnp.where(incl, gram[H:, H:], 0.0)
    eye = (lax.broadcasted_iota(jnp.int32, (H, H), 0) == lax.broadcasted_iota(jnp.int32, (H, H), 1)).astype(F32)
    tinv = eye + a_ab
    pw = _dot(a_ab, a_ab)
    av = _dot(a_ak, vs)
    bv = _dot(b_rk, vs)
    yield
    for _ in range(int(math.log2(C)) - 2):
        tinv = tinv + _dot(tinv, pw)
        pw = _dot(pw, pw)
        yield
    tinv = tinv + _dot(tinv, pw)
    yield
    tw = _dot(tinv, jnp.concatenate([w0s, av], axis=1))
    yield
    br = _dot(b_rb, tw)
    m_t = _dot(tw[:, :LANES].T, bds)
    z = _dot(jnp.concatenate([tw[:, LANES:], vs], axis=0).T, jnp.concatenate([bds, kds], axis=0))
    yield
    r_t = rgs + br[:, :LANES]
    y_t = br[:, LANES:] + bv
    state = state_ref[...]
    ys = _dot_nt(r_t, state) + y_t
    state_ref[...] = state * jnp.exp(ltot) + _dot(state, m_t) + z
    yield
    y = ys[:C] + ys[C:]
    mean = headsum(y) * (1.0 / R_HEADDIM)
    d = y - mean
    var = headsum(d * d) * (1.0 / R_HEADDIM)
    yn = d * lax.rsqrt(var + R_GN_EPS) * gnw + gnb
    y_ref[...] = ((yn + headsum(r * k * rk) * v) * g).astype(y_ref.dtype)


def _interleave(generators, skew=0):
    waiting = list(enumerate(generators))
    live = []
    rnd = 0
    while waiting or live:
        live += [gen for q, gen in waiting if q * skew <= rnd]
        waiting = [(q, gen) for q, gen in waiting if q * skew > rnd]
        nxt = []
        for gen in live:
            try:
                next(gen)
                nxt.append(gen)
            except StopIteration:
                pass
        live = nxt
        rnd += 1


def _wkv_prompt_kernel(*refs, chunks_per_seq, n_aliased):
    (r_ref, k_ref, v_ref, kkr_ref, a_ref, w_ref, g_ref, rk_ref, gnw_ref, gnb_ref, incl_ref,
     y_ref, s_out_ref, s_ref) = refs[n_aliased:]
    c = pl.program_id(0) % chunks_per_seq
    npair = r_ref.shape[0]

    @pl.when(c == 0)
    def _():
        s_ref[...] = jnp.zeros(s_ref.shape, F32)

    _interleave(_wkv_pair_chunk(r_ref[p], k_ref[p], v_ref[p], kkr_ref[p], a_ref[p], w_ref[p], g_ref[p],
                                rk_ref[p], gnw_ref[p], gnb_ref[p], incl_ref[...], s_ref.at[p], y_ref.at[p])
                for p in range(npair))

    @pl.when(c == chunks_per_seq - 1)
    def _():
        for p in range(npair):
            s = s_ref[p]
            s_out_ref[0, 2 * p:2 * p + 2] = (s[:, :R_HEADDIM] + s[:, R_HEADDIM:]).reshape(2, R_HEADDIM, R_HEADDIM)


def _wkv_prompt(streams, rk, gnw, gnb, B, L, layer, n_layers, prev):
    npair, T, _ = streams[0].shape
    C = R_CHUNK
    cps = L // C
    blk = pl.BlockSpec((npair, C, LANES), lambda i: (0, i, 0))
    vec = _resident((npair, 1, LANES))
    prev, prev_specs, aliases = _alias_prev(prev, first_out=1)
    return pl.pallas_call(
        functools.partial(_wkv_prompt_kernel, chunks_per_seq=cps, n_aliased=len(prev)),
        out_shape=(jax.ShapeDtypeStruct((npair, T, LANES), BF16),
                   jax.ShapeDtypeStruct((n_layers, B, 2 * npair, R_HEADDIM, R_HEADDIM), F32)),
        grid=(B * cps,),
        in_specs=prev_specs + [blk] * 7 + [vec] * 3 + [_resident((C, 3 * C))],
        out_specs=(blk, pl.BlockSpec((None, 1, 2 * npair, R_HEADDIM, R_HEADDIM),
                                     lambda i: (layer, i // cps, 0, 0, 0))),
        scratch_shapes=[pltpu.VMEM((npair, LANES, LANES), F32)],
        input_output_aliases=aliases,
        compiler_params=_cparams(1),
        name="wkv_prompt",
    )(*prev, *streams, rk, gnw, gnb, _within_sequence_selectors(C, C)[0])


def _wkv_sample_kernel(*refs, seq, n_aliased):
    (r_ref, k_ref, v_ref, kkr_ref, a_ref, w_ref, g_ref, rk_ref, gnw_ref, gnb_ref, incl_ref, same_ref,
     s0_ref, y_ref, s_out_ref) = refs[n_aliased:]
    npair = r_ref.shape[0]
    ns = R_CHUNK // seq
    lane = lax.broadcasted_iota(jnp.int32, (LANES, LANES), 1) >= R_HEADDIM
    rowh = lax.broadcasted_iota(jnp.int32, (LANES, LANES), 0) >= R_HEADDIM
    same_head = lane == rowh

    def pair(p):
        states = []
        for s in range(ns):
            s2 = s0_ref[s, 2 * p:2 * p + 2].reshape(LANES, R_HEADDIM)
            states.append(jnp.where(same_head, jnp.concatenate([s2, s2], axis=1), 0.0))
        out, new = yield from _wkv_pair(r_ref[p], k_ref[p], v_ref[p], kkr_ref[p], a_ref[p], w_ref[p], g_ref[p],
                                        rk_ref[p], gnw_ref[p], gnb_ref[p], incl_ref[...], same_ref[...],
                                        states, seq)
        y_ref[p] = out.astype(y_ref.dtype)
        for s in range(ns):
            s_out_ref[s, 2 * p:2 * p + 2] = (new[s][:, :R_HEADDIM] + new[s][:, R_HEADDIM:]).reshape(
                2, R_HEADDIM, R_HEADDIM)

    _interleave(pair(p) for p in range(npair))


def _wkv_sample(y_all, streams, rk, gnw, gnb, s0, row0, seq, layer, prev):
    npair, T, _ = streams[0].shape
    C = R_CHUNK
    ns = C // seq
    DB = s0.shape[1]
    blk0 = row0 // C
    blk = pl.BlockSpec((npair, C, LANES), lambda i: (0, blk0 + i, 0))
    vec = _resident((npair, 1, LANES))
    sblk = pl.BlockSpec((None, ns, 2 * npair, R_HEADDIM, R_HEADDIM), lambda i: (layer, i, 0, 0, 0))
    aliased, alias_specs, aliases = _alias_prev((y_all,) + tuple(prev or ()), first_out=0)
    return pl.pallas_call(
        functools.partial(_wkv_sample_kernel, seq=seq, n_aliased=len(aliased)),
        out_shape=(jax.ShapeDtypeStruct((npair, T, LANES), BF16), jax.ShapeDtypeStruct(s0.shape, F32)),
        grid=(DB // ns,),
        in_specs=alias_specs + [blk] * 7 + [vec] * 3 + [_resident((C, 3 * C))] * 2 + [sblk],
        out_specs=(blk, sblk),
        input_output_aliases=aliases,
        compiler_params=_cparams(1),
        name="wkv_sample",
    )(*aliased, *streams, rk, gnw, gnb, *_within_sequence_selectors(C, seq), s0)


def kernel(x_prompt, x_sample, p_prompt, p_sample, state_ssm, state_conv, state_wkv, state_shift, norm_ffn1, ffn1_gate_up, ffn1_down, norm_mix, norm_ffn2, ffn2_gate_up, ffn2_down, norm_ple, ple_in, ple_gate, norm_final, m_in_proj, m_conv_w, m_conv_b, m_dt_bias, m_A_log, m_D, m_norm, m_out_proj, r_mu, r_wr, r_wk, r_wv, r_wo, r_w0, r_w1, r_w2, r_a0, r_a1, r_a2, r_g1, r_g2, r_k_k, r_k_a, r_r_k, r_gn_w, r_gn_b, r_v0, r_v1, r_v2):
    B, L, D = x_prompt.shape
    DB, DL, _ = x_sample.shape
    depth = norm_ffn1.shape[0]
    TP, TS = B * L, DB * DL
    n_heads = m_A_log.shape[1]
    d_inner = n_heads * M_HEADDIM
    conv_dim = m_conv_w.shape[2]
    npair = D // LANES

    h = (x_prompt.reshape(TP, D), x_sample.reshape(TS, D))
    pp = p_prompt.reshape(depth, TP, -1)
    ps = p_sample.reshape(depth, TS, -1)
    e64, e128 = _head_expanders(n_heads)
    n_mamba = state_ssm.shape[0]
    n_rwkv = state_wkv.shape[0]

    tri = _within_sequence_selectors(M_CHUNK, M_CHUNK)[0]

    m_prompt = m_sample = wkv_p = wkv_s = None
    shift_p, shift_s = [], []
    v_first = None
    for i in range(depth):
        j = i // 2
        h = _ffn(h, norm_ffn1[i], ffn1_gate_up[i], ffn1_down[i])
        if i % 2 == 0:
            prm = dict(conv_w=m_conv_w[j], conv_b=m_conv_b[j].reshape(1, conv_dim),
                       dt_bias=jnp.pad(m_dt_bias[j], (0, LANES - n_heads)).reshape(1, LANES),
                       a_log=jnp.pad(m_A_log[j], (0, LANES - n_heads)).reshape(1, LANES),
                       d_full=jnp.repeat(m_D[j], M_HEADDIM).reshape(1, d_inner),
                       norm_w=m_norm[j].reshape(1, d_inner), e64=e64, e128=e128, tri=tri)
            z, xbc, dt = _m_in_proj(h, norm_mix[i], m_in_proj[j], d_inner, conv_dim, n_heads)
            y, *m_prompt = _ssd_prompt(z, xbc, dt, prm, B, L, n_heads, j, n_mamba, m_prompt)
            y, *m_sample = _ssd_sample(y, z, xbc, dt, state_ssm, state_conv, prm, TP, DL, n_heads, j, m_sample)
            w_mix = m_out_proj[j]
        else:
            rp = dict(mu=r_mu[j], wr=r_wr[j], wk=r_wk[j], wv=r_wv[j], w0=r_w0[j], w1=r_w1[j], w2=r_w2[j],
                      a0=r_a0[j], a1=r_a1[j], a2=r_a2[j], g1=r_g1[j], g2=r_g2[j], k_k=r_k_k[j], k_a=r_k_a[j])
            if j > 0:
                rp.update(v0=r_v0[j - 1], v1=r_v1[j - 1], v2=r_v2[j - 1])
            pshift = jnp.zeros((B, D), F32)
            srows = jnp.zeros((DB, DL, D), F32).at[:, 0].set(state_shift[j]).reshape(TS, D)
            outs = _r_proj(h, norm_mix[i], pshift, srows, rp, v_first if j > 0 else None, L, TP, DL)
            streams, u_tail, u_sample = outs[:7], outs[7], outs[8]
            if j == 0:
                v_first = streams[2]
            rk = r_r_k[j].reshape(npair, 1, LANES)
            gnw = r_gn_w[j].reshape(npair, 1, LANES)
            gnb = r_gn_b[j].reshape(npair, 1, LANES)
            y, wkv_p = _wkv_prompt(streams, rk, gnw, gnb, B, L, j, n_rwkv, None if wkv_p is None else (wkv_p,))
            y, wkv_s = _wkv_sample(y, streams, rk, gnw, gnb, state_wkv, TP, DL, j, None if wkv_s is None else (wkv_s,))
            shift_p.append(u_tail[:, 7])
            shift_s.append(u_sample.reshape(DB, DL, D)[:, DL - 1])
            w_mix = r_wo[j]
        h = _layer_tail(h, y, w_mix, norm_ffn2[i], ffn2_gate_up[i], ffn2_down[i], pp, ps, i, norm_ple[i],
                        ple_in[i], ple_gate[i], norm_final, final=(i == depth - 1))

    y_prompt, y_sample = h
    return (y_prompt.reshape(B, L, D), y_sample.reshape(DB, DL, D),
            m_prompt[0], m_prompt[1], wkv_p, jnp.stack(shift_p),
            m_sample[0], m_sample[1], wkv_s, jnp.stack(shift_s))
```

```python
import functools
import math

import jax
import jax.numpy as jnp
from jax import lax
from jax.experimental import pallas as pl
from jax.experimental.pallas import tpu as pltpu

F32 = jnp.float32
BF16 = jnp.bfloat16

NORM_EPS = 1e-6
M_NORM_EPS = 1e-5
R_GN_EPS = 64e-5

LANES = 128
M_HEADDIM = 64
M_STATE = 128
M_GROUPS = 8
M_CHUNK = 128
M_CONV_W = 4
R_HEADDIM = 64
R_CHUNK = 64
VMEM_LIMIT_BYTES = 56 * 1024 * 1024


def _cparams(n_axes):
    return pltpu.CompilerParams(dimension_semantics=("arbitrary",) * n_axes,
                                vmem_limit_bytes=VMEM_LIMIT_BYTES)


def _resident(shape):
    zeros = (0,) * len(shape)
    return pl.BlockSpec(shape, lambda *_: zeros, pipeline_mode=pl.Buffered(1))


def _dot(a, b):
    return jnp.dot(a.astype(BF16), b.astype(BF16), preferred_element_type=F32)


def _dot_nt(a, b):
    return lax.dot_general(a.astype(BF16), b.astype(BF16), (((1,), (1,)), ((), ())),
                           preferred_element_type=F32)


def _split3(x):
    hi = x.astype(BF16)
    r1 = x - hi.astype(F32)
    mid = r1.astype(BF16)
    lo = (r1 - mid.astype(F32)).astype(BF16)
    return hi, mid, lo


def _dot_sel_l(sel3, x):
    return jnp.dot(sel3, jnp.concatenate(_split3(x), axis=0), preferred_element_type=F32)


def _dot_sel_r(x, sel3):
    return jnp.dot(jnp.concatenate(_split3(x), axis=1), sel3, preferred_element_type=F32)


def _within_sequence_selectors(rows, seq):
    t = jnp.arange(rows)[:, None]
    s = jnp.arange(rows)[None, :]
    same = (t // seq) == (s // seq)
    incl = jnp.logical_and(same, s <= t)
    return jnp.tile(incl.astype(BF16), (1, 3)), jnp.tile(same.astype(BF16), (1, 3))


def _rms(x, g):
    ms = jnp.mean(x * x, axis=-1, keepdims=True)
    return (x * lax.rsqrt(ms + NORM_EPS)) * g


def _silu(x):
    return x * jax.nn.sigmoid(x)


def _softplus(x):
    return jnp.maximum(x, 0.0) + jnp.log1p(jnp.exp(-jnp.abs(x)))


def _split_rows_specs(tm, width, n_first_blocks, lead=()):
    none = (None,) * len(lead)
    first = pl.BlockSpec(none + (tm, width), lambda i: lead + (jnp.minimum(i, n_first_blocks - 1), 0))
    second = pl.BlockSpec(none + (tm, width), lambda i: lead + (jnp.maximum(i - n_first_blocks, 0), 0))
    return first, second


def _pick_rows(first_ref, second_ref, n_first_blocks):
    return jnp.where(pl.program_id(0) < n_first_blocks, first_ref[...], second_ref[...])


def _ffn_rows(x, g_ref, wgu_ref, wd_ref, ub_ref, fc):
    ub_ref[...] = _rms(x, g_ref[...]).astype(BF16)
    F = wd_ref.shape[0]
    acc = x
    for c in range(F // fc):
        ub = ub_ref[...]
        gate = jnp.dot(ub, wgu_ref[:, c * fc:(c + 1) * fc], preferred_element_type=F32)
        up = jnp.dot(ub, wgu_ref[:, F + c * fc:F + (c + 1) * fc], preferred_element_type=F32)
        act = (_silu(gate) * up).astype(BF16)
        acc = acc + 0.5 * jnp.dot(act, wd_ref[c * fc:(c + 1) * fc, :], preferred_element_type=F32)
    return acc


def _ffn_kernel(*refs, fc, n_first_blocks):
    if n_first_blocks is None:
        h_ref, g_ref, wgu_ref, wd_ref, o_ref, ub_ref = refs
        x = h_ref[...]
    else:
        hp_ref, hs_ref, g_ref, wgu_ref, wd_ref, o_ref, ub_ref = refs
        x = _pick_rows(hp_ref, hs_ref, n_first_blocks)
    o_ref[...] = _ffn_rows(x, g_ref, wgu_ref, wd_ref, ub_ref, fc)


def _ffn(h, gain, w_gate_up, w_down, *, tm=512, fc=256):
    parts = h if isinstance(h, tuple) else (h,)
    T = sum(p.shape[0] for p in parts)
    D = parts[0].shape[1]
    F = w_down.shape[0]
    if len(parts) == 2:
        nfb = parts[0].shape[0] // tm
        h_specs = list(_split_rows_specs(tm, D, nfb))
    else:
        nfb = None
        h_specs = [pl.BlockSpec((tm, D), lambda i: (i, 0))]
    return pl.pallas_call(
        functools.partial(_ffn_kernel, fc=fc, n_first_blocks=nfb),
        out_shape=jax.ShapeDtypeStruct((T, D), F32),
        grid=(T // tm,),
        in_specs=h_specs + [_resident((1, D)), _resident((D, 2 * F)), _resident((F, D))],
        out_specs=pl.BlockSpec((tm, D), lambda i: (i, 0)),
        scratch_shapes=[pltpu.VMEM((tm, D), BF16)],
        compiler_params=_cparams(1),
        name="ffn",
    )(*parts, gain.reshape(1, D), w_gate_up.astype(BF16), w_down.astype(BF16))


def _tail_kernel(*refs, fc, final, n_first_blocks, y_pairs):
    (h_ref, y_ref, wm_ref, g2_ref, wgu_ref, wd_ref, pp_ref, ps_ref, gp_ref, wi_ref, wg_ref) = refs[:11]
    if final:
        gf_ref, op_ref, os_ref, ub_ref = refs[11:]
    else:
        o_ref, ub_ref = refs[11:]
    y = jnp.concatenate([y_ref[p] for p in range(y_pairs)], axis=1) if y_pairs else y_ref[...]
    x = h_ref[...] + jnp.dot(y, wm_ref[...], preferred_element_type=F32)
    x = _ffn_rows(x, g2_ref, wgu_ref, wd_ref, ub_ref, fc)
    gate = jax.nn.sigmoid(_dot(_rms(x, gp_ref[...]), wg_ref[...]))
    x = x + _dot(_pick_rows(pp_ref, ps_ref, n_first_blocks), wi_ref[...]) * gate
    if not final:
        o_ref[...] = x
        return
    x = _rms(x, gf_ref[...])
    i = pl.program_id(0)

    @pl.when(i < n_first_blocks)
    def _():
        op_ref[...] = x

    @pl.when(i >= n_first_blocks)
    def _():
        os_ref[...] = x


def _layer_tail(h, y, w_mix, gain_ffn, w_gate_up, w_down, p_first, p_second, layer, gain_ple, w_in, w_gate,
                gain_final, *, final, tm=512, fc=256):
    T, D = h.shape
    F = w_down.shape[0]
    P = p_first.shape[2]
    nfb = p_first.shape[1] // tm
    row = pl.BlockSpec((tm, D), lambda i: (i, 0))
    y_pairs = y.shape[0] if y.ndim == 3 else 0
    y_spec = (pl.BlockSpec((y_pairs, tm, LANES), lambda i: (0, i, 0)) if y_pairs
              else pl.BlockSpec((tm, y.shape[1]), lambda i: (i, 0)))
    ops = [h, y, w_mix.astype(BF16), gain_ffn.reshape(1, D), w_gate_up.astype(BF16), w_down.astype(BF16),
           p_first, p_second, gain_ple.reshape(1, D), w_in.astype(BF16), w_gate.astype(BF16)]
    specs = [row, y_spec, _resident(w_mix.shape), _resident((1, D)), _resident((D, 2 * F)), _resident((F, D)),
             *_split_rows_specs(tm, P, nfb, lead=(layer,)), _resident((1, D)), _resident((P, D)), _resident((D, D))]
    if final:
        ops.append(gain_final.reshape(1, D))
        specs.append(_resident((1, D)))
        out_shape = (jax.ShapeDtypeStruct((p_first.shape[1], D), F32), jax.ShapeDtypeStruct((p_second.shape[1], D), F32))
        out_specs = _split_rows_specs(tm, D, nfb)
    else:
        out_shape = jax.ShapeDtypeStruct((T, D), F32)
        out_specs = row
    return pl.pallas_call(
        functools.partial(_tail_kernel, fc=fc, final=final, n_first_blocks=nfb, y_pairs=y_pairs),
        out_shape=out_shape,
        grid=(T // tm,),
        in_specs=specs,
        out_specs=out_specs,
        scratch_shapes=[pltpu.VMEM((tm, D), BF16)],
        compiler_params=_cparams(1),
        name="layer_tail",
    )(*ops)


def _m_in_kernel(h_ref, g_ref, wz_ref, wx_ref, wdt_ref, z_ref, xbc_ref, dt_ref):
    u = _rms(h_ref[...], g_ref[...]).astype(BF16)
    z_ref[...] = jnp.dot(u, wz_ref[...], preferred_element_type=F32)
    xbc_ref[...] = jnp.dot(u, wx_ref[...], preferred_element_type=F32)
    dt_ref[...] = jnp.dot(u, wdt_ref[...], preferred_element_type=F32)


def _m_in_proj(h, gain, w_in, d_inner, conv_dim, n_heads, *, tm=256):
    T, D = h.shape
    wz = w_in[:, :d_inner].astype(BF16)
    wx = w_in[:, d_inner:d_inner + conv_dim].astype(BF16)
    wdt = jnp.pad(w_in[:, d_inner + conv_dim:], ((0, 0), (0, LANES - n_heads))).astype(BF16)
    return pl.pallas_call(
        _m_in_kernel,
        out_shape=(jax.ShapeDtypeStruct((T, d_inner), F32), jax.ShapeDtypeStruct((T, conv_dim), F32),
                   jax.ShapeDtypeStruct((T, LANES), F32)),
        grid=(T // tm,),
        in_specs=[pl.BlockSpec((tm, D), lambda i: (i, 0)), _resident((1, D)),
                  _resident((D, d_inner)), _resident((D, conv_dim)), _resident((D, LANES))],
        out_specs=(pl.BlockSpec((tm, d_inner), lambda i: (i, 0)), pl.BlockSpec((tm, conv_dim), lambda i: (i, 0)),
                   pl.BlockSpec((tm, LANES), lambda i: (i, 0))),
        compiler_params=_cparams(1),
        name="m_in_proj",
    )(h, gain.reshape(1, D), wz, wx, wdt)


def _head_expanders(n_heads):
    hh = jnp.arange(LANES)[:, None]
    e64 = (hh == (jnp.arange(n_heads * M_HEADDIM)[None, :] // M_HEADDIM)).astype(BF16)
    e128 = (hh == (jnp.arange(n_heads * LANES)[None, :] // LANES)).astype(BF16)
    return jnp.tile(e64, (3, 1)), jnp.tile(e128, (3, 1))


def _ssd_prompt_kernel(*refs, n_heads, n_aliased):
    (z_ref, xbc_ref, dt_ref, cw_ref, cb_ref, dtb_ref, alog_ref, dsk_ref, nw_ref,
     e64_ref, e128_ref, tri_ref, y_ref, ssm_ref, conv_ref, xpad_ref, s_ref) = refs[n_aliased:]
    c = pl.program_id(1)
    nc = pl.num_programs(1)
    Q = M_CHUNK
    d_inner = n_heads * M_HEADDIM
    gw = d_inner // M_GROUPS
    bc0 = d_inner
    cc0 = d_inner + M_GROUPS * M_STATE

    @pl.when(c == 0)
    def _():
        xpad_ref[:, 0:8, :] = jnp.zeros((xpad_ref.shape[0], 8, LANES), F32)
        s_ref[...] = jnp.zeros(s_ref.shape, F32)

    slabs = []
    for s in range(xpad_ref.shape[0]):
        lanes = slice(s * LANES, (s + 1) * LANES)
        xpad_ref[s, 8:8 + Q, :] = xbc_ref[:, lanes]
        xc = None
        for k in range(M_CONV_W):
            lo = 8 - (M_CONV_W - 1) + k
            term = xpad_ref[s, lo:lo + Q, :] * cw_ref[k:k + 1, lanes]
            xc = term if xc is None else xc + term
        slabs.append(_silu(xc + cb_ref[:, lanes]))
        xpad_ref[s, 0:8, :] = xpad_ref[s, Q:Q + 8, :]
    xs = jnp.concatenate(slabs, axis=1)
    xh = xs[:, :d_inner]

    dt = _softplus(dt_ref[...] + dtb_ref[...])
    dta = dt * (-jnp.exp(alog_ref[...]))
    ri = lax.broadcasted_iota(jnp.int32, (Q, Q), 0)
    ci = lax.broadcasted_iota(jnp.int32, (Q, Q), 1)
    causal = ri >= ci
    acum = _dot_sel_l(tri_ref[...], dta)
    a_full = _dot_sel_r(acum, e64_ref[...])
    dt_full = _dot_sel_r(dt, e64_ref[...])
    a_col = _dot_sel_r(acum, e128_ref[...])
    acum_t = acum.T
    a_last = a_full[Q - 1:Q, :]
    exp_a = jnp.exp(a_full)
    xdt = xh * dt_full
    xw = xdt * jnp.exp(a_last - a_full)
    chunk_decay = jnp.exp(a_last)
    lane = lax.broadcasted_iota(jnp.int32, (Q, LANES), 1)
    first_head = lane < M_HEADDIM
    hpg = n_heads // M_GROUPS

    def group(g):
        bg = xs[:, bc0 + g * M_STATE: bc0 + (g + 1) * M_STATE]
        cg = xs[:, cc0 + g * M_STATE: cc0 + (g + 1) * M_STATE]
        cb = _dot_nt(cg, bg)
        sl = slice(g * gw, (g + 1) * gw)
        s_g = s_ref[:, sl]
        y_g = _dot(cg, s_g) * exp_a[:, sl]
        s_ref[:, sl] = s_g * chunk_decay[:, sl] + _dot(bg.T, xw[:, sl])
        yield
        parts = []
        for q in range(hpg // 2):
            h0 = g * hpg + 2 * q
            xp = xdt[:, h0 * M_HEADDIM: h0 * M_HEADDIM + LANES]
            ys = []
            for hh in (h0, h0 + 1):
                seg = a_col[:, hh * LANES:(hh + 1) * LANES] - acum_t[hh:hh + 1, :]
                m = cb * jnp.exp(jnp.where(causal, seg, -jnp.inf))
                ys.append(_dot(m, xp))
            parts.append(jnp.where(first_head, ys[0], ys[1]))
            yield
        y_g = y_g + jnp.concatenate(parts, axis=1) + dsk_ref[:, sl] * xh[:, sl]
        y_g = y_g * _silu(z_ref[:, sl])
        y_g = y_g * lax.rsqrt(jnp.mean(y_g * y_g, axis=-1, keepdims=True) + M_NORM_EPS)
        y_ref[:, sl] = (y_g * nw_ref[:, sl]).astype(y_ref.dtype)

    _interleave(group(g) for g in range(M_GROUPS))

    @pl.when(c == nc - 1)
    def _():
        ssm_ref[0] = s_ref[...].T.reshape(n_heads, M_HEADDIM, M_STATE)
        conv_ref[0] = xbc_ref[Q - (M_CONV_W - 1):Q, :]


def _alias_prev(prev, first_out):
    prev = tuple(prev or ())
    specs = [pl.BlockSpec(memory_space=pl.ANY)] * len(prev)
    return prev, specs, {k: first_out + k for k in range(len(prev))}


def _ssd_prompt(z, xbc, dt, prm, B, L, n_heads, layer, n_layers, prev):
    T, d_inner = z.shape
    conv_dim = xbc.shape[1]
    Q = M_CHUNK
    nc = L // Q
    row = lambda b, c: (b * nc + c, 0)
    prev, prev_specs, aliases = _alias_prev(prev, first_out=1)
    kern = functools.partial(_ssd_prompt_kernel, n_heads=n_heads, n_aliased=len(prev))
    return pl.pallas_call(
        kern,
        out_shape=(jax.ShapeDtypeStruct((T, d_inner), BF16),
                   jax.ShapeDtypeStruct((n_layers, B, n_heads, M_HEADDIM, M_STATE), F32),
                   jax.ShapeDtypeStruct((n_layers, B, M_CONV_W - 1, conv_dim), F32)),
        grid=(B, nc),
        in_specs=prev_specs + [
            pl.BlockSpec((Q, d_inner), row), pl.BlockSpec((Q, conv_dim), row), pl.BlockSpec((Q, LANES), row),
            _resident((M_CONV_W, conv_dim)), _resident((1, conv_dim)),
            _resident((1, LANES)), _resident((1, LANES)), _resident((1, d_inner)), _resident((1, d_inner)),
            _resident((3 * LANES, d_inner)), _resident((3 * LANES, n_heads * LANES)), _resident((Q, 3 * Q))],
        out_specs=(pl.BlockSpec((Q, d_inner), row),
                   pl.BlockSpec((None, 1, n_heads, M_HEADDIM, M_STATE), lambda b, c: (layer, b, 0, 0, 0)),
                   pl.BlockSpec((None, 1, M_CONV_W - 1, conv_dim), lambda b, c: (layer, b, 0, 0))),
        scratch_shapes=[pltpu.VMEM((conv_dim // LANES, Q + 8, LANES), F32), pltpu.VMEM((M_STATE, d_inner), F32)],
        input_output_aliases=aliases,
        compiler_params=_cparams(2),
        name="ssd_prompt",
    )(*prev, z, xbc, dt, prm["conv_w"], prm["conv_b"], prm["dt_bias"], prm["a_log"], prm["d_full"], prm["norm_w"],
      prm["e64"], prm["e128"], prm["tri"])


def _ssd_sample_kernel(*refs, n_heads, seq, nb, n_aliased):
    (z_ref, xbc_ref, dt_ref, s0_ref, c0_ref, cw_ref, cb_ref, dtb_ref, alog_ref,
     dsk_ref, nw_ref, e64_ref, e128_ref, rowsel_ref, y_ref, ssm_ref, conv_ref, hist_ref) = refs[n_aliased:]
    d_inner = n_heads * M_HEADDIM
    gw = d_inner // M_GROUPS
    bc0 = d_inner
    cc0 = d_inner + M_GROUPS * M_STATE
    hpg = n_heads // M_GROUPS
    rows = nb * seq
    row = lax.broadcasted_iota(jnp.int32, (rows, 1), 0)
    tok = jnp.bitwise_and(row, seq - 1)
    seq_of_row = jnp.right_shift(row, int(math.log2(seq)))
    seq_of_col = jnp.right_shift(lax.broadcasted_iota(jnp.int32, (1, rows), 1), int(math.log2(seq)))
    eye = (lax.broadcasted_iota(jnp.int32, (LANES, LANES), 0)
           == lax.broadcasted_iota(jnp.int32, (LANES, LANES), 1)).astype(BF16)
    top_head = lax.broadcasted_iota(jnp.int32, (LANES, 1), 0) < M_HEADDIM

    @pl.when(pl.program_id(0) == 0)
    def _():
        hist_ref[...] = jnp.zeros(hist_ref.shape, F32)

    x = xbc_ref[...]
    for s in range(1, M_CONV_W):
        for bi in range(nb):
            hist_ref[s - 1, bi * seq:bi * seq + s, :] = c0_ref[bi, M_CONV_W - 1 - s:M_CONV_W - 1, :]
    xc = None
    for k in range(M_CONV_W):
        s = M_CONV_W - 1 - k
        tap = x if s == 0 else jnp.where(tok >= s, pltpu.roll(x, s, 0), hist_ref[s - 1])
        term = tap * cw_ref[k:k + 1, :]
        xc = term if xc is None else xc + term
    xs = _silu(xc + cb_ref[...])
    xh = xs[:, :d_inner]
    for bi in range(nb):
        conv_ref[bi] = xbc_ref[(bi + 1) * seq - (M_CONV_W - 1):(bi + 1) * seq, :]

    dt = _softplus(dt_ref[...] + dtb_ref[...])
    acum = dt * (-jnp.exp(alog_ref[...]))
    s = 1
    while s < seq:
        acum = acum + jnp.where(tok >= s, pltpu.roll(acum, s, 0), 0.0)
        s *= 2
    a_full = _dot_sel_r(acum, e64_ref[...])
    dt_full = _dot_sel_r(dt, e64_ref[...])
    a_col = _dot_sel_r(acum, e128_ref[...])
    exp_a = jnp.exp(a_full)
    xdt = xh * dt_full
    packed = jnp.concatenate([a_full, xdt, xs[:, bc0:cc0]], axis=1)
    from_tok = [_dot_sel_l(rowsel_ref[j], packed) for j in range(seq)]
    aux = xdt * jnp.exp(from_tok[seq - 1][:, :d_inner] - a_full)

    y = dsk_ref[...] * xh
    for j in range(seq):
        a_j = from_tok[j][:, :d_inner]
        wj = jnp.where(tok >= j, jnp.exp(a_full - a_j), 0.0) * from_tok[j][:, d_inner:2 * d_inner]
        parts = []
        for g in range(M_GROUPS):
            cgb = (xs[:, cc0 + g * M_STATE: cc0 + (g + 1) * M_STATE]
                   * from_tok[j][:, 2 * d_inner + g * M_STATE: 2 * d_inner + (g + 1) * M_STATE])
            parts.append(jnp.sum(cgb, axis=-1, keepdims=True) * wj[:, g * gw:(g + 1) * gw])
        y = y + jnp.concatenate(parts, axis=1)

    inter = [None] * (n_heads // 2)

    def head_pair(pr):
        g = (2 * pr) // hpg
        cg = xs[:, cc0 + g * M_STATE: cc0 + (g + 1) * M_STATE]
        bg = xs[:, bc0 + g * M_STATE: bc0 + (g + 1) * M_STATE]
        aux_t = _dot_nt(eye, aux[:, pr * LANES:(pr + 1) * LANES])
        h0 = [s0_ref[bi, 2 * pr:2 * pr + 2].reshape(2 * M_HEADDIM, M_STATE) for bi in range(nb)]
        reads = [_dot_nt(cg, h) for h in h0]
        yield
        acc = None
        for bi in range(nb):
            t = jnp.where(seq_of_row == bi, reads[bi], 0.0)
            acc = t if acc is None else acc + t
        inter[pr] = acc
        upd = [_dot(jnp.where(seq_of_col == bi, aux_t, 0.0), bg) for bi in range(nb)]
        yield
        for bi in range(nb):
            last = (bi + 1) * seq - 1
            dec = jnp.where(top_head, jnp.exp(a_col[last:last + 1, (2 * pr) * LANES:(2 * pr + 1) * LANES]),
                            jnp.exp(a_col[last:last + 1, (2 * pr + 1) * LANES:(2 * pr + 2) * LANES]))
            ssm_ref[bi, 2 * pr:2 * pr + 2] = (h0[bi] * dec + upd[bi]).reshape(2, M_HEADDIM, M_STATE)

    _interleave(head_pair(pr) for pr in range(n_heads // 2))
    y = y + jnp.concatenate(inter, axis=1) * exp_a

    y = y * _silu(z_ref[...])
    outs = []
    for g in range(M_GROUPS):
        yg = y[:, g * gw:(g + 1) * gw]
        outs.append(yg * lax.rsqrt(jnp.mean(yg * yg, axis=-1, keepdims=True) + M_NORM_EPS))
    y_ref[...] = (jnp.concatenate(outs, axis=1) * nw_ref[...]).astype(y_ref.dtype)


def _ssd_sample(y_all, z, xbc, dt, s0, c0, prm, row0, seq, n_heads, layer, prev, *, nb=4):
    T, d_inner = z.shape
    conv_dim = xbc.shape[1]
    DB = s0.shape[1]
    assert seq & (seq - 1) == 0 and seq >= M_CONV_W - 1
    rows = nb * seq
    r = jnp.arange(rows)
    rowsel = jnp.stack([jnp.tile((r[None, :] == ((r // seq) * seq + j)[:, None]).astype(BF16), (1, 3))
                        for j in range(seq)])
    blk0 = row0 // rows
    row = lambda i: (blk0 + i, 0)
    s_spec = pl.BlockSpec((None, nb, n_heads, M_HEADDIM, M_STATE), lambda i: (layer, i, 0, 0, 0))
    c_spec = pl.BlockSpec((None, nb, M_CONV_W - 1, conv_dim), lambda i: (layer, i, 0, 0))
    aliased, alias_specs, aliases = _alias_prev((y_all,) + tuple(prev or ()), first_out=0)
    kern = functools.partial(_ssd_sample_kernel, n_heads=n_heads, seq=seq, nb=nb, n_aliased=len(aliased))
    return pl.pallas_call(
        kern,
        out_shape=(jax.ShapeDtypeStruct((T, d_inner), y_all.dtype),
                   jax.ShapeDtypeStruct(s0.shape, F32), jax.ShapeDtypeStruct(c0.shape, F32)),
        grid=(DB // nb,),
        in_specs=alias_specs + [
            pl.BlockSpec((rows, d_inner), row), pl.BlockSpec((rows, conv_dim), row),
            pl.BlockSpec((rows, LANES), row), s_spec, c_spec,
            _resident((M_CONV_W, conv_dim)), _resident((1, conv_dim)), _resident((1, LANES)),
            _resident((1, LANES)), _resident((1, d_inner)), _resident((1, d_inner)),
            _resident((3 * LANES, d_inner)), _resident((3 * LANES, n_heads * LANES)),
            _resident((seq, rows, 3 * rows))],
        out_specs=(pl.BlockSpec((rows, d_inner), row), s_spec, c_spec),
        scratch_shapes=[pltpu.VMEM((M_CONV_W - 1, rows, conv_dim), F32)],
        input_output_aliases=aliases,
        compiler_params=_cparams(1),
        name="ssd_sample",
    )(*aliased, z, xbc, dt, s0, c0, prm["conv_w"], prm["conv_b"], prm["dt_bias"], prm["a_log"], prm["d_full"],
      prm["norm_w"], prm["e64"], prm["e128"], rowsel)


def _r_proj_kernel(*refs, tm, blocks_per_seq, n_prompt_blocks, sample_seq, has_vres):
    if has_vres:
        (h_ref, g_ref, pshift_ref, srows_ref, mu_ref, wr_ref, wk_ref, wv_ref, w1_ref, w2_ref, a1_ref, a2_ref,
         g1_ref, g2_ref, w0_ref, a0_ref, kk_ref, ka_ref, vf_ref, v0_ref, v1_ref, v2_ref,
         r_o, k_o, v_o, kkr_o, a_o, w_o, g_o, up_o, us_o, sc_ref, xx_ref) = refs
    else:
        (h_ref, g_ref, pshift_ref, srows_ref, mu_ref, wr_ref, wk_ref, wv_ref, w1_ref, w2_ref, a1_ref, a2_ref,
         g1_ref, g2_ref, w0_ref, a0_ref, kk_ref, ka_ref,
         r_o, k_o, v_o, kkr_o, a_o, w_o, g_o, up_o, us_o, sc_ref, xx_ref) = refs
    i = pl.program_id(0)
    u = _rms(h_ref[...], g_ref[...])
    sc_ref[8:8 + tm, :] = u

    @pl.when(jnp.logical_and(i % blocks_per_seq == blocks_per_seq - 1, i < n_prompt_blocks))
    def _():
        up_o[0] = u[tm - 8:tm, :]

    @pl.when(i >= n_prompt_blocks)
    def _():
        us_o[...] = u

    @pl.when(jnp.logical_and(i % blocks_per_seq == 0, i < n_prompt_blocks))
    def _():
        sc_ref[7:8, :] = pshift_ref[pl.ds(i // blocks_per_seq, 1), :]

    prev = sc_ref[7:7 + tm, :]
    row = lax.broadcasted_iota(jnp.int32, (tm, 1), 0)
    use_state = jnp.logical_and(i >= n_prompt_blocks, jnp.bitwise_and(row, sample_seq - 1) == 0)
    prev = jnp.where(use_state, srows_ref[...], prev)
    sc_ref[7:8, :] = sc_ref[7 + tm:8 + tm, :]

    xx_ref[...] = prev - u
    npair = r_o.shape[0]

    def rows(r0, n):
        u_h = sc_ref[8 + r0:8 + r0 + n, :]
        xx_h = xx_ref[r0:r0 + n, :]

        def mix(m):
            return u_h + xx_h * mu_ref[m:m + 1, :]

        xv = mix(3)
        r = _dot(mix(0), wr_ref[...])
        k = _dot(mix(2), wk_ref[...])
        v = _dot(xv, wv_ref[...])
        yield
        lw = _dot(mix(1), w1_ref[...])
        la = _dot(mix(4), a1_ref[...])
        lg = _dot(mix(5), g1_ref[...])
        lv = _dot(xv, v1_ref[...]) if has_vres else None
        yield
        w = -_softplus(-(w0_ref[...] + _dot(jnp.tanh(lw), w2_ref[...]))) - 0.5
        a = jax.nn.sigmoid(a0_ref[...] + _dot(la, a2_ref[...]))
        g = _dot(jax.nn.sigmoid(lg), g2_ref[...])
        if has_vres:
            vf = jnp.concatenate([vf_ref[p, r0:r0 + n, :] for p in range(npair)], axis=1)
            v = v + (vf - v) * jax.nn.sigmoid(v0_ref[...] + _dot(lv, v2_ref[...]))
        yield
        kkr = k * kk_ref[...]
        k = k * (1.0 + (a - 1.0) * ka_ref[...])
        for p in range(npair):
            sl = slice(p * LANES, (p + 1) * LANES)
            r_o[p, r0:r0 + n, :] = r[:, sl]
            k_o[p, r0:r0 + n, :] = k[:, sl]
            v_o[p, r0:r0 + n, :] = v[:, sl]
            kkr_o[p, r0:r0 + n, :] = kkr[:, sl]
            a_o[p, r0:r0 + n, :] = a[:, sl]
            w_o[p, r0:r0 + n, :] = w[:, sl]
            g_o[p, r0:r0 + n, :] = g[:, sl]

    n_groups = 2
    _interleave((rows(q * (tm // n_groups), tm // n_groups) for q in range(n_groups)))


def _pad_cols(w, n):
    return jnp.pad(w, ((0, 0), (0, n - w.shape[1])))


def _pad_rows(w, n):
    return jnp.pad(w, ((0, n - w.shape[0]), (0, 0)))


def _round_up(n, m):
    return -(-n // m) * m


def _r_proj(h, gain, pshift, srows, rp, v_first, L, n_prompt_rows, sample_seq, *, tm=256):
    T, D = h.shape
    npair = D // LANES
    has_vres = v_first is not None
    row2 = lambda i: (i, 0)
    npb = n_prompt_rows // tm
    lw = _round_up(rp["w1"].shape[1], LANES)
    la = _round_up(rp["a1"].shape[1], LANES)
    lg = _round_up(rp["g1"].shape[1], LANES)
    ops = [h, gain.reshape(1, D), pshift, srows, rp["mu"],
           rp["wr"].astype(BF16), rp["wk"].astype(BF16), rp["wv"].astype(BF16),
           _pad_cols(rp["w1"], lw).astype(BF16), _pad_rows(rp["w2"], lw).astype(BF16),
           _pad_cols(rp["a1"], la).astype(BF16), _pad_rows(rp["a2"], la).astype(BF16),
           _pad_cols(rp["g1"], lg).astype(BF16), _pad_rows(rp["g2"], lg).astype(BF16),
           rp["w0"].reshape(1, D), rp["a0"].reshape(1, D), rp["k_k"].reshape(1, D), rp["k_a"].reshape(1, D)]
    specs = [pl.BlockSpec((tm, D), row2), _resident((1, D)), _resident(pshift.shape),
             pl.BlockSpec((tm, D), lambda i: (jnp.maximum(i - npb, 0), 0)), _resident((6, D)),
             _resident((D, D)), _resident((D, D)), _resident((D, D)),
             _resident((D, lw)), _resident((lw, D)), _resident((D, la)), _resident((la, D)),
             _resident((D, lg)), _resident((lg, D)),
             _resident((1, D)), _resident((1, D)), _resident((1, D)), _resident((1, D))]
    pair_spec = pl.BlockSpec((npair, tm, LANES), lambda i: (0, i, 0))
    if has_vres:
        lv = _round_up(rp["v1"].shape[1], LANES)
        ops += [v_first, rp["v0"].reshape(1, D), _pad_cols(rp["v1"], lv).astype(BF16),
                _pad_rows(rp["v2"], lv).astype(BF16)]
        specs += [pair_spec, _resident((1, D)), _resident((D, lv)), _resident((lv, D))]
    pair_shape = jax.ShapeDtypeStruct((npair, T, LANES), F32)
    bps = L // tm
    n_seq = n_prompt_rows // L
    kern = functools.partial(_r_proj_kernel, tm=tm, blocks_per_seq=bps, n_prompt_blocks=npb,
                             sample_seq=sample_seq, has_vres=has_vres)
    return pl.pallas_call(
        kern,
        out_shape=(pair_shape,) * 7 + (jax.ShapeDtypeStruct((n_seq, 8, D), F32),
                                       jax.ShapeDtypeStruct((T - n_prompt_rows, D), F32)),
        grid=(T // tm,),
        in_specs=specs,
        out_specs=(pair_spec,) * 7 + (
            pl.BlockSpec((1, 8, D), lambda i: (jnp.minimum(i // bps, n_seq - 1), 0, 0)),
            pl.BlockSpec((tm, D), lambda i: (jnp.maximum(i - npb, 0), 0))),
        scratch_shapes=[pltpu.VMEM((tm + 8, D), F32), pltpu.VMEM((tm, D), F32)],
        compiler_params=_cparams(1),
        name="r_proj",
    )(*ops)


def _wkv_pair(r, k, v, kkr, a, w, g, rk, gnw, gnb, incl3, same3, states, lseq):
    C = r.shape[0]
    ns = C // lseq
    lane = lax.broadcasted_iota(jnp.int32, (1, LANES), 1)
    m0 = lane < R_HEADDIM

    def headsum(x):
        s0 = jnp.sum(jnp.where(m0, x, 0.0), axis=-1, keepdims=True)
        s1 = jnp.sum(jnp.where(m0, 0.0, x), axis=-1, keepdims=True)
        return jnp.where(m0, s0, s1)

    kk = kkr * lax.rsqrt(jnp.maximum(headsum(kkr * kkr), 1e-24))
    alpha = -kk
    beta = kk * a
    logd = -jnp.exp(w)

    ti = lax.broadcasted_iota(jnp.int32, (C, C), 0)
    si = lax.broadcasted_iota(jnp.int32, (C, C), 1)
    sh = int(math.log2(lseq))
    same = jnp.right_shift(ti, sh) == jnp.right_shift(si, sh)
    incl = jnp.logical_and(same, si <= ti)
    strict = jnp.logical_and(same, si < ti)
    lg = _dot_sel_l(incl3, logd)
    ltot = _dot_sel_l(same3, logd)
    yield
    g_in = jnp.exp(lg)
    ginv = jnp.exp(-lg)
    gend = jnp.exp(ltot - lg)
    w0 = alpha * jnp.exp(lg - logd)
    rg = r * g_in
    bi = beta * ginv
    ki = k * ginv
    wr = jnp.concatenate([w0, rg], axis=0)
    rsel = jnp.right_shift(jnp.bitwise_and(lax.broadcasted_iota(jnp.int32, (2 * C, 1), 0), C - 1), sh)

    xy0 = None
    for s in range(ns):
        t = _dot_nt(wr, states[s])
        if ns > 1:
            t = jnp.where(rsel == s, t, 0.0)
        xy0 = t if xy0 is None else xy0 + t
    x0 = xy0[:C]
    y0 = xy0[C:]
    yield

    eye = (ti == si).astype(F32)
    heads = (m0, jnp.logical_not(m0))
    wrm = [jnp.where(mh, wr, 0.0) for mh in heads]
    gb = [_dot_nt(x, bi) for x in wrm]
    gk = [_dot_nt(x, ki) for x in wrm]
    yield
    a_ab = [jnp.where(strict, x[:C], 0.0) for x in gb]
    a_ak = [jnp.where(strict, x[:C], 0.0) for x in gk]
    b_rb = [jnp.where(incl, x[C:], 0.0) for x in gb]
    b_rk = [jnp.where(incl, x[C:], 0.0) for x in gk]
    tinv = [eye + x for x in a_ab]
    pw = a_ab
    for _ in range(int(math.log2(lseq)) - 1):
        pw = [_dot(x, x) for x in pw]
        yield
        tinv = [t + _dot(t, x) for t, x in zip(tinv, pw)]
    xh = [x0 + _dot(x, v) for x in a_ak]
    yv = [_dot(x, v) for x in b_rk]
    yield
    uh = [_dot(t, x) for t, x in zip(tinv, xh)]
    yield
    yh = [y0 + _dot(b, x) + c for b, x, c in zip(b_rb, uh, yv)]
    u = jnp.where(m0, uh[0], uh[1])
    y = jnp.where(m0, yh[0], yh[1])
    yield

    uvt = jnp.concatenate([u, v], axis=0).T
    bkd = jnp.concatenate([beta * gend, k * gend], axis=0)
    bi_r = lax.broadcasted_iota(jnp.int32, (LANES, LANES), 0) >= R_HEADDIM
    bi_c = lax.broadcasted_iota(jnp.int32, (LANES, LANES), 1) >= R_HEADDIM
    same_head = bi_r == bi_c
    new_states = []
    for s in range(ns):
        rhs = bkd if ns == 1 else jnp.where(rsel == s, bkd, 0.0)
        ds = jnp.where(same_head, _dot(uvt, rhs), 0.0)
        new_states.append(states[s] * jnp.exp(ltot[s * lseq:s * lseq + 1, :]) + ds)
    yield

    mean = headsum(y) * (1.0 / R_HEADDIM)
    d = y - mean
    var = headsum(d * d) * (1.0 / R_HEADDIM)
    yn = d * lax.rsqrt(var + R_GN_EPS) * gnw + gnb
    out = (yn + headsum(r * k * rk) * v) * g
    return out, new_states


def _wkv_pair_chunk(r, k, v, kkr, a, w, g, rk, gnw, gnb, incl3, state_ref, y_ref):
    C = r.shape[0]
    H = 2 * C
    lane = lax.broadcasted_iota(jnp.int32, (1, LANES), 1)
    m0 = lane < R_HEADDIM

    def headsum(x):
        s0 = jnp.sum(jnp.where(m0, x, 0.0), axis=-1, keepdims=True)
        s1 = jnp.sum(jnp.where(m0, 0.0, x), axis=-1, keepdims=True)
        return jnp.where(m0, s0, s1)

    def stack(x):
        return jnp.concatenate([jnp.where(m0, x, 0.0), jnp.where(m0, 0.0, x)], axis=0)

    kk = kkr * lax.rsqrt(jnp.maximum(headsum(kkr * kkr), 1e-24))
    beta = kk * a
    logd = -jnp.exp(w)
    lg = _dot_sel_l(incl3, logd)
    yield
    ltot = lg[C - 1:C, :]
    ginv = jnp.exp(-lg)
    gend = jnp.exp(ltot - lg)
    w0s = stack(-kk * jnp.exp(lg - logd))
    rgs = stack(r * jnp.exp(lg))
    vs = stack(v)
    bds = stack(beta * gend)
    kds = stack(k * gend)
    gram = _dot_nt(jnp.concatenate([w0s, rgs], axis=0),
                   jnp.concatenate([stack(beta * ginv), stack(k * ginv)], axis=0))
    yield
    tr = jnp.bitwise_and(lax.broadcasted_iota(jnp.int32, (H, H), 0), C - 1)
    sc = jnp.bitwise_and(lax.broadcasted_iota(jnp.int32, (H, H), 1), C - 1)
    strict = sc < tr
    incl = sc <= tr
    a_ab = jnp.where(strict, gram[:H, :H], 0.0)
    a_ak = jnp.where(strict, gram[:H, H:], 0.0)
    b_rb = jnp.where(incl, gram[H:, :H], 0.0)
    b_rk = jnp.where(incl, gram[H:, H:], 0.0)
    eye = (lax.broadcasted_iota(jnp.int32, (H, H), 0) == lax.broadcasted_iota(jnp.int32, (H, H), 1)).astype(F32)
    tinv = eye + a_ab
    pw = _dot(a_ab, a_ab)
    av = _dot(a_ak, vs)
    bv = _dot(b_rk, vs)
    yield
    for _ in range(int(math.log2(C)) - 2):
        tinv = tinv + _dot(tinv, pw)
        pw = _dot(pw, pw)
        yield
    tinv = tinv + _dot(tinv, pw)
    yield
    tw = _dot(tinv, jnp.concatenate([w0s, av], axis=1))
    yield
    br = _dot(b_rb, tw)
    m_t = _dot(tw[:, :LANES].T, bds)
    z = _dot(jnp.concatenate([tw[:, LANES:], vs], axis=0).T, jnp.concatenate([bds, kds], axis=0))
    yield
    r_t = rgs + br[:, :LANES]
    y_t = br[:, LANES:] + bv
    state = state_ref[...]
    ys = _dot_nt(r_t, state) + y_t
    state_ref[...] = state * jnp.exp(ltot) + _dot(state, m_t) + z
    yield
    y = ys[:C] + ys[C:]
    mean = headsum(y) * (1.0 / R_HEADDIM)
    d = y - mean
    var = headsum(d * d) * (1.0 / R_HEADDIM)
    yn = d * lax.rsqrt(var + R_GN_EPS) * gnw + gnb
    y_ref[...] = ((yn + headsum(r * k * rk) * v) * g).astype(y_ref.dtype)


def _interleave(generators):
    live = list(generators)
    while live:
        nxt = []
        for gen in live:
            try:
                next(gen)
                nxt.append(gen)
            except StopIteration:
                pass
        live = nxt


def _wkv_prompt_kernel(*refs, chunks_per_seq, n_aliased):
    (r_ref, k_ref, v_ref, kkr_ref, a_ref, w_ref, g_ref, rk_ref, gnw_ref, gnb_ref, incl_ref,
     y_ref, s_out_ref, s_ref) = refs[n_aliased:]
    c = pl.program_id(0) % chunks_per_seq
    npair = r_ref.shape[0]

    @pl.when(c == 0)
    def _():
        s_ref[...] = jnp.zeros(s_ref.shape, F32)

    _interleave((_wkv_pair_chunk(r_ref[p], k_ref[p], v_ref[p], kkr_ref[p], a_ref[p], w_ref[p], g_ref[p],
                                 rk_ref[p], gnw_ref[p], gnb_ref[p], incl_ref[...], s_ref.at[p], y_ref.at[p])
                 for p in range(npair)))

    @pl.when(c == chunks_per_seq - 1)
    def _():
        for p in range(npair):
            s = s_ref[p]
            s_out_ref[0, 2 * p:2 * p + 2] = (s[:, :R_HEADDIM] + s[:, R_HEADDIM:]).reshape(2, R_HEADDIM, R_HEADDIM)


def _wkv_prompt(streams, rk, gnw, gnb, B, L, layer, n_layers, prev):
    npair, T, _ = streams[0].shape
    C = R_CHUNK
    cps = L // C
    blk = pl.BlockSpec((npair, C, LANES), lambda i: (0, i, 0))
    vec = _resident((npair, 1, LANES))
    prev, prev_specs, aliases = _alias_prev(prev, first_out=1)
    return pl.pallas_call(
        functools.partial(_wkv_prompt_kernel, chunks_per_seq=cps, n_aliased=len(prev)),
        out_shape=(jax.ShapeDtypeStruct((npair, T, LANES), BF16),
                   jax.ShapeDtypeStruct((n_layers, B, 2 * npair, R_HEADDIM, R_HEADDIM), F32)),
        grid=(B * cps,),
        in_specs=prev_specs + [blk] * 7 + [vec] * 3 + [_resident((C, 3 * C))],
        out_specs=(blk, pl.BlockSpec((None, 1, 2 * npair, R_HEADDIM, R_HEADDIM),
                                     lambda i: (layer, i // cps, 0, 0, 0))),
        scratch_shapes=[pltpu.VMEM((npair, LANES, LANES), F32)],
        input_output_aliases=aliases,
        compiler_params=_cparams(1),
        name="wkv_prompt",
    )(*prev, *streams, rk, gnw, gnb, _within_sequence_selectors(C, C)[0])


def _wkv_sample_kernel(*refs, seq, n_aliased):
    (r_ref, k_ref, v_ref, kkr_ref, a_ref, w_ref, g_ref, rk_ref, gnw_ref, gnb_ref, incl_ref, same_ref,
     s0_ref, y_ref, s_out_ref) = refs[n_aliased:]
    npair = r_ref.shape[0]
    ns = R_CHUNK // seq
    lane = lax.broadcasted_iota(jnp.int32, (LANES, LANES), 1) >= R_HEADDIM
    rowh = lax.broadcasted_iota(jnp.int32, (LANES, LANES), 0) >= R_HEADDIM
    same_head = lane == rowh

    def pair(p):
        states = []
        for s in range(ns):
            s2 = s0_ref[s, 2 * p:2 * p + 2].reshape(LANES, R_HEADDIM)
            states.append(jnp.where(same_head, jnp.concatenate([s2, s2], axis=1), 0.0))
        out, new = yield from _wkv_pair(r_ref[p], k_ref[p], v_ref[p], kkr_ref[p], a_ref[p], w_ref[p], g_ref[p],
                                        rk_ref[p], gnw_ref[p], gnb_ref[p], incl_ref[...], same_ref[...],
                                        states, seq)
        y_ref[p] = out.astype(y_ref.dtype)
        for s in range(ns):
            s_out_ref[s, 2 * p:2 * p + 2] = (new[s][:, :R_HEADDIM] + new[s][:, R_HEADDIM:]).reshape(
                2, R_HEADDIM, R_HEADDIM)

    _interleave(pair(p) for p in range(npair))


def _wkv_sample(y_all, streams, rk, gnw, gnb, s0, row0, seq, layer, prev):
    npair, T, _ = streams[0].shape
    C = R_CHUNK
    ns = C // seq
    DB = s0.shape[1]
    blk0 = row0 // C
    blk = pl.BlockSpec((npair, C, LANES), lambda i: (0, blk0 + i, 0))
    vec = _resident((npair, 1, LANES))
    sblk = pl.BlockSpec((None, ns, 2 * npair, R_HEADDIM, R_HEADDIM), lambda i: (layer, i, 0, 0, 0))
    aliased, alias_specs, aliases = _alias_prev((y_all,) + tuple(prev or ()), first_out=0)
    return pl.pallas_call(
        functools.partial(_wkv_sample_kernel, seq=seq, n_aliased=len(aliased)),
        out_shape=(jax.ShapeDtypeStruct((npair, T, LANES), BF16), jax.ShapeDtypeStruct(s0.shape, F32)),
        grid=(DB // ns,),
        in_specs=alias_specs + [blk] * 7 + [vec] * 3 + [_resident((C, 3 * C))] * 2 + [sblk],
        out_specs=(blk, sblk),
        input_output_aliases=aliases,
        compiler_params=_cparams(1),
        name="wkv_sample",
    )(*aliased, *streams, rk, gnw, gnb, *_within_sequence_selectors(C, seq), s0)


def kernel(x_prompt, x_sample, p_prompt, p_sample, state_ssm, state_conv, state_wkv, state_shift, norm_ffn1, ffn1_gate_up, ffn1_down, norm_mix, norm_ffn2, ffn2_gate_up, ffn2_down, norm_ple, ple_in, ple_gate, norm_final, m_in_proj, m_conv_w, m_conv_b, m_dt_bias, m_A_log, m_D, m_norm, m_out_proj, r_mu, r_wr, r_wk, r_wv, r_wo, r_w0, r_w1, r_w2, r_a0, r_a1, r_a2, r_g1, r_g2, r_k_k, r_k_a, r_r_k, r_gn_w, r_gn_b, r_v0, r_v1, r_v2):
    B, L, D = x_prompt.shape
    DB, DL, _ = x_sample.shape
    depth = norm_ffn1.shape[0]
    TP, TS = B * L, DB * DL
    n_heads = m_A_log.shape[1]
    d_inner = n_heads * M_HEADDIM
    conv_dim = m_conv_w.shape[2]
    npair = D // LANES

    h = (x_prompt.reshape(TP, D), x_sample.reshape(TS, D))
    pp = p_prompt.reshape(depth, TP, -1)
    ps = p_sample.reshape(depth, TS, -1)
    e64, e128 = _head_expanders(n_heads)
    n_mamba = state_ssm.shape[0]
    n_rwkv = state_wkv.shape[0]

    tri = _within_sequence_selectors(M_CHUNK, M_CHUNK)[0]

    m_prompt = m_sample = wkv_p = wkv_s = None
    shift_p, shift_s = [], []
    v_first = None
    for i in range(depth):
        j = i // 2
        h = _ffn(h, norm_ffn1[i], ffn1_gate_up[i], ffn1_down[i])
        if i % 2 == 0:
            prm = dict(conv_w=m_conv_w[j], conv_b=m_conv_b[j].reshape(1, conv_dim),
                       dt_bias=jnp.pad(m_dt_bias[j], (0, LANES - n_heads)).reshape(1, LANES),
                       a_log=jnp.pad(m_A_log[j], (0, LANES - n_heads)).reshape(1, LANES),
                       d_full=jnp.repeat(m_D[j], M_HEADDIM).reshape(1, d_inner),
                       norm_w=m_norm[j].reshape(1, d_inner), e64=e64, e128=e128, tri=tri)
            z, xbc, dt = _m_in_proj(h, norm_mix[i], m_in_proj[j], d_inner, conv_dim, n_heads)
            y, *m_prompt = _ssd_prompt(z, xbc, dt, prm, B, L, n_heads, j, n_mamba, m_prompt)
            y, *m_sample = _ssd_sample(y, z, xbc, dt, state_ssm, state_conv, prm, TP, DL, n_heads, j, m_sample)
            w_mix = m_out_proj[j]
        else:
            rp = dict(mu=r_mu[j], wr=r_wr[j], wk=r_wk[j], wv=r_wv[j], w0=r_w0[j], w1=r_w1[j], w2=r_w2[j],
                      a0=r_a0[j], a1=r_a1[j], a2=r_a2[j], g1=r_g1[j], g2=r_g2[j], k_k=r_k_k[j], k_a=r_k_a[j])
            if j > 0:
                rp.update(v0=r_v0[j - 1], v1=r_v1[j - 1], v2=r_v2[j - 1])
            pshift = jnp.zeros((B, D), F32)
            srows = jnp.zeros((DB, DL, D), F32).at[:, 0].set(state_shift[j]).reshape(TS, D)
            outs = _r_proj(h, norm_mix[i], pshift, srows, rp, v_first if j > 0 else None, L, TP, DL)
            streams, u_tail, u_sample = outs[:7], outs[7], outs[8]
            if j == 0:
                v_first = streams[2]
            rk = r_r_k[j].reshape(npair, 1, LANES)
            gnw = r_gn_w[j].reshape(npair, 1, LANES)
            gnb = r_gn_b[j].reshape(npair, 1, LANES)
            y, wkv_p = _wkv_prompt(streams, rk, gnw, gnb, B, L, j, n_rwkv, None if wkv_p is None else (wkv_p,))
            y, wkv_s = _wkv_sample(y, streams, rk, gnw, gnb, state_wkv, TP, DL, j, None if wkv_s is None else (wkv_s,))
            shift_p.append(u_tail[:, 7])
            shift_s.append(u_sample.reshape(DB, DL, D)[:, DL - 1])
            w_mix = r_wo[j]
        h = _layer_tail(h, y, w_mix, norm_ffn2[i], ffn2_gate_up[i], ffn2_down[i], pp, ps, i, norm_ple[i],
                        ple_in[i], ple_gate[i], norm_final, final=(i == depth - 1))

    y_prompt, y_sample = h
    return (y_prompt.reshape(B, L, D), y_sample.reshape(DB, DL, D),
            m_prompt[0], m_prompt[1], wkv_p, jnp.stack(shift_p),
            m_sample[0], m_sample[1], wkv_s, jnp.stack(shift_s))
```

```python
import functools
import math

import jax
import jax.numpy as jnp
from jax import lax
from jax.experimental import pallas as pl
from jax.experimental.pallas import tpu as pltpu

F32 = jnp.float32
BF16 = jnp.bfloat16

NORM_EPS = 1e-6
M_NORM_EPS = 1e-5
R_GN_EPS = 64e-5

LANES = 128
M_HEADDIM = 64
M_STATE = 128
M_GROUPS = 8
M_CHUNK = 128
M_CONV_W = 4
R_HEADDIM = 64
R_CHUNK = 64
R_PROMPT_CHUNKS_PER_STEP = 4
VMEM_LIMIT_BYTES = 56 * 1024 * 1024


def _cparams(n_axes):
    return pltpu.CompilerParams(dimension_semantics=("arbitrary",) * n_axes,
                                vmem_limit_bytes=VMEM_LIMIT_BYTES)


def _resident(shape):
    zeros = (0,) * len(shape)
    return pl.BlockSpec(shape, lambda *_: zeros, pipeline_mode=pl.Buffered(1))


def _dot(a, b):
    return jnp.dot(a.astype(BF16), b.astype(BF16), preferred_element_type=F32)


def _dot_nt(a, b):
    return lax.dot_general(a.astype(BF16), b.astype(BF16), (((1,), (1,)), ((), ())),
                           preferred_element_type=F32)


def _split3(x):
    hi = x.astype(BF16)
    r1 = x - hi.astype(F32)
    mid = r1.astype(BF16)
    lo = (r1 - mid.astype(F32)).astype(BF16)
    return hi, mid, lo


def _dot_sel_l(sel3, x):
    return jnp.dot(sel3, jnp.concatenate(_split3(x), axis=0), preferred_element_type=F32)


def _dot_sel_r(x, sel3):
    return jnp.dot(jnp.concatenate(_split3(x), axis=1), sel3, preferred_element_type=F32)


def _within_sequence_selectors(rows, seq):
    t = jnp.arange(rows)[:, None]
    s = jnp.arange(rows)[None, :]
    same = (t // seq) == (s // seq)
    incl = jnp.logical_and(same, s <= t)
    return jnp.tile(incl.astype(BF16), (1, 3)), jnp.tile(same.astype(BF16), (1, 3))


def _rms(x, g):
    ms = jnp.mean(x * x, axis=-1, keepdims=True)
    return (x * lax.rsqrt(ms + NORM_EPS)) * g


def _silu(x):
    return x * jax.nn.sigmoid(x)


def _softplus(x):
    return jnp.maximum(x, 0.0) + jnp.log1p(jnp.exp(-jnp.abs(x)))


def _split_rows_specs(tm, width, n_first_blocks, lead=()):
    none = (None,) * len(lead)
    first = pl.BlockSpec(none + (tm, width), lambda i: lead + (jnp.minimum(i, n_first_blocks - 1), 0))
    second = pl.BlockSpec(none + (tm, width), lambda i: lead + (jnp.maximum(i - n_first_blocks, 0), 0))
    return first, second


def _pick_rows(first_ref, second_ref, n_first_blocks):
    return jnp.where(pl.program_id(0) < n_first_blocks, first_ref[...], second_ref[...])


def _ffn_rows(x, g_ref, wgu_ref, wd_ref, ub_ref, fc):
    ub_ref[...] = _rms(x, g_ref[...]).astype(BF16)
    F = wd_ref.shape[0]
    acc = x
    for c in range(F // fc):
        ub = ub_ref[...]
        gate = jnp.dot(ub, wgu_ref[:, c * fc:(c + 1) * fc], preferred_element_type=F32)
        up = jnp.dot(ub, wgu_ref[:, F + c * fc:F + (c + 1) * fc], preferred_element_type=F32)
        act = (_silu(gate) * up).astype(BF16)
        acc = acc + 0.5 * jnp.dot(act, wd_ref[c * fc:(c + 1) * fc, :], preferred_element_type=F32)
    return acc


def _ffn_kernel(*refs, fc, n_first_blocks):
    if n_first_blocks is None:
        h_ref, g_ref, wgu_ref, wd_ref, o_ref, ub_ref = refs
        x = h_ref[...]
    else:
        hp_ref, hs_ref, g_ref, wgu_ref, wd_ref, o_ref, ub_ref = refs
        x = _pick_rows(hp_ref, hs_ref, n_first_blocks)
    o_ref[...] = _ffn_rows(x, g_ref, wgu_ref, wd_ref, ub_ref, fc)


def _ffn(h, gain, w_gate_up, w_down, *, tm=512, fc=256):
    parts = h if isinstance(h, tuple) else (h,)
    T = sum(p.shape[0] for p in parts)
    D = parts[0].shape[1]
    F = w_down.shape[0]
    if len(parts) == 2:
        nfb = parts[0].shape[0] // tm
        h_specs = list(_split_rows_specs(tm, D, nfb))
    else:
        nfb = None
        h_specs = [pl.BlockSpec((tm, D), lambda i: (i, 0))]
    return pl.pallas_call(
        functools.partial(_ffn_kernel, fc=fc, n_first_blocks=nfb),
        out_shape=jax.ShapeDtypeStruct((T, D), F32),
        grid=(T // tm,),
        in_specs=h_specs + [_resident((1, D)), _resident((D, 2 * F)), _resident((F, D))],
        out_specs=pl.BlockSpec((tm, D), lambda i: (i, 0)),
        scratch_shapes=[pltpu.VMEM((tm, D), BF16)],
        compiler_params=_cparams(1),
        name="ffn",
    )(*parts, gain.reshape(1, D), w_gate_up.astype(BF16), w_down.astype(BF16))


def _tail_kernel(*refs, fc, final, n_first_blocks, y_pairs):
    (h_ref, y_ref, wm_ref, g2_ref, wgu_ref, wd_ref, pp_ref, ps_ref, gp_ref, wi_ref, wg_ref) = refs[:11]
    if final:
        gf_ref, op_ref, os_ref, ub_ref = refs[11:]
    else:
        o_ref, ub_ref = refs[11:]
    y = jnp.concatenate([y_ref[p] for p in range(y_pairs)], axis=1) if y_pairs else y_ref[...]
    x = h_ref[...] + jnp.dot(y, wm_ref[...], preferred_element_type=F32)
    x = _ffn_rows(x, g2_ref, wgu_ref, wd_ref, ub_ref, fc)
    gate = jax.nn.sigmoid(_dot(_rms(x, gp_ref[...]), wg_ref[...]))
    x = x + _dot(_pick_rows(pp_ref, ps_ref, n_first_blocks), wi_ref[...]) * gate
    if not final:
        o_ref[...] = x
        return
    x = _rms(x, gf_ref[...])
    i = pl.program_id(0)

    @pl.when(i < n_first_blocks)
    def _():
        op_ref[...] = x

    @pl.when(i >= n_first_blocks)
    def _():
        os_ref[...] = x


def _layer_tail(h, y, w_mix, gain_ffn, w_gate_up, w_down, p_first, p_second, layer, gain_ple, w_in, w_gate,
                gain_final, *, final, tm=512, fc=256):
    T, D = h.shape
    F = w_down.shape[0]
    P = p_first.shape[2]
    nfb = p_first.shape[1] // tm
    row = pl.BlockSpec((tm, D), lambda i: (i, 0))
    y_pairs = y.shape[0] if y.ndim == 3 else 0
    y_spec = (pl.BlockSpec((y_pairs, tm, LANES), lambda i: (0, i, 0)) if y_pairs
              else pl.BlockSpec((tm, y.shape[1]), lambda i: (i, 0)))
    ops = [h, y, w_mix.astype(BF16), gain_ffn.reshape(1, D), w_gate_up.astype(BF16), w_down.astype(BF16),
           p_first, p_second, gain_ple.reshape(1, D), w_in.astype(BF16), w_gate.astype(BF16)]
    specs = [row, y_spec, _resident(w_mix.shape), _resident((1, D)), _resident((D, 2 * F)), _resident((F, D)),
             *_split_rows_specs(tm, P, nfb, lead=(layer,)), _resident((1, D)), _resident((P, D)), _resident((D, D))]
    if final:
        ops.append(gain_final.reshape(1, D))
        specs.append(_resident((1, D)))
        out_shape = (jax.ShapeDtypeStruct((p_first.shape[1], D), F32), jax.ShapeDtypeStruct((p_second.shape[1], D), F32))
        out_specs = _split_rows_specs(tm, D, nfb)
    else:
        out_shape = jax.ShapeDtypeStruct((T, D), F32)
        out_specs = row
    return pl.pallas_call(
        functools.partial(_tail_kernel, fc=fc, final=final, n_first_blocks=nfb, y_pairs=y_pairs),
        out_shape=out_shape,
        grid=(T // tm,),
        in_specs=specs,
        out_specs=out_specs,
        scratch_shapes=[pltpu.VMEM((tm, D), BF16)],
        compiler_params=_cparams(1),
        name="layer_tail",
    )(*ops)


def _m_in_kernel(h_ref, g_ref, wz_ref, wx_ref, wdt_ref, z_ref, xbc_ref, dt_ref):
    u = _rms(h_ref[...], g_ref[...]).astype(BF16)
    z_ref[...] = jnp.dot(u, wz_ref[...], preferred_element_type=F32)
    xbc_ref[...] = jnp.dot(u, wx_ref[...], preferred_element_type=F32)
    dt_ref[...] = jnp.dot(u, wdt_ref[...], preferred_element_type=F32)


def _m_in_proj(h, gain, w_in, d_inner, conv_dim, n_heads, *, tm=256):
    T, D = h.shape
    wz = w_in[:, :d_inner].astype(BF16)
    wx = w_in[:, d_inner:d_inner + conv_dim].astype(BF16)
    wdt = jnp.pad(w_in[:, d_inner + conv_dim:], ((0, 0), (0, LANES - n_heads))).astype(BF16)
    return pl.pallas_call(
        _m_in_kernel,
        out_shape=(jax.ShapeDtypeStruct((T, d_inner), F32), jax.ShapeDtypeStruct((T, conv_dim), F32),
                   jax.ShapeDtypeStruct((T, LANES), F32)),
        grid=(T // tm,),
        in_specs=[pl.BlockSpec((tm, D), lambda i: (i, 0)), _resident((1, D)),
                  _resident((D, d_inner)), _resident((D, conv_dim)), _resident((D, LANES))],
        out_specs=(pl.BlockSpec((tm, d_inner), lambda i: (i, 0)), pl.BlockSpec((tm, conv_dim), lambda i: (i, 0)),
                   pl.BlockSpec((tm, LANES), lambda i: (i, 0))),
        compiler_params=_cparams(1),
        name="m_in_proj",
    )(h, gain.reshape(1, D), wz, wx, wdt)


def _head_expanders(n_heads):
    hh = jnp.arange(LANES)[:, None]
    e64 = (hh == (jnp.arange(n_heads * M_HEADDIM)[None, :] // M_HEADDIM)).astype(BF16)
    e128 = (hh == (jnp.arange(n_heads * LANES)[None, :] // LANES)).astype(BF16)
    return jnp.tile(e64, (3, 1)), jnp.tile(e128, (3, 1))


def _ssd_prompt_kernel(*refs, n_heads, n_aliased):
    (z_ref, xbc_ref, dt_ref, cw_ref, cb_ref, dtb_ref, alog_ref, dsk_ref, nw_ref,
     e64_ref, e128_ref, tri_ref, y_ref, ssm_ref, conv_ref, xpad_ref, s_ref) = refs[n_aliased:]
    c = pl.program_id(1)
    nc = pl.num_programs(1)
    Q = M_CHUNK
    d_inner = n_heads * M_HEADDIM
    gw = d_inner // M_GROUPS
    bc0 = d_inner
    cc0 = d_inner + M_GROUPS * M_STATE

    @pl.when(c == 0)
    def _():
        xpad_ref[:, 0:8, :] = jnp.zeros((xpad_ref.shape[0], 8, LANES), F32)
        s_ref[...] = jnp.zeros(s_ref.shape, F32)

    slabs = []
    for s in range(xpad_ref.shape[0]):
        lanes = slice(s * LANES, (s + 1) * LANES)
        xpad_ref[s, 8:8 + Q, :] = xbc_ref[:, lanes]
        xc = None
        for k in range(M_CONV_W):
            lo = 8 - (M_CONV_W - 1) + k
            term = xpad_ref[s, lo:lo + Q, :] * cw_ref[k:k + 1, lanes]
            xc = term if xc is None else xc + term
        slabs.append(_silu(xc + cb_ref[:, lanes]))
        xpad_ref[s, 0:8, :] = xpad_ref[s, Q:Q + 8, :]
    xs = jnp.concatenate(slabs, axis=1)
    xh = xs[:, :d_inner]

    dt = _softplus(dt_ref[...] + dtb_ref[...])
    dta = dt * (-jnp.exp(alog_ref[...]))
    ri = lax.broadcasted_iota(jnp.int32, (Q, Q), 0)
    ci = lax.broadcasted_iota(jnp.int32, (Q, Q), 1)
    causal = ri >= ci
    acum = _dot_sel_l(tri_ref[...], dta)
    a_full = _dot_sel_r(acum, e64_ref[...])
    dt_full = _dot_sel_r(dt, e64_ref[...])
    a_col = _dot_sel_r(acum, e128_ref[...])
    acum_t = acum.T
    a_last = a_full[Q - 1:Q, :]
    exp_a = jnp.exp(a_full)
    xdt = xh * dt_full
    xw = xdt * jnp.exp(a_last - a_full)
    chunk_decay = jnp.exp(a_last)
    lane = lax.broadcasted_iota(jnp.int32, (Q, LANES), 1)
    first_head = lane < M_HEADDIM
    hpg = n_heads // M_GROUPS

    def group(g):
        bg = xs[:, bc0 + g * M_STATE: bc0 + (g + 1) * M_STATE]
        cg = xs[:, cc0 + g * M_STATE: cc0 + (g + 1) * M_STATE]
        cb = _dot_nt(cg, bg)
        sl = slice(g * gw, (g + 1) * gw)
        s_g = s_ref[:, sl]
        y_g = _dot(cg, s_g) * exp_a[:, sl]
        s_ref[:, sl] = s_g * chunk_decay[:, sl] + _dot(bg.T, xw[:, sl])
        yield
        parts = []
        for q in range(hpg // 2):
            h0 = g * hpg + 2 * q
            xp = xdt[:, h0 * M_HEADDIM: h0 * M_HEADDIM + LANES]
            ys = []
            for hh in (h0, h0 + 1):
                seg = a_col[:, hh * LANES:(hh + 1) * LANES] - acum_t[hh:hh + 1, :]
                m = cb * jnp.exp(jnp.where(causal, seg, -jnp.inf))
                ys.append(_dot(m, xp))
            parts.append(jnp.where(first_head, ys[0], ys[1]))
            yield
        y_g = y_g + jnp.concatenate(parts, axis=1) + dsk_ref[:, sl] * xh[:, sl]
        y_g = y_g * _silu(z_ref[:, sl])
        y_g = y_g * lax.rsqrt(jnp.mean(y_g * y_g, axis=-1, keepdims=True) + M_NORM_EPS)
        y_ref[:, sl] = (y_g * nw_ref[:, sl]).astype(y_ref.dtype)

    _interleave(group(g) for g in range(M_GROUPS))

    @pl.when(c == nc - 1)
    def _():
        ssm_ref[0] = s_ref[...].T.reshape(n_heads, M_HEADDIM, M_STATE)
        conv_ref[0] = xbc_ref[Q - (M_CONV_W - 1):Q, :]


def _alias_prev(prev, first_out):
    prev = tuple(prev or ())
    specs = [pl.BlockSpec(memory_space=pl.ANY)] * len(prev)
    return prev, specs, {k: first_out + k for k in range(len(prev))}


def _ssd_prompt(z, xbc, dt, prm, B, L, n_heads, layer, n_layers, prev):
    T, d_inner = z.shape
    conv_dim = xbc.shape[1]
    Q = M_CHUNK
    nc = L // Q
    row = lambda b, c: (b * nc + c, 0)
    prev, prev_specs, aliases = _alias_prev(prev, first_out=1)
    kern = functools.partial(_ssd_prompt_kernel, n_heads=n_heads, n_aliased=len(prev))
    return pl.pallas_call(
        kern,
        out_shape=(jax.ShapeDtypeStruct((T, d_inner), BF16),
                   jax.ShapeDtypeStruct((n_layers, B, n_heads, M_HEADDIM, M_STATE), F32),
                   jax.ShapeDtypeStruct((n_layers, B, M_CONV_W - 1, conv_dim), F32)),
        grid=(B, nc),
        in_specs=prev_specs + [
            pl.BlockSpec((Q, d_inner), row), pl.BlockSpec((Q, conv_dim), row), pl.BlockSpec((Q, LANES), row),
            _resident((M_CONV_W, conv_dim)), _resident((1, conv_dim)),
            _resident((1, LANES)), _resident((1, LANES)), _resident((1, d_inner)), _resident((1, d_inner)),
            _resident((3 * LANES, d_inner)), _resident((3 * LANES, n_heads * LANES)), _resident((Q, 3 * Q))],
        out_specs=(pl.BlockSpec((Q, d_inner), row),
                   pl.BlockSpec((None, 1, n_heads, M_HEADDIM, M_STATE), lambda b, c: (layer, b, 0, 0, 0)),
                   pl.BlockSpec((None, 1, M_CONV_W - 1, conv_dim), lambda b, c: (layer, b, 0, 0))),
        scratch_shapes=[pltpu.VMEM((conv_dim // LANES, Q + 8, LANES), F32), pltpu.VMEM((M_STATE, d_inner), F32)],
        input_output_aliases=aliases,
        compiler_params=_cparams(2),
        name="ssd_prompt",
    )(*prev, z, xbc, dt, prm["conv_w"], prm["conv_b"], prm["dt_bias"], prm["a_log"], prm["d_full"], prm["norm_w"],
      prm["e64"], prm["e128"], prm["tri"])


def _ssd_sample_kernel(*refs, n_heads, seq, nb, n_aliased):
    (z_ref, xbc_ref, dt_ref, s0_ref, c0_ref, cw_ref, cb_ref, dtb_ref, alog_ref,
     dsk_ref, nw_ref, e64_ref, e128_ref, rowsel_ref, y_ref, ssm_ref, conv_ref, hist_ref) = refs[n_aliased:]
    d_inner = n_heads * M_HEADDIM
    gw = d_inner // M_GROUPS
    bc0 = d_inner
    cc0 = d_inner + M_GROUPS * M_STATE
    hpg = n_heads // M_GROUPS
    rows = nb * seq
    row = lax.broadcasted_iota(jnp.int32, (rows, 1), 0)
    tok = jnp.bitwise_and(row, seq - 1)
    seq_of_row = jnp.right_shift(row, int(math.log2(seq)))
    seq_of_col = jnp.right_shift(lax.broadcasted_iota(jnp.int32, (1, rows), 1), int(math.log2(seq)))
    eye = (lax.broadcasted_iota(jnp.int32, (LANES, LANES), 0)
           == lax.broadcasted_iota(jnp.int32, (LANES, LANES), 1)).astype(BF16)
    top_head = lax.broadcasted_iota(jnp.int32, (LANES, 1), 0) < M_HEADDIM

    @pl.when(pl.program_id(0) == 0)
    def _():
        hist_ref[...] = jnp.zeros(hist_ref.shape, F32)

    x = xbc_ref[...]
    for s in range(1, M_CONV_W):
        for bi in range(nb):
            hist_ref[s - 1, bi * seq:bi * seq + s, :] = c0_ref[bi, M_CONV_W - 1 - s:M_CONV_W - 1, :]
    xc = None
    for k in range(M_CONV_W):
        s = M_CONV_W - 1 - k
        tap = x if s == 0 else jnp.where(tok >= s, pltpu.roll(x, s, 0), hist_ref[s - 1])
        term = tap * cw_ref[k:k + 1, :]
        xc = term if xc is None else xc + term
    xs = _silu(xc + cb_ref[...])
    xh = xs[:, :d_inner]
    for bi in range(nb):
        conv_ref[bi] = xbc_ref[(bi + 1) * seq - (M_CONV_W - 1):(bi + 1) * seq, :]

    dt = _softplus(dt_ref[...] + dtb_ref[...])
    acum = dt * (-jnp.exp(alog_ref[...]))
    s = 1
    while s < seq:
        acum = acum + jnp.where(tok >= s, pltpu.roll(acum, s, 0), 0.0)
        s *= 2
    a_full = _dot_sel_r(acum, e64_ref[...])
    dt_full = _dot_sel_r(dt, e64_ref[...])
    a_col = _dot_sel_r(acum, e128_ref[...])
    exp_a = jnp.exp(a_full)
    xdt = xh * dt_full
    packed = jnp.concatenate([a_full, xdt, xs[:, bc0:cc0]], axis=1)
    from_tok = [_dot_sel_l(rowsel_ref[j], packed) for j in range(seq)]
    aux = xdt * jnp.exp(from_tok[seq - 1][:, :d_inner] - a_full)

    y = dsk_ref[...] * xh
    for j in range(seq):
        a_j = from_tok[j][:, :d_inner]
        wj = jnp.where(tok >= j, jnp.exp(a_full - a_j), 0.0) * from_tok[j][:, d_inner:2 * d_inner]
        parts = []
        for g in range(M_GROUPS):
            cgb = (xs[:, cc0 + g * M_STATE: cc0 + (g + 1) * M_STATE]
                   * from_tok[j][:, 2 * d_inner + g * M_STATE: 2 * d_inner + (g + 1) * M_STATE])
            parts.append(jnp.sum(cgb, axis=-1, keepdims=True) * wj[:, g * gw:(g + 1) * gw])
        y = y + jnp.concatenate(parts, axis=1)

    inter = [None] * (n_heads // 2)

    def head_pair(pr):
        g = (2 * pr) // hpg
        cg = xs[:, cc0 + g * M_STATE: cc0 + (g + 1) * M_STATE]
        bg = xs[:, bc0 + g * M_STATE: bc0 + (g + 1) * M_STATE]
        aux_t = _dot_nt(eye, aux[:, pr * LANES:(pr + 1) * LANES])
        h0 = [s0_ref[bi, 2 * pr:2 * pr + 2].reshape(2 * M_HEADDIM, M_STATE) for bi in range(nb)]
        reads = [_dot_nt(cg, h) for h in h0]
        yield
        acc = None
        for bi in range(nb):
            t = jnp.where(seq_of_row == bi, reads[bi], 0.0)
            acc = t if acc is None else acc + t
        inter[pr] = acc
        upd = [_dot(jnp.where(seq_of_col == bi, aux_t, 0.0), bg) for bi in range(nb)]
        yield
        for bi in range(nb):
            last = (bi + 1) * seq - 1
            dec = jnp.where(top_head, jnp.exp(a_col[last:last + 1, (2 * pr) * LANES:(2 * pr + 1) * LANES]),
                            jnp.exp(a_col[last:last + 1, (2 * pr + 1) * LANES:(2 * pr + 2) * LANES]))
            ssm_ref[bi, 2 * pr:2 * pr + 2] = (h0[bi] * dec + upd[bi]).reshape(2, M_HEADDIM, M_STATE)

    _interleave(head_pair(pr) for pr in range(n_heads // 2))
    y = y + jnp.concatenate(inter, axis=1) * exp_a

    y = y * _silu(z_ref[...])
    outs = []
    for g in range(M_GROUPS):
        yg = y[:, g * gw:(g + 1) * gw]
        outs.append(yg * lax.rsqrt(jnp.mean(yg * yg, axis=-1, keepdims=True) + M_NORM_EPS))
    y_ref[...] = (jnp.concatenate(outs, axis=1) * nw_ref[...]).astype(y_ref.dtype)


def _ssd_sample(y_all, z, xbc, dt, s0, c0, prm, row0, seq, n_heads, layer, prev, *, nb=4):
    T, d_inner = z.shape
    conv_dim = xbc.shape[1]
    DB = s0.shape[1]
    assert seq & (seq - 1) == 0 and seq >= M_CONV_W - 1
    rows = nb * seq
    r = jnp.arange(rows)
    rowsel = jnp.stack([jnp.tile((r[None, :] == ((r // seq) * seq + j)[:, None]).astype(BF16), (1, 3))
                        for j in range(seq)])
    blk0 = row0 // rows
    row = lambda i: (blk0 + i, 0)
    s_spec = pl.BlockSpec((None, nb, n_heads, M_HEADDIM, M_STATE), lambda i: (layer, i, 0, 0, 0))
    c_spec = pl.BlockSpec((None, nb, M_CONV_W - 1, conv_dim), lambda i: (layer, i, 0, 0))
    aliased, alias_specs, aliases = _alias_prev((y_all,) + tuple(prev or ()), first_out=0)
    kern = functools.partial(_ssd_sample_kernel, n_heads=n_heads, seq=seq, nb=nb, n_aliased=len(aliased))
    return pl.pallas_call(
        kern,
        out_shape=(jax.ShapeDtypeStruct((T, d_inner), y_all.dtype),
                   jax.ShapeDtypeStruct(s0.shape, F32), jax.ShapeDtypeStruct(c0.shape, F32)),
        grid=(DB // nb,),
        in_specs=alias_specs + [
            pl.BlockSpec((rows, d_inner), row), pl.BlockSpec((rows, conv_dim), row),
            pl.BlockSpec((rows, LANES), row), s_spec, c_spec,
            _resident((M_CONV_W, conv_dim)), _resident((1, conv_dim)), _resident((1, LANES)),
            _resident((1, LANES)), _resident((1, d_inner)), _resident((1, d_inner)),
            _resident((3 * LANES, d_inner)), _resident((3 * LANES, n_heads * LANES)),
            _resident((seq, rows, 3 * rows))],
        out_specs=(pl.BlockSpec((rows, d_inner), row), s_spec, c_spec),
        scratch_shapes=[pltpu.VMEM((M_CONV_W - 1, rows, conv_dim), F32)],
        input_output_aliases=aliases,
        compiler_params=_cparams(1),
        name="ssd_sample",
    )(*aliased, z, xbc, dt, s0, c0, prm["conv_w"], prm["conv_b"], prm["dt_bias"], prm["a_log"], prm["d_full"],
      prm["norm_w"], prm["e64"], prm["e128"], rowsel)


def _r_proj_kernel(*refs, tm, blocks_per_seq, n_prompt_blocks, sample_seq, has_vres):
    if has_vres:
        (h_ref, g_ref, pshift_ref, srows_ref, mu_ref, wr_ref, wk_ref, wv_ref, w1_ref, w2_ref, a1_ref, a2_ref,
         g1_ref, g2_ref, w0_ref, a0_ref, kk_ref, ka_ref, vf_ref, v0_ref, v1_ref, v2_ref,
         r_o, k_o, v_o, kkr_o, a_o, w_o, g_o, up_o, us_o, sc_ref, xx_ref) = refs
    else:
        (h_ref, g_ref, pshift_ref, srows_ref, mu_ref, wr_ref, wk_ref, wv_ref, w1_ref, w2_ref, a1_ref, a2_ref,
         g1_ref, g2_ref, w0_ref, a0_ref, kk_ref, ka_ref,
         r_o, k_o, v_o, kkr_o, a_o, w_o, g_o, up_o, us_o, sc_ref, xx_ref) = refs
    i = pl.program_id(0)
    u = _rms(h_ref[...], g_ref[...])
    sc_ref[8:8 + tm, :] = u

    @pl.when(jnp.logical_and(i % blocks_per_seq == blocks_per_seq - 1, i < n_prompt_blocks))
    def _():
        up_o[0] = u[tm - 8:tm, :]

    @pl.when(i >= n_prompt_blocks)
    def _():
        us_o[...] = u

    @pl.when(jnp.logical_and(i % blocks_per_seq == 0, i < n_prompt_blocks))
    def _():
        sc_ref[7:8, :] = pshift_ref[pl.ds(i // blocks_per_seq, 1), :]

    prev = sc_ref[7:7 + tm, :]
    row = lax.broadcasted_iota(jnp.int32, (tm, 1), 0)
    use_state = jnp.logical_and(i >= n_prompt_blocks, jnp.bitwise_and(row, sample_seq - 1) == 0)
    prev = jnp.where(use_state, srows_ref[...], prev)
    sc_ref[7:8, :] = sc_ref[7 + tm:8 + tm, :]

    xx_ref[...] = prev - u
    npair = r_o.shape[0]

    def rows(r0, n):
        u_h = sc_ref[8 + r0:8 + r0 + n, :]
        xx_h = xx_ref[r0:r0 + n, :]

        def mix(m):
            return u_h + xx_h * mu_ref[m:m + 1, :]

        xv = mix(3)
        r = _dot(mix(0), wr_ref[...])
        k = _dot(mix(2), wk_ref[...])
        v = _dot(xv, wv_ref[...])
        yield
        lw = _dot(mix(1), w1_ref[...])
        la = _dot(mix(4), a1_ref[...])
        lg = _dot(mix(5), g1_ref[...])
        lv = _dot(xv, v1_ref[...]) if has_vres else None
        yield
        w = -_softplus(-(w0_ref[...] + _dot(jnp.tanh(lw), w2_ref[...]))) - 0.5
        a = jax.nn.sigmoid(a0_ref[...] + _dot(la, a2_ref[...]))
        g = _dot(jax.nn.sigmoid(lg), g2_ref[...])
        if has_vres:
            vf = jnp.concatenate([vf_ref[p, r0:r0 + n, :] for p in range(npair)], axis=1)
            v = v + (vf - v) * jax.nn.sigmoid(v0_ref[...] + _dot(lv, v2_ref[...]))
        yield
        kkr = k * kk_ref[...]
        k = k * (1.0 + (a - 1.0) * ka_ref[...])
        for p in range(npair):
            sl = slice(p * LANES, (p + 1) * LANES)
            r_o[p, r0:r0 + n, :] = r[:, sl]
            k_o[p, r0:r0 + n, :] = k[:, sl]
            v_o[p, r0:r0 + n, :] = v[:, sl]
            kkr_o[p, r0:r0 + n, :] = kkr[:, sl]
            a_o[p, r0:r0 + n, :] = a[:, sl]
            w_o[p, r0:r0 + n, :] = w[:, sl]
            g_o[p, r0:r0 + n, :] = g[:, sl]

    n_groups = 2
    _interleave((rows(q * (tm // n_groups), tm // n_groups) for q in range(n_groups)))


def _pad_cols(w, n):
    return jnp.pad(w, ((0, 0), (0, n - w.shape[1])))


def _pad_rows(w, n):
    return jnp.pad(w, ((0, n - w.shape[0]), (0, 0)))


def _round_up(n, m):
    return -(-n // m) * m


def _r_proj(h, gain, pshift, srows, rp, v_first, L, n_prompt_rows, sample_seq, *, tm=256):
    T, D = h.shape
    npair = D // LANES
    has_vres = v_first is not None
    row2 = lambda i: (i, 0)
    npb = n_prompt_rows // tm
    lw = _round_up(rp["w1"].shape[1], LANES)
    la = _round_up(rp["a1"].shape[1], LANES)
    lg = _round_up(rp["g1"].shape[1], LANES)
    ops = [h, gain.reshape(1, D), pshift, srows, rp["mu"],
           rp["wr"].astype(BF16), rp["wk"].astype(BF16), rp["wv"].astype(BF16),
           _pad_cols(rp["w1"], lw).astype(BF16), _pad_rows(rp["w2"], lw).astype(BF16),
           _pad_cols(rp["a1"], la).astype(BF16), _pad_rows(rp["a2"], la).astype(BF16),
           _pad_cols(rp["g1"], lg).astype(BF16), _pad_rows(rp["g2"], lg).astype(BF16),
           rp["w0"].reshape(1, D), rp["a0"].reshape(1, D), rp["k_k"].reshape(1, D), rp["k_a"].reshape(1, D)]
    specs = [pl.BlockSpec((tm, D), row2), _resident((1, D)), _resident(pshift.shape),
             pl.BlockSpec((tm, D), lambda i: (jnp.maximum(i - npb, 0), 0)), _resident((6, D)),
             _resident((D, D)), _resident((D, D)), _resident((D, D)),
             _resident((D, lw)), _resident((lw, D)), _resident((D, la)), _resident((la, D)),
             _resident((D, lg)), _resident((lg, D)),
             _resident((1, D)), _resident((1, D)), _resident((1, D)), _resident((1, D))]
    pair_spec = pl.BlockSpec((npair, tm, LANES), lambda i: (0, i, 0))
    if has_vres:
        lv = _round_up(rp["v1"].shape[1], LANES)
        ops += [v_first, rp["v0"].reshape(1, D), _pad_cols(rp["v1"], lv).astype(BF16),
                _pad_rows(rp["v2"], lv).astype(BF16)]
        specs += [pair_spec, _resident((1, D)), _resident((D, lv)), _resident((lv, D))]
    pair_shape = jax.ShapeDtypeStruct((npair, T, LANES), F32)
    bps = L // tm
    n_seq = n_prompt_rows // L
    kern = functools.partial(_r_proj_kernel, tm=tm, blocks_per_seq=bps, n_prompt_blocks=npb,
                             sample_seq=sample_seq, has_vres=has_vres)
    return pl.pallas_call(
        kern,
        out_shape=(pair_shape,) * 7 + (jax.ShapeDtypeStruct((n_seq, 8, D), F32),
                                       jax.ShapeDtypeStruct((T - n_prompt_rows, D), F32)),
        grid=(T // tm,),
        in_specs=specs,
        out_specs=(pair_spec,) * 7 + (
            pl.BlockSpec((1, 8, D), lambda i: (jnp.minimum(i // bps, n_seq - 1), 0, 0)),
            pl.BlockSpec((tm, D), lambda i: (jnp.maximum(i - npb, 0), 0))),
        scratch_shapes=[pltpu.VMEM((tm + 8, D), F32), pltpu.VMEM((tm, D), F32)],
        compiler_params=_cparams(1),
        name="r_proj",
    )(*ops)


def _wkv_pair(r, k, v, kkr, a, w, g, rk, gnw, gnb, incl3, same3, states, lseq):
    C = r.shape[0]
    ns = C // lseq
    lane = lax.broadcasted_iota(jnp.int32, (1, LANES), 1)
    m0 = lane < R_HEADDIM

    def headsum(x):
        s0 = jnp.sum(jnp.where(m0, x, 0.0), axis=-1, keepdims=True)
        s1 = jnp.sum(jnp.where(m0, 0.0, x), axis=-1, keepdims=True)
        return jnp.where(m0, s0, s1)

    kk = kkr * lax.rsqrt(jnp.maximum(headsum(kkr * kkr), 1e-24))
    alpha = -kk
    beta = kk * a
    logd = -jnp.exp(w)

    ti = lax.broadcasted_iota(jnp.int32, (C, C), 0)
    si = lax.broadcasted_iota(jnp.int32, (C, C), 1)
    sh = int(math.log2(lseq))
    same = jnp.right_shift(ti, sh) == jnp.right_shift(si, sh)
    incl = jnp.logical_and(same, si <= ti)
    strict = jnp.logical_and(same, si < ti)
    lg = _dot_sel_l(incl3, logd)
    ltot = _dot_sel_l(same3, logd)
    yield
    g_in = jnp.exp(lg)
    ginv = jnp.exp(-lg)
    gend = jnp.exp(ltot - lg)
    w0 = alpha * jnp.exp(lg - logd)
    rg = r * g_in
    bi = beta * ginv
    ki = k * ginv
    wr = jnp.concatenate([w0, rg], axis=0)
    rsel = jnp.right_shift(jnp.bitwise_and(lax.broadcasted_iota(jnp.int32, (2 * C, 1), 0), C - 1), sh)

    xy0 = None
    for s in range(ns):
        t = _dot_nt(wr, states[s])
        if ns > 1:
            t = jnp.where(rsel == s, t, 0.0)
        xy0 = t if xy0 is None else xy0 + t
    x0 = xy0[:C]
    y0 = xy0[C:]
    yield

    eye = (ti == si).astype(F32)
    heads = (m0, jnp.logical_not(m0))
    wrm = [jnp.where(mh, wr, 0.0) for mh in heads]
    gb = [_dot_nt(x, bi) for x in wrm]
    gk = [_dot_nt(x, ki) for x in wrm]
    yield
    a_ab = [jnp.where(strict, x[:C], 0.0) for x in gb]
    a_ak = [jnp.where(strict, x[:C], 0.0) for x in gk]
    b_rb = [jnp.where(incl, x[C:], 0.0) for x in gb]
    b_rk = [jnp.where(incl, x[C:], 0.0) for x in gk]
    tinv = [eye + x for x in a_ab]
    pw = a_ab
    for _ in range(int(math.log2(lseq)) - 1):
        pw = [_dot(x, x) for x in pw]
        yield
        tinv = [t + _dot(t, x) for t, x in zip(tinv, pw)]
    xh = [x0 + _dot(x, v) for x in a_ak]
    yv = [_dot(x, v) for x in b_rk]
    yield
    uh = [_dot(t, x) for t, x in zip(tinv, xh)]
    yield
    yh = [y0 + _dot(b, x) + c for b, x, c in zip(b_rb, uh, yv)]
    u = jnp.where(m0, uh[0], uh[1])
    y = jnp.where(m0, yh[0], yh[1])
    yield

    uvt = jnp.concatenate([u, v], axis=0).T
    bkd = jnp.concatenate([beta * gend, k * gend], axis=0)
    bi_r = lax.broadcasted_iota(jnp.int32, (LANES, LANES), 0) >= R_HEADDIM
    bi_c = lax.broadcasted_iota(jnp.int32, (LANES, LANES), 1) >= R_HEADDIM
    same_head = bi_r == bi_c
    new_states = []
    for s in range(ns):
        rhs = bkd if ns == 1 else jnp.where(rsel == s, bkd, 0.0)
        ds = jnp.where(same_head, _dot(uvt, rhs), 0.0)
        new_states.append(states[s] * jnp.exp(ltot[s * lseq:s * lseq + 1, :]) + ds)
    yield

    mean = headsum(y) * (1.0 / R_HEADDIM)
    d = y - mean
    var = headsum(d * d) * (1.0 / R_HEADDIM)
    yn = d * lax.rsqrt(var + R_GN_EPS) * gnw + gnb
    out = (yn + headsum(r * k * rk) * v) * g
    return out, new_states


def _wkv_pair_chunk(r, k, v, kkr, a, w, g, rk, gnw, gnb, incl3, state_ref, y_ref):
    C = r.shape[0]
    H = 2 * C
    lane = lax.broadcasted_iota(jnp.int32, (1, LANES), 1)
    m0 = lane < R_HEADDIM

    def headsum(x):
        s0 = jnp.sum(jnp.where(m0, x, 0.0), axis=-1, keepdims=True)
        s1 = jnp.sum(jnp.where(m0, 0.0, x), axis=-1, keepdims=True)
        return jnp.where(m0, s0, s1)

    def stack(x):
        return jnp.concatenate([jnp.where(m0, x, 0.0), jnp.where(m0, 0.0, x)], axis=0)

    kk = kkr * lax.rsqrt(jnp.maximum(headsum(kkr * kkr), 1e-24))
    beta = kk * a
    logd = -jnp.exp(w)
    lg = _dot_sel_l(incl3, logd)
    yield
    ltot = lg[C - 1:C, :]
    ginv = jnp.exp(-lg)
    gend = jnp.exp(ltot - lg)
    w0s = stack(-kk * jnp.exp(lg - logd))
    rgs = stack(r * jnp.exp(lg))
    vs = stack(v)
    bds = stack(beta * gend)
    kds = stack(k * gend)
    gram = _dot_nt(jnp.concatenate([w0s, rgs], axis=0),
                   jnp.concatenate([stack(beta * ginv), stack(k * ginv)], axis=0))
    yield
    tr = jnp.bitwise_and(lax.broadcasted_iota(jnp.int32, (H, H), 0), C - 1)
    sc = jnp.bitwise_and(lax.broadcasted_iota(jnp.int32, (H, H), 1), C - 1)
    strict = sc < tr
    incl = sc <= tr
    a_ab = jnp.where(strict, gram[:H, :H], 0.0)
    a_ak = jnp.where(strict, gram[:H, H:], 0.0)
    b_rb = jnp.where(incl, gram[H:, :H], 0.0)
    b_rk = jnp.where(incl, gram[H:, H:], 0.0)
    eye = (lax.broadcasted_iota(jnp.int32, (H, H), 0) == lax.broadcasted_iota(jnp.int32, (H, H), 1)).astype(F32)
    tinv = eye + a_ab
    pw = _dot(a_ab, a_ab)
    av = _dot(a_ak, vs)
    bv = _dot(b_rk, vs)
    yield
    for _ in range(int(math.log2(C)) - 2):
        tinv = tinv + _dot(tinv, pw)
        pw = _dot(pw, pw)
        yield
    tinv = tinv + _dot(tinv, pw)
    yield
    tw = _dot(tinv, jnp.concatenate([w0s, av], axis=1))
    yield
    br = _dot(b_rb, tw)
    m_t = _dot(tw[:, :LANES].T, bds)
    z = _dot(jnp.concatenate([tw[:, LANES:], vs], axis=0).T, jnp.concatenate([bds, kds], axis=0))
    yield
    r_t = rgs + br[:, :LANES]
    y_t = br[:, LANES:] + bv
    state = state_ref[...]
    ys = _dot_nt(r_t, state) + y_t
    state_ref[...] = state * jnp.exp(ltot) + _dot(state, m_t) + z
    yield
    y = ys[:C] + ys[C:]
    mean = headsum(y) * (1.0 / R_HEADDIM)
    d = y - mean
    var = headsum(d * d) * (1.0 / R_HEADDIM)
    yn = d * lax.rsqrt(var + R_GN_EPS) * gnw + gnb
    y_ref[...] = ((yn + headsum(r * k * rk) * v) * g).astype(y_ref.dtype)


def _interleave(generators):
    live = list(generators)
    while live:
        nxt = []
        for gen in live:
            try:
                next(gen)
                nxt.append(gen)
            except StopIteration:
                pass
        live = nxt


def _wkv_prompt_kernel(*refs, steps_per_seq, n_aliased):
    (r_ref, k_ref, v_ref, kkr_ref, a_ref, w_ref, g_ref, rk_ref, gnw_ref, gnb_ref, incl_ref,
     y_ref, s_out_ref, s_ref) = refs[n_aliased:]
    c = pl.program_id(0) % steps_per_seq
    npair = r_ref.shape[0]
    C = R_CHUNK

    @pl.when(c == 0)
    def _():
        s_ref[...] = jnp.zeros(s_ref.shape, F32)

    def chunk(p, q):
        rows = slice(q * C, (q + 1) * C)
        return _wkv_pair_chunk(r_ref[p, rows], k_ref[p, rows], v_ref[p, rows], kkr_ref[p, rows], a_ref[p, rows],
                               w_ref[p, rows], g_ref[p, rows], rk_ref[p], gnw_ref[p], gnb_ref[p], incl_ref[...],
                               s_ref.at[p], y_ref.at[p, pl.ds(q * C, C)])

    _interleave(chunk(p, q) for q in range(r_ref.shape[1] // C) for p in range(npair))

    @pl.when(c == steps_per_seq - 1)
    def _():
        for p in range(npair):
            s = s_ref[p]
            s_out_ref[0, 2 * p:2 * p + 2] = (s[:, :R_HEADDIM] + s[:, R_HEADDIM:]).reshape(2, R_HEADDIM, R_HEADDIM)


def _wkv_prompt(streams, rk, gnw, gnb, B, L, layer, n_layers, prev):
    npair, T, _ = streams[0].shape
    C = R_CHUNK
    rows = R_PROMPT_CHUNKS_PER_STEP * C
    cps = L // rows
    blk = pl.BlockSpec((npair, rows, LANES), lambda i: (0, i, 0))
    vec = _resident((npair, 1, LANES))
    prev, prev_specs, aliases = _alias_prev(prev, first_out=1)
    return pl.pallas_call(
        functools.partial(_wkv_prompt_kernel, steps_per_seq=cps, n_aliased=len(prev)),
        out_shape=(jax.ShapeDtypeStruct((npair, T, LANES), BF16),
                   jax.ShapeDtypeStruct((n_layers, B, 2 * npair, R_HEADDIM, R_HEADDIM), F32)),
        grid=(B * cps,),
        in_specs=prev_specs + [blk] * 7 + [vec] * 3 + [_resident((C, 3 * C))],
        out_specs=(blk, pl.BlockSpec((None, 1, 2 * npair, R_HEADDIM, R_HEADDIM),
                                     lambda i: (layer, i // cps, 0, 0, 0))),
        scratch_shapes=[pltpu.VMEM((npair, LANES, LANES), F32)],
        input_output_aliases=aliases,
        compiler_params=_cparams(1),
        name="wkv_prompt",
    )(*prev, *streams, rk, gnw, gnb, _within_sequence_selectors(C, C)[0])


def _wkv_sample_kernel(*refs, seq, n_aliased):
    (r_ref, k_ref, v_ref, kkr_ref, a_ref, w_ref, g_ref, rk_ref, gnw_ref, gnb_ref, incl_ref, same_ref,
     s0_ref, y_ref, s_out_ref) = refs[n_aliased:]
    npair = r_ref.shape[0]
    ns = R_CHUNK // seq
    lane = lax.broadcasted_iota(jnp.int32, (LANES, LANES), 1) >= R_HEADDIM
    rowh = lax.broadcasted_iota(jnp.int32, (LANES, LANES), 0) >= R_HEADDIM
    same_head = lane == rowh

    def pair(p):
        states = []
        for s in range(ns):
            s2 = s0_ref[s, 2 * p:2 * p + 2].reshape(LANES, R_HEADDIM)
            states.append(jnp.where(same_head, jnp.concatenate([s2, s2], axis=1), 0.0))
        out, new = yield from _wkv_pair(r_ref[p], k_ref[p], v_ref[p], kkr_ref[p], a_ref[p], w_ref[p], g_ref[p],
                                        rk_ref[p], gnw_ref[p], gnb_ref[p], incl_ref[...], same_ref[...],
                                        states, seq)
        y_ref[p] = out.astype(y_ref.dtype)
        for s in range(ns):
            s_out_ref[s, 2 * p:2 * p + 2] = (new[s][:, :R_HEADDIM] + new[s][:, R_HEADDIM:]).reshape(
                2, R_HEADDIM, R_HEADDIM)

    _interleave(pair(p) for p in range(npair))


def _wkv_sample(y_all, streams, rk, gnw, gnb, s0, row0, seq, layer, prev):
    npair, T, _ = streams[0].shape
    C = R_CHUNK
    ns = C // seq
    DB = s0.shape[1]
    blk0 = row0 // C
    blk = pl.BlockSpec((npair, C, LANES), lambda i: (0, blk0 + i, 0))
    vec = _resident((npair, 1, LANES))
    sblk = pl.BlockSpec((None, ns, 2 * npair, R_HEADDIM, R_HEADDIM), lambda i: (layer, i, 0, 0, 0))
    aliased, alias_specs, aliases = _alias_prev((y_all,) + tuple(prev or ()), first_out=0)
    return pl.pallas_call(
        functools.partial(_wkv_sample_kernel, seq=seq, n_aliased=len(aliased)),
        out_shape=(jax.ShapeDtypeStruct((npair, T, LANES), BF16), jax.ShapeDtypeStruct(s0.shape, F32)),
        grid=(DB // ns,),
        in_specs=alias_specs + [blk] * 7 + [vec] * 3 + [_resident((C, 3 * C))] * 2 + [sblk],
        out_specs=(blk, sblk),
        input_output_aliases=aliases,
        compiler_params=_cparams(1),
        name="wkv_sample",
    )(*aliased, *streams, rk, gnw, gnb, *_within_sequence_selectors(C, seq), s0)


def kernel(x_prompt, x_sample, p_prompt, p_sample, state_ssm, state_conv, state_wkv, state_shift, norm_ffn1, ffn1_gate_up, ffn1_down, norm_mix, norm_ffn2, ffn2_gate_up, ffn2_down, norm_ple, ple_in, ple_gate, norm_final, m_in_proj, m_conv_w, m_conv_b, m_dt_bias, m_A_log, m_D, m_norm, m_out_proj, r_mu, r_wr, r_wk, r_wv, r_wo, r_w0, r_w1, r_w2, r_a0, r_a1, r_a2, r_g1, r_g2, r_k_k, r_k_a, r_r_k, r_gn_w, r_gn_b, r_v0, r_v1, r_v2):
    B, L, D = x_prompt.shape
    DB, DL, _ = x_sample.shape
    depth = norm_ffn1.shape[0]
    TP, TS = B * L, DB * DL
    n_heads = m_A_log.shape[1]
    d_inner = n_heads * M_HEADDIM
    conv_dim = m_conv_w.shape[2]
    npair = D // LANES

    h = (x_prompt.reshape(TP, D), x_sample.reshape(TS, D))
    pp = p_prompt.reshape(depth, TP, -1)
    ps = p_sample.reshape(depth, TS, -1)
    e64, e128 = _head_expanders(n_heads)
    n_mamba = state_ssm.shape[0]
    n_rwkv = state_wkv.shape[0]

    tri = _within_sequence_selectors(M_CHUNK, M_CHUNK)[0]

    m_prompt = m_sample = wkv_p = wkv_s = None
    shift_p, shift_s = [], []
    v_first = None
    for i in range(depth):
        j = i // 2
        h = _ffn(h, norm_ffn1[i], ffn1_gate_up[i], ffn1_down[i])
        if i % 2 == 0:
            prm = dict(conv_w=m_conv_w[j], conv_b=m_conv_b[j].reshape(1, conv_dim),
                       dt_bias=jnp.pad(m_dt_bias[j], (0, LANES - n_heads)).reshape(1, LANES),
                       a_log=jnp.pad(m_A_log[j], (0, LANES - n_heads)).reshape(1, LANES),
                       d_full=jnp.repeat(m_D[j], M_HEADDIM).reshape(1, d_inner),
                       norm_w=m_norm[j].reshape(1, d_inner), e64=e64, e128=e128, tri=tri)
            z, xbc, dt = _m_in_proj(h, norm_mix[i], m_in_proj[j], d_inner, conv_dim, n_heads)
            y, *m_prompt = _ssd_prompt(z, xbc, dt, prm, B, L, n_heads, j, n_mamba, m_prompt)
            y, *m_sample = _ssd_sample(y, z, xbc, dt, state_ssm, state_conv, prm, TP, DL, n_heads, j, m_sample)
            w_mix = m_out_proj[j]
        else:
            rp = dict(mu=r_mu[j], wr=r_wr[j], wk=r_wk[j], wv=r_wv[j], w0=r_w0[j], w1=r_w1[j], w2=r_w2[j],
                      a0=r_a0[j], a1=r_a1[j], a2=r_a2[j], g1=r_g1[j], g2=r_g2[j], k_k=r_k_k[j], k_a=r_k_a[j])
            if j > 0:
                rp.update(v0=r_v0[j - 1], v1=r_v1[j - 1], v2=r_v2[j - 1])
            pshift = jnp.zeros((B, D), F32)
            srows = jnp.zeros((DB, DL, D), F32).at[:, 0].set(state_shift[j]).reshape(TS, D)
            outs = _r_proj(h, norm_mix[i], pshift, srows, rp, v_first if j > 0 else None, L, TP, DL)
            streams, u_tail, u_sample = outs[:7], outs[7], outs[8]
            if j == 0:
                v_first = streams[2]
            rk = r_r_k[j].reshape(npair, 1, LANES)
            gnw = r_gn_w[j].reshape(npair, 1, LANES)
            gnb = r_gn_b[j].reshape(npair, 1, LANES)
            y, wkv_p = _wkv_prompt(streams, rk, gnw, gnb, B, L, j, n_rwkv, None if wkv_p is None else (wkv_p,))
            y, wkv_s = _wkv_sample(y, streams, rk, gnw, gnb, state_wkv, TP, DL, j, None if wkv_s is None else (wkv_s,))
            shift_p.append(u_tail[:, 7])
            shift_s.append(u_sample.reshape(DB, DL, D)[:, DL - 1])
            w_mix = r_wo[j]
        h = _layer_tail(h, y, w_mix, norm_ffn2[i], ffn2_gate_up[i], ffn2_down[i], pp, ps, i, norm_ple[i],
                        ple_in[i], ple_gate[i], norm_final, final=(i == depth - 1))

    y_prompt, y_sample = h
    return (y_prompt.reshape(B, L, D), y_sample.reshape(DB, DL, D),
            m_prompt[0], m_prompt[1], wkv_p, jnp.stack(shift_p),
            m_sample[0], m_sample[1], wkv_s, jnp.stack(shift_s))
```

```python
import functools
import math

import jax
import jax.numpy as jnp
from jax import lax
from jax.experimental import pallas as pl
from jax.experimental.pallas import tpu as pltpu

F32 = jnp.float32
BF16 = jnp.bfloat16

NORM_EPS = 1e-6
M_NORM_EPS = 1e-5
R_GN_EPS = 64e-5

LANES = 128
M_HEADDIM = 64
M_STATE = 128
M_GROUPS = 8
M_CHUNK = 128
M_CONV_W = 4
R_HEADDIM = 64
R_CHUNK = 64
R_PROMPT_CHUNKS_PER_STEP = 4
VMEM_LIMIT_BYTES = 56 * 1024 * 1024


def _cparams(n_axes):
    return pltpu.CompilerParams(dimension_semantics=("arbitrary",) * n_axes,
                                vmem_limit_bytes=VMEM_LIMIT_BYTES)


def _resident(shape):
    zeros = (0,) * len(shape)
    return pl.BlockSpec(shape, lambda *_: zeros, pipeline_mode=pl.Buffered(1))


def _resident_layer(stack, layer):
    index = (layer,) + (0,) * (stack.ndim - 1)
    return pl.BlockSpec((None,) + stack.shape[1:], lambda *_: index, pipeline_mode=pl.Buffered(1))


def _dot(a, b):
    return jnp.dot(a.astype(BF16), b.astype(BF16), preferred_element_type=F32)


def _dot_nt(a, b):
    return lax.dot_general(a.astype(BF16), b.astype(BF16), (((1,), (1,)), ((), ())),
                           preferred_element_type=F32)


def _split3(x):
    hi = x.astype(BF16)
    r1 = x - hi.astype(F32)
    mid = r1.astype(BF16)
    lo = (r1 - mid.astype(F32)).astype(BF16)
    return hi, mid, lo


def _dot_sel_l(sel3, x):
    return jnp.dot(sel3, jnp.concatenate(_split3(x), axis=0), preferred_element_type=F32)


def _dot_sel_r(x, sel3):
    return jnp.dot(jnp.concatenate(_split3(x), axis=1), sel3, preferred_element_type=F32)


def _within_sequence_selectors(rows, seq):
    t = jnp.arange(rows)[:, None]
    s = jnp.arange(rows)[None, :]
    same = (t // seq) == (s // seq)
    incl = jnp.logical_and(same, s <= t)
    return jnp.tile(incl.astype(BF16), (1, 3)), jnp.tile(same.astype(BF16), (1, 3))


def _rms(x, g):
    ms = jnp.mean(x * x, axis=-1, keepdims=True)
    return (x * lax.rsqrt(ms + NORM_EPS)) * g


def _silu(x):
    return x * jax.nn.sigmoid(x)


def _softplus(x):
    return jnp.maximum(x, 0.0) + jnp.log1p(jnp.exp(-jnp.abs(x)))


def _split_rows_specs(tm, width, n_first_blocks, lead=()):
    none = (None,) * len(lead)
    first = pl.BlockSpec(none + (tm, width), lambda i: lead + (jnp.minimum(i, n_first_blocks - 1), 0))
    second = pl.BlockSpec(none + (tm, width), lambda i: lead + (jnp.maximum(i - n_first_blocks, 0), 0))
    return first, second


def _pick_rows(first_ref, second_ref, n_first_blocks):
    return jnp.where(pl.program_id(0) < n_first_blocks, first_ref[...], second_ref[...])


def _ffn_rows(x, g_ref, wgu_ref, wd_ref, ub_ref, fc):
    ub_ref[...] = _rms(x, g_ref[...]).astype(BF16)
    F = wd_ref.shape[0]
    acc = x
    for c in range(F // fc):
        ub = ub_ref[...]
        gate = jnp.dot(ub, wgu_ref[:, c * fc:(c + 1) * fc], preferred_element_type=F32)
        up = jnp.dot(ub, wgu_ref[:, F + c * fc:F + (c + 1) * fc], preferred_element_type=F32)
        act = (_silu(gate) * up).astype(BF16)
        acc = acc + 0.5 * jnp.dot(act, wd_ref[c * fc:(c + 1) * fc, :], preferred_element_type=F32)
    return acc


def _ffn_kernel(*refs, fc, n_first_blocks):
    if n_first_blocks is None:
        h_ref, g_ref, wgu_ref, wd_ref, o_ref, ub_ref = refs
        x = h_ref[...]
    else:
        hp_ref, hs_ref, g_ref, wgu_ref, wd_ref, o_ref, ub_ref = refs
        x = _pick_rows(hp_ref, hs_ref, n_first_blocks)
    o_ref[...] = _ffn_rows(x, g_ref, wgu_ref, wd_ref, ub_ref, fc)


def _ffn(h, gain, w_gate_up, w_down, layer, *, tm=512, fc=256):
    parts = h if isinstance(h, tuple) else (h,)
    T = sum(p.shape[0] for p in parts)
    D = parts[0].shape[1]
    if len(parts) == 2:
        nfb = parts[0].shape[0] // tm
        h_specs = list(_split_rows_specs(tm, D, nfb))
    else:
        nfb = None
        h_specs = [pl.BlockSpec((tm, D), lambda i: (i, 0))]
    return pl.pallas_call(
        functools.partial(_ffn_kernel, fc=fc, n_first_blocks=nfb),
        out_shape=jax.ShapeDtypeStruct((T, D), F32),
        grid=(T // tm,),
        in_specs=h_specs + [_resident((1, D)), _resident_layer(w_gate_up, layer), _resident_layer(w_down, layer)],
        out_specs=pl.BlockSpec((tm, D), lambda i: (i, 0)),
        scratch_shapes=[pltpu.VMEM((tm, D), BF16)],
        compiler_params=_cparams(1),
        name="ffn",
    )(*parts, gain.reshape(1, D), w_gate_up, w_down)


def _tail_kernel(*refs, fc, final, n_first_blocks, y_pairs):
    (h_ref, y_ref, wm_ref, g2_ref, wgu_ref, wd_ref, pp_ref, ps_ref, gp_ref, wi_ref, wg_ref) = refs[:11]
    if final:
        gf_ref, op_ref, os_ref, ub_ref = refs[11:]
    else:
        o_ref, ub_ref = refs[11:]
    y = jnp.concatenate([y_ref[p] for p in range(y_pairs)], axis=1) if y_pairs else y_ref[...]
    x = h_ref[...] + jnp.dot(y, wm_ref[...], preferred_element_type=F32)
    x = _ffn_rows(x, g2_ref, wgu_ref, wd_ref, ub_ref, fc)
    gate = jax.nn.sigmoid(_dot(_rms(x, gp_ref[...]), wg_ref[...]))
    x = x + _dot(_pick_rows(pp_ref, ps_ref, n_first_blocks), wi_ref[...]) * gate
    if not final:
        o_ref[...] = x
        return
    x = _rms(x, gf_ref[...])
    i = pl.program_id(0)

    @pl.when(i < n_first_blocks)
    def _():
        op_ref[...] = x

    @pl.when(i >= n_first_blocks)
    def _():
        os_ref[...] = x


def _layer_tail(h, y, w_mix, mixer_layer, gain_ffn, w_gate_up, w_down, p_first, p_second, layer, gain_ple, w_in,
                w_gate, gain_final, *, final, tm=512, fc=256):
    T, D = h.shape
    P = p_first.shape[2]
    nfb = p_first.shape[1] // tm
    row = pl.BlockSpec((tm, D), lambda i: (i, 0))
    y_pairs = y.shape[0] if y.ndim == 3 else 0
    y_spec = (pl.BlockSpec((y_pairs, tm, LANES), lambda i: (0, i, 0)) if y_pairs
              else pl.BlockSpec((tm, y.shape[1]), lambda i: (i, 0)))
    ops = [h, y, w_mix, gain_ffn.reshape(1, D), w_gate_up, w_down,
           p_first, p_second, gain_ple.reshape(1, D), w_in, w_gate]
    specs = [row, y_spec, _resident_layer(w_mix, mixer_layer), _resident((1, D)),
             _resident_layer(w_gate_up, layer), _resident_layer(w_down, layer),
             *_split_rows_specs(tm, P, nfb, lead=(layer,)), _resident((1, D)),
             _resident_layer(w_in, layer), _resident_layer(w_gate, layer)]
    if final:
        ops.append(gain_final.reshape(1, D))
        specs.append(_resident((1, D)))
        out_shape = (jax.ShapeDtypeStruct((p_first.shape[1], D), F32), jax.ShapeDtypeStruct((p_second.shape[1], D), F32))
        out_specs = _split_rows_specs(tm, D, nfb)
    else:
        out_shape = jax.ShapeDtypeStruct((T, D), F32)
        out_specs = row
    return pl.pallas_call(
        functools.partial(_tail_kernel, fc=fc, final=final, n_first_blocks=nfb, y_pairs=y_pairs),
        out_shape=out_shape,
        grid=(T // tm,),
        in_specs=specs,
        out_specs=out_specs,
        scratch_shapes=[pltpu.VMEM((tm, D), BF16)],
        compiler_params=_cparams(1),
        name="layer_tail",
    )(*ops)


def _m_in_kernel(h_ref, g_ref, wz_ref, wx_ref, wdt_ref, z_ref, xbc_ref, dt_ref):
    u = _rms(h_ref[...], g_ref[...]).astype(BF16)
    z_ref[...] = jnp.dot(u, wz_ref[...], preferred_element_type=F32)
    xbc_ref[...] = jnp.dot(u, wx_ref[...], preferred_element_type=F32)
    dt_ref[...] = jnp.dot(u, wdt_ref[...], preferred_element_type=F32)


def _m_in_proj(h, gain, w_in, d_inner, conv_dim, n_heads, *, tm=256):
    T, D = h.shape
    wz = w_in[:, :d_inner].astype(BF16)
    wx = w_in[:, d_inner:d_inner + conv_dim].astype(BF16)
    wdt = jnp.pad(w_in[:, d_inner + conv_dim:], ((0, 0), (0, LANES - n_heads))).astype(BF16)
    return pl.pallas_call(
        _m_in_kernel,
        out_shape=(jax.ShapeDtypeStruct((T, d_inner), F32), jax.ShapeDtypeStruct((T, conv_dim), F32),
                   jax.ShapeDtypeStruct((T, LANES), F32)),
        grid=(T // tm,),
        in_specs=[pl.BlockSpec((tm, D), lambda i: (i, 0)), _resident((1, D)),
                  _resident((D, d_inner)), _resident((D, conv_dim)), _resident((D, LANES))],
        out_specs=(pl.BlockSpec((tm, d_inner), lambda i: (i, 0)), pl.BlockSpec((tm, conv_dim), lambda i: (i, 0)),
                   pl.BlockSpec((tm, LANES), lambda i: (i, 0))),
        compiler_params=_cparams(1),
        name="m_in_proj",
    )(h, gain.reshape(1, D), wz, wx, wdt)


def _head_expanders(n_heads):
    hh = jnp.arange(LANES)[:, None]
    e64 = (hh == (jnp.arange(n_heads * M_HEADDIM)[None, :] // M_HEADDIM)).astype(BF16)
    e128 = (hh == (jnp.arange(n_heads * LANES)[None, :] // LANES)).astype(BF16)
    return jnp.tile(e64, (3, 1)), jnp.tile(e128, (3, 1))


def _ssd_prompt_kernel(*refs, n_heads, n_aliased):
    (z_ref, xbc_ref, dt_ref, cw_ref, cb_ref, dtb_ref, alog_ref, dsk_ref, nw_ref,
     e64_ref, e128_ref, tri_ref, y_ref, ssm_ref, conv_ref, xpad_ref, s_ref) = refs[n_aliased:]
    c = pl.program_id(1)
    nc = pl.num_programs(1)
    Q = M_CHUNK
    d_inner = n_heads * M_HEADDIM
    gw = d_inner // M_GROUPS
    bc0 = d_inner
    cc0 = d_inner + M_GROUPS * M_STATE

    @pl.when(c == 0)
    def _():
        xpad_ref[:, 0:8, :] = jnp.zeros((xpad_ref.shape[0], 8, LANES), F32)
        s_ref[...] = jnp.zeros(s_ref.shape, F32)

    slabs = []
    for s in range(xpad_ref.shape[0]):
        lanes = slice(s * LANES, (s + 1) * LANES)
        xpad_ref[s, 8:8 + Q, :] = xbc_ref[:, lanes]
        xc = None
        for k in range(M_CONV_W):
            lo = 8 - (M_CONV_W - 1) + k
            term = xpad_ref[s, lo:lo + Q, :] * cw_ref[k:k + 1, lanes]
            xc = term if xc is None else xc + term
        slabs.append(_silu(xc + cb_ref[:, lanes]))
        xpad_ref[s, 0:8, :] = xpad_ref[s, Q:Q + 8, :]
    xs = jnp.concatenate(slabs, axis=1)
    xh = xs[:, :d_inner]

    dt = _softplus(dt_ref[...] + dtb_ref[...])
    dta = dt * (-jnp.exp(alog_ref[...]))
    ri = lax.broadcasted_iota(jnp.int32, (Q, Q), 0)
    ci = lax.broadcasted_iota(jnp.int32, (Q, Q), 1)
    causal = ri >= ci
    acum = _dot_sel_l(tri_ref[...], dta)
    a_full = _dot_sel_r(acum, e64_ref[...])
    dt_full = _dot_sel_r(dt, e64_ref[...])
    a_col = _dot_sel_r(acum, e128_ref[...])
    acum_t = acum.T
    a_last = a_full[Q - 1:Q, :]
    exp_a = jnp.exp(a_full)
    xdt = xh * dt_full
    xw = xdt * jnp.exp(a_last - a_full)
    chunk_decay = jnp.exp(a_last)
    lane = lax.broadcasted_iota(jnp.int32, (Q, LANES), 1)
    first_head = lane < M_HEADDIM
    hpg = n_heads // M_GROUPS

    def group(g):
        bg = xs[:, bc0 + g * M_STATE: bc0 + (g + 1) * M_STATE]
        cg = xs[:, cc0 + g * M_STATE: cc0 + (g + 1) * M_STATE]
        cb = _dot_nt(cg, bg)
        sl = slice(g * gw, (g + 1) * gw)
        s_g = s_ref[:, sl]
        y_g = _dot(cg, s_g) * exp_a[:, sl]
        s_ref[:, sl] = s_g * chunk_decay[:, sl] + _dot(bg.T, xw[:, sl])
        yield
        parts = []
        for q in range(hpg // 2):
            h0 = g * hpg + 2 * q
            xp = xdt[:, h0 * M_HEADDIM: h0 * M_HEADDIM + LANES]
            ys = []
            for hh in (h0, h0 + 1):
                seg = a_col[:, hh * LANES:(hh + 1) * LANES] - acum_t[hh:hh + 1, :]
                m = cb * jnp.exp(jnp.where(causal, seg, -jnp.inf))
                ys.append(_dot(m, xp))
            parts.append(jnp.where(first_head, ys[0], ys[1]))
            yield
        y_g = y_g + jnp.concatenate(parts, axis=1) + dsk_ref[:, sl] * xh[:, sl]
        y_g = y_g * _silu(z_ref[:, sl])
        y_g = y_g * lax.rsqrt(jnp.mean(y_g * y_g, axis=-1, keepdims=True) + M_NORM_EPS)
        y_ref[:, sl] = (y_g * nw_ref[:, sl]).astype(y_ref.dtype)

    _interleave(group(g) for g in range(M_GROUPS))

    @pl.when(c == nc - 1)
    def _():
        ssm_ref[0] = s_ref[...].T.reshape(n_heads, M_HEADDIM, M_STATE)
        conv_ref[0] = xbc_ref[Q - (M_CONV_W - 1):Q, :]


def _alias_prev(prev, first_out):
    prev = tuple(prev or ())
    specs = [pl.BlockSpec(memory_space=pl.ANY)] * len(prev)
    return prev, specs, {k: first_out + k for k in range(len(prev))}


def _ssd_prompt(z, xbc, dt, prm, B, L, n_heads, layer, n_layers, prev):
    T, d_inner = z.shape
    conv_dim = xbc.shape[1]
    Q = M_CHUNK
    nc = L // Q
    row = lambda b, c: (b * nc + c, 0)
    prev, prev_specs, aliases = _alias_prev(prev, first_out=1)
    kern = functools.partial(_ssd_prompt_kernel, n_heads=n_heads, n_aliased=len(prev))
    return pl.pallas_call(
        kern,
        out_shape=(jax.ShapeDtypeStruct((T, d_inner), BF16),
                   jax.ShapeDtypeStruct((n_layers, B, n_heads, M_HEADDIM, M_STATE), F32),
                   jax.ShapeDtypeStruct((n_layers, B, M_CONV_W - 1, conv_dim), F32)),
        grid=(B, nc),
        in_specs=prev_specs + [
            pl.BlockSpec((Q, d_inner), row), pl.BlockSpec((Q, conv_dim), row), pl.BlockSpec((Q, LANES), row),
            _resident((M_CONV_W, conv_dim)), _resident((1, conv_dim)),
            _resident((1, LANES)), _resident((1, LANES)), _resident((1, d_inner)), _resident((1, d_inner)),
            _resident((3 * LANES, d_inner)), _resident((3 * LANES, n_heads * LANES)), _resident((Q, 3 * Q))],
        out_specs=(pl.BlockSpec((Q, d_inner), row),
                   pl.BlockSpec((None, 1, n_heads, M_HEADDIM, M_STATE), lambda b, c: (layer, b, 0, 0, 0)),
                   pl.BlockSpec((None, 1, M_CONV_W - 1, conv_dim), lambda b, c: (layer, b, 0, 0))),
        scratch_shapes=[pltpu.VMEM((conv_dim // LANES, Q + 8, LANES), F32), pltpu.VMEM((M_STATE, d_inner), F32)],
        input_output_aliases=aliases,
        compiler_params=_cparams(2),
        name="ssd_prompt",
    )(*prev, z, xbc, dt, prm["conv_w"], prm["conv_b"], prm["dt_bias"], prm["a_log"], prm["d_full"], prm["norm_w"],
      prm["e64"], prm["e128"], prm["tri"])


def _ssd_sample_kernel(*refs, n_heads, seq, nb, n_aliased):
    (z_ref, xbc_ref, dt_ref, s0_ref, c0_ref, cw_ref, cb_ref, dtb_ref, alog_ref,
     dsk_ref, nw_ref, e64_ref, e128_ref, rowsel_ref, y_ref, ssm_ref, conv_ref, hist_ref) = refs[n_aliased:]
    d_inner = n_heads * M_HEADDIM
    gw = d_inner // M_GROUPS
    bc0 = d_inner
    cc0 = d_inner + M_GROUPS * M_STATE
    hpg = n_heads // M_GROUPS
    rows = nb * seq
    row = lax.broadcasted_iota(jnp.int32, (rows, 1), 0)
    tok = jnp.bitwise_and(row, seq - 1)
    seq_of_row = jnp.right_shift(row, int(math.log2(seq)))
    seq_of_col = jnp.right_shift(lax.broadcasted_iota(jnp.int32, (1, rows), 1), int(math.log2(seq)))
    eye = (lax.broadcasted_iota(jnp.int32, (LANES, LANES), 0)
           == lax.broadcasted_iota(jnp.int32, (LANES, LANES), 1)).astype(BF16)
    top_head = lax.broadcasted_iota(jnp.int32, (LANES, 1), 0) < M_HEADDIM

    @pl.when(pl.program_id(0) == 0)
    def _():
        hist_ref[...] = jnp.zeros(hist_ref.shape, F32)

    x = xbc_ref[...]
    for s in range(1, M_CONV_W):
        for bi in range(nb):
            hist_ref[s - 1, bi * seq:bi * seq + s, :] = c0_ref[bi, M_CONV_W - 1 - s:M_CONV_W - 1, :]
    xc = None
    for k in range(M_CONV_W):
        s = M_CONV_W - 1 - k
        tap = x if s == 0 else jnp.where(tok >= s, pltpu.roll(x, s, 0), hist_ref[s - 1])
        term = tap * cw_ref[k:k + 1, :]
        xc = term if xc is None else xc + term
    xs = _silu(xc + cb_ref[...])
    xh = xs[:, :d_inner]
    for bi in range(nb):
        conv_ref[bi] = xbc_ref[(bi + 1) * seq - (M_CONV_W - 1):(bi + 1) * seq, :]

    dt = _softplus(dt_ref[...] + dtb_ref[...])
    acum = dt * (-jnp.exp(alog_ref[...]))
    s = 1
    while s < seq:
        acum = acum + jnp.where(tok >= s, pltpu.roll(acum, s, 0), 0.0)
        s *= 2
    a_full = _dot_sel_r(acum, e64_ref[...])
    dt_full = _dot_sel_r(dt, e64_ref[...])
    a_col = _dot_sel_r(acum, e128_ref[...])
    exp_a = jnp.exp(a_full)
    xdt = xh * dt_full
    packed = jnp.concatenate([a_full, xdt, xs[:, bc0:cc0]], axis=1)
    from_tok = [_dot_sel_l(rowsel_ref[j], packed) for j in range(seq)]
    aux = xdt * jnp.exp(from_tok[seq - 1][:, :d_inner] - a_full)

    y = dsk_ref[...] * xh
    for j in range(seq):
        a_j = from_tok[j][:, :d_inner]
        wj = jnp.where(tok >= j, jnp.exp(a_full - a_j), 0.0) * from_tok[j][:, d_inner:2 * d_inner]
        parts = []
        for g in range(M_GROUPS):
            cgb = (xs[:, cc0 + g * M_STATE: cc0 + (g + 1) * M_STATE]
                   * from_tok[j][:, 2 * d_inner + g * M_STATE: 2 * d_inner + (g + 1) * M_STATE])
            parts.append(jnp.sum(cgb, axis=-1, keepdims=True) * wj[:, g * gw:(g + 1) * gw])
        y = y + jnp.concatenate(parts, axis=1)

    inter = [None] * (n_heads // 2)

    def head_pair(pr):
        g = (2 * pr) // hpg
        cg = xs[:, cc0 + g * M_STATE: cc0 + (g + 1) * M_STATE]
        bg = xs[:, bc0 + g * M_STATE: bc0 + (g + 1) * M_STATE]
        aux_t = _dot_nt(eye, aux[:, pr * LANES:(pr + 1) * LANES])
        h0 = [s0_ref[bi, 2 * pr:2 * pr + 2].reshape(2 * M_HEADDIM, M_STATE) for bi in range(nb)]
        reads = [_dot_nt(cg, h) for h in h0]
        yield
        acc = None
        for bi in range(nb):
            t = jnp.where(seq_of_row == bi, reads[bi], 0.0)
            acc = t if acc is None else acc + t
        inter[pr] = acc
        upd = [_dot(jnp.where(seq_of_col == bi, aux_t, 0.0), bg) for bi in range(nb)]
        yield
        for bi in range(nb):
            last = (bi + 1) * seq - 1
            dec = jnp.where(top_head, jnp.exp(a_col[last:last + 1, (2 * pr) * LANES:(2 * pr + 1) * LANES]),
                            jnp.exp(a_col[last:last + 1, (2 * pr + 1) * LANES:(2 * pr + 2) * LANES]))
            ssm_ref[bi, 2 * pr:2 * pr + 2] = (h0[bi] * dec + upd[bi]).reshape(2, M_HEADDIM, M_STATE)

    _interleave(head_pair(pr) for pr in range(n_heads // 2))
    y = y + jnp.concatenate(inter, axis=1) * exp_a

    y = y * _silu(z_ref[...])
    outs = []
    for g in range(M_GROUPS):
        yg = y[:, g * gw:(g + 1) * gw]
        outs.append(yg * lax.rsqrt(jnp.mean(yg * yg, axis=-1, keepdims=True) + M_NORM_EPS))
    y_ref[...] = (jnp.concatenate(outs, axis=1) * nw_ref[...]).astype(y_ref.dtype)


def _ssd_sample(y_all, z, xbc, dt, s0, c0, prm, row0, seq, n_heads, layer, prev, *, nb=4):
    T, d_inner = z.shape
    conv_dim = xbc.shape[1]
    DB = s0.shape[1]
    assert seq & (seq - 1) == 0 and seq >= M_CONV_W - 1
    rows = nb * seq
    r = jnp.arange(rows)
    rowsel = jnp.stack([jnp.tile((r[None, :] == ((r // seq) * seq + j)[:, None]).astype(BF16), (1, 3))
                        for j in range(seq)])
    blk0 = row0 // rows
    row = lambda i: (blk0 + i, 0)
    s_spec = pl.BlockSpec((None, nb, n_heads, M_HEADDIM, M_STATE), lambda i: (layer, i, 0, 0, 0))
    c_spec = pl.BlockSpec((None, nb, M_CONV_W - 1, conv_dim), lambda i: (layer, i, 0, 0))
    aliased, alias_specs, aliases = _alias_prev((y_all,) + tuple(prev or ()), first_out=0)
    kern = functools.partial(_ssd_sample_kernel, n_heads=n_heads, seq=seq, nb=nb, n_aliased=len(aliased))
    return pl.pallas_call(
        kern,
        out_shape=(jax.ShapeDtypeStruct((T, d_inner), y_all.dtype),
                   jax.ShapeDtypeStruct(s0.shape, F32), jax.ShapeDtypeStruct(c0.shape, F32)),
        grid=(DB // nb,),
        in_specs=alias_specs + [
            pl.BlockSpec((rows, d_inner), row), pl.BlockSpec((rows, conv_dim), row),
            pl.BlockSpec((rows, LANES), row), s_spec, c_spec,
            _resident((M_CONV_W, conv_dim)), _resident((1, conv_dim)), _resident((1, LANES)),
            _resident((1, LANES)), _resident((1, d_inner)), _resident((1, d_inner)),
            _resident((3 * LANES, d_inner)), _resident((3 * LANES, n_heads * LANES)),
            _resident((seq, rows, 3 * rows))],
        out_specs=(pl.BlockSpec((rows, d_inner), row), s_spec, c_spec),
        scratch_shapes=[pltpu.VMEM((M_CONV_W - 1, rows, conv_dim), F32)],
        input_output_aliases=aliases,
        compiler_params=_cparams(1),
        name="ssd_sample",
    )(*aliased, z, xbc, dt, s0, c0, prm["conv_w"], prm["conv_b"], prm["dt_bias"], prm["a_log"], prm["d_full"],
      prm["norm_w"], prm["e64"], prm["e128"], rowsel)


def _r_proj_kernel(*refs, tm, blocks_per_seq, n_prompt_blocks, sample_seq, has_vres):
    if has_vres:
        (h_ref, g_ref, pshift_ref, srows_ref, mu_ref, wr_ref, wk_ref, wv_ref, w1_ref, w2_ref, a1_ref, a2_ref,
         g1_ref, g2_ref, w0_ref, a0_ref, kk_ref, ka_ref, vf_ref, v0_ref, v1_ref, v2_ref,
         r_o, k_o, v_o, kkr_o, a_o, w_o, g_o, up_o, us_o, sc_ref, xx_ref) = refs
    else:
        (h_ref, g_ref, pshift_ref, srows_ref, mu_ref, wr_ref, wk_ref, wv_ref, w1_ref, w2_ref, a1_ref, a2_ref,
         g1_ref, g2_ref, w0_ref, a0_ref, kk_ref, ka_ref,
         r_o, k_o, v_o, kkr_o, a_o, w_o, g_o, up_o, us_o, sc_ref, xx_ref) = refs
    i = pl.program_id(0)
    u = _rms(h_ref[...], g_ref[...])
    sc_ref[8:8 + tm, :] = u

    @pl.when(jnp.logical_and(i % blocks_per_seq == blocks_per_seq - 1, i < n_prompt_blocks))
    def _():
        up_o[0] = u[tm - 8:tm, :]

    @pl.when(i >= n_prompt_blocks)
    def _():
        us_o[...] = u

    @pl.when(jnp.logical_and(i % blocks_per_seq == 0, i < n_prompt_blocks))
    def _():
        sc_ref[7:8, :] = pshift_ref[pl.ds(i // blocks_per_seq, 1), :]

    prev = sc_ref[7:7 + tm, :]
    row = lax.broadcasted_iota(jnp.int32, (tm, 1), 0)
    use_state = jnp.logical_and(i >= n_prompt_blocks, jnp.bitwise_and(row, sample_seq - 1) == 0)
    prev = jnp.where(use_state, srows_ref[...], prev)
    sc_ref[7:8, :] = sc_ref[7 + tm:8 + tm, :]

    xx_ref[...] = prev - u
    npair = r_o.shape[0]

    def rows(r0, n):
        u_h = sc_ref[8 + r0:8 + r0 + n, :]
        xx_h = xx_ref[r0:r0 + n, :]

        def mix(m):
            return u_h + xx_h * mu_ref[m:m + 1, :]

        xv = mix(3)
        r = _dot(mix(0), wr_ref[...])
        k = _dot(mix(2), wk_ref[...])
        v = _dot(xv, wv_ref[...])
        yield
        lw = _dot(mix(1), w1_ref[...])
        la = _dot(mix(4), a1_ref[...])
        lg = _dot(mix(5), g1_ref[...])
        lv = _dot(xv, v1_ref[...]) if has_vres else None
        yield
        w = -_softplus(-(w0_ref[...] + _dot(jnp.tanh(lw), w2_ref[...]))) - 0.5
        a = jax.nn.sigmoid(a0_ref[...] + _dot(la, a2_ref[...]))
        g = _dot(jax.nn.sigmoid(lg), g2_ref[...])
        if has_vres:
            vf = jnp.concatenate([vf_ref[p, r0:r0 + n, :] for p in range(npair)], axis=1)
            v = v + (vf - v) * jax.nn.sigmoid(v0_ref[...] + _dot(lv, v2_ref[...]))
        yield
        kkr = k * kk_ref[...]
        k = k * (1.0 + (a - 1.0) * ka_ref[...])
        for p in range(npair):
            sl = slice(p * LANES, (p + 1) * LANES)
            r_o[p, r0:r0 + n, :] = r[:, sl]
            k_o[p, r0:r0 + n, :] = k[:, sl]
            v_o[p, r0:r0 + n, :] = v[:, sl]
            kkr_o[p, r0:r0 + n, :] = kkr[:, sl]
            a_o[p, r0:r0 + n, :] = a[:, sl]
            w_o[p, r0:r0 + n, :] = w[:, sl]
            g_o[p, r0:r0 + n, :] = g[:, sl]

    n_groups = 2
    _interleave((rows(q * (tm // n_groups), tm // n_groups) for q in range(n_groups)))


def _pad_cols(w, n):
    return jnp.pad(w, ((0, 0), (0, n - w.shape[1])))


def _pad_rows(w, n):
    return jnp.pad(w, ((0, n - w.shape[0]), (0, 0)))


def _round_up(n, m):
    return -(-n // m) * m


def _r_proj(h, gain, pshift, srows, rp, v_first, L, n_prompt_rows, sample_seq, layer, *, tm=256):
    T, D = h.shape
    npair = D // LANES
    has_vres = v_first is not None
    row2 = lambda i: (i, 0)
    npb = n_prompt_rows // tm
    lw = _round_up(rp["w1"].shape[1], LANES)
    la = _round_up(rp["a1"].shape[1], LANES)
    lg = _round_up(rp["g1"].shape[1], LANES)
    ops = [h, gain.reshape(1, D), pshift, srows, rp["mu"],
           rp["wr"], rp["wk"], rp["wv"],
           _pad_cols(rp["w1"], lw).astype(BF16), _pad_rows(rp["w2"], lw).astype(BF16),
           _pad_cols(rp["a1"], la).astype(BF16), _pad_rows(rp["a2"], la).astype(BF16),
           _pad_cols(rp["g1"], lg).astype(BF16), _pad_rows(rp["g2"], lg).astype(BF16),
           rp["w0"].reshape(1, D), rp["a0"].reshape(1, D), rp["k_k"].reshape(1, D), rp["k_a"].reshape(1, D)]
    specs = [pl.BlockSpec((tm, D), row2), _resident((1, D)), _resident(pshift.shape),
             pl.BlockSpec((tm, D), lambda i: (jnp.maximum(i - npb, 0), 0)), _resident((6, D)),
             _resident_layer(rp["wr"], layer), _resident_layer(rp["wk"], layer), _resident_layer(rp["wv"], layer),
             _resident((D, lw)), _resident((lw, D)), _resident((D, la)), _resident((la, D)),
             _resident((D, lg)), _resident((lg, D)),
             _resident((1, D)), _resident((1, D)), _resident((1, D)), _resident((1, D))]
    pair_spec = pl.BlockSpec((npair, tm, LANES), lambda i: (0, i, 0))
    if has_vres:
        lv = _round_up(rp["v1"].shape[1], LANES)
        ops += [v_first, rp["v0"].reshape(1, D), _pad_cols(rp["v1"], lv).astype(BF16),
                _pad_rows(rp["v2"], lv).astype(BF16)]
        specs += [pair_spec, _resident((1, D)), _resident((D, lv)), _resident((lv, D))]
    pair_shape = jax.ShapeDtypeStruct((npair, T, LANES), F32)
    bps = L // tm
    n_seq = n_prompt_rows // L
    kern = functools.partial(_r_proj_kernel, tm=tm, blocks_per_seq=bps, n_prompt_blocks=npb,
                             sample_seq=sample_seq, has_vres=has_vres)
    return pl.pallas_call(
        kern,
        out_shape=(pair_shape,) * 7 + (jax.ShapeDtypeStruct((n_seq, 8, D), F32),
                                       jax.ShapeDtypeStruct((T - n_prompt_rows, D), F32)),
        grid=(T // tm,),
        in_specs=specs,
        out_specs=(pair_spec,) * 7 + (
            pl.BlockSpec((1, 8, D), lambda i: (jnp.minimum(i // bps, n_seq - 1), 0, 0)),
            pl.BlockSpec((tm, D), lambda i: (jnp.maximum(i - npb, 0), 0))),
        scratch_shapes=[pltpu.VMEM((tm + 8, D), F32), pltpu.VMEM((tm, D), F32)],
        compiler_params=_cparams(1),
        name="r_proj",
    )(*ops)


def _wkv_pair(r, k, v, kkr, a, w, g, rk, gnw, gnb, incl3, same3, states, lseq):
    C = r.shape[0]
    ns = C // lseq
    lane = lax.broadcasted_iota(jnp.int32, (1, LANES), 1)
    m0 = lane < R_HEADDIM

    def headsum(x):
        s0 = jnp.sum(jnp.where(m0, x, 0.0), axis=-1, keepdims=True)
        s1 = jnp.sum(jnp.where(m0, 0.0, x), axis=-1, keepdims=True)
        return jnp.where(m0, s0, s1)

    kk = kkr * lax.rsqrt(jnp.maximum(headsum(kkr * kkr), 1e-24))
    alpha = -kk
    beta = kk * a
    logd = -jnp.exp(w)

    ti = lax.broadcasted_iota(jnp.int32, (C, C), 0)
    si = lax.broadcasted_iota(jnp.int32, (C, C), 1)
    sh = int(math.log2(lseq))
    same = jnp.right_shift(ti, sh) == jnp.right_shift(si, sh)
    incl = jnp.logical_and(same, si <= ti)
    strict = jnp.logical_and(same, si < ti)
    lg = _dot_sel_l(incl3, logd)
    ltot = _dot_sel_l(same3, logd)
    yield
    g_in = jnp.exp(lg)
    ginv = jnp.exp(-lg)
    gend = jnp.exp(ltot - lg)
    w0 = alpha * jnp.exp(lg - logd)
    rg = r * g_in
    bi = beta * ginv
    ki = k * ginv
    wr = jnp.concatenate([w0, rg], axis=0)
    rsel = jnp.right_shift(jnp.bitwise_and(lax.broadcasted_iota(jnp.int32, (2 * C, 1), 0), C - 1), sh)

    xy0 = None
    for s in range(ns):
        t = _dot_nt(wr, states[s])
        if ns > 1:
            t = jnp.where(rsel == s, t, 0.0)
        xy0 = t if xy0 is None else xy0 + t
    x0 = xy0[:C]
    y0 = xy0[C:]
    yield

    eye = (ti == si).astype(F32)
    heads = (m0, jnp.logical_not(m0))
    wrm = [jnp.where(mh, wr, 0.0) for mh in heads]
    gb = [_dot_nt(x, bi) for x in wrm]
    gk = [_dot_nt(x, ki) for x in wrm]
    yield
    a_ab = [jnp.where(strict, x[:C], 0.0) for x in gb]
    a_ak = [jnp.where(strict, x[:C], 0.0) for x in gk]
    b_rb = [jnp.where(incl, x[C:], 0.0) for x in gb]
    b_rk = [jnp.where(incl, x[C:], 0.0) for x in gk]
    tinv = [eye + x for x in a_ab]
    pw = a_ab
    for _ in range(int(math.log2(lseq)) - 1):
        pw = [_dot(x, x) for x in pw]
        yield
        tinv = [t + _dot(t, x) for t, x in zip(tinv, pw)]
    xh = [x0 + _dot(x, v) for x in a_ak]
    yv = [_dot(x, v) for x in b_rk]
    yield
    uh = [_dot(t, x) for t, x in zip(tinv, xh)]
    yield
    yh = [y0 + _dot(b, x) + c for b, x, c in zip(b_rb, uh, yv)]
    u = jnp.where(m0, uh[0], uh[1])
    y = jnp.where(m0, yh[0], yh[1])
    yield

    uvt = jnp.concatenate([u, v], axis=0).T
    bkd = jnp.concatenate([beta * gend, k * gend], axis=0)
    bi_r = lax.broadcasted_iota(jnp.int32, (LANES, LANES), 0) >= R_HEADDIM
    bi_c = lax.broadcasted_iota(jnp.int32, (LANES, LANES), 1) >= R_HEADDIM
    same_head = bi_r == bi_c
    new_states = []
    for s in range(ns):
        rhs = bkd if ns == 1 else jnp.where(rsel == s, bkd, 0.0)
        ds = jnp.where(same_head, _dot(uvt, rhs), 0.0)
        new_states.append(states[s] * jnp.exp(ltot[s * lseq:s * lseq + 1, :]) + ds)
    yield

    mean = headsum(y) * (1.0 / R_HEADDIM)
    d = y - mean
    var = headsum(d * d) * (1.0 / R_HEADDIM)
    yn = d * lax.rsqrt(var + R_GN_EPS) * gnw + gnb
    out = (yn + headsum(r * k * rk) * v) * g
    return out, new_states


def _wkv_pair_chunk(r, k, v, kkr, a, w, g, rk, gnw, gnb, incl3, state_ref, y_ref):
    C = r.shape[0]
    H = 2 * C
    lane = lax.broadcasted_iota(jnp.int32, (1, LANES), 1)
    m0 = lane < R_HEADDIM

    def headsum(x):
        s0 = jnp.sum(jnp.where(m0, x, 0.0), axis=-1, keepdims=True)
        s1 = jnp.sum(jnp.where(m0, 0.0, x), axis=-1, keepdims=True)
        return jnp.where(m0, s0, s1)

    def stack(x):
        return jnp.concatenate([jnp.where(m0, x, 0.0), jnp.where(m0, 0.0, x)], axis=0)

    kk = kkr * lax.rsqrt(jnp.maximum(headsum(kkr * kkr), 1e-24))
    beta = kk * a
    logd = -jnp.exp(w)
    lg = _dot_sel_l(incl3, logd)
    yield
    ltot = lg[C - 1:C, :]
    ginv = jnp.exp(-lg)
    gend = jnp.exp(ltot - lg)
    w0s = stack(-kk * jnp.exp(lg - logd))
    rgs = stack(r * jnp.exp(lg))
    vs = stack(v)
    bds = stack(beta * gend)
    kds = stack(k * gend)
    gram = _dot_nt(jnp.concatenate([w0s, rgs], axis=0),
                   jnp.concatenate([stack(beta * ginv), stack(k * ginv)], axis=0))
    yield
    tr = jnp.bitwise_and(lax.broadcasted_iota(jnp.int32, (H, H), 0), C - 1)
    sc = jnp.bitwise_and(lax.broadcasted_iota(jnp.int32, (H, H), 1), C - 1)
    strict = sc < tr
    incl = sc <= tr
    a_ab = jnp.where(strict, gram[:H, :H], 0.0)
    a_ak = jnp.where(strict, gram[:H, H:], 0.0)
    b_rb = jnp.where(incl, gram[H:, :H], 0.0)
    b_rk = jnp.where(incl, gram[H:, H:], 0.0)
    eye = (lax.broadcasted_iota(jnp.int32, (H, H), 0) == lax.broadcasted_iota(jnp.int32, (H, H), 1)).astype(F32)
    tinv = eye + a_ab
    pw = _dot(a_ab, a_ab)
    av = _dot(a_ak, vs)
    bv = _dot(b_rk, vs)
    yield
    for _ in range(int(math.log2(C)) - 2):
        tinv = tinv + _dot(tinv, pw)
        pw = _dot(pw, pw)
        yield
    tinv = tinv + _dot(tinv, pw)
    yield
    tw = _dot(tinv, jnp.concatenate([w0s, av], axis=1))
    yield
    br = _dot(b_rb, tw)
    m_t = _dot(tw[:, :LANES].T, bds)
    z = _dot(jnp.concatenate([tw[:, LANES:], vs], axis=0).T, jnp.concatenate([bds, kds], axis=0))
    yield
    r_t = rgs + br[:, :LANES]
    y_t = br[:, LANES:] + bv
    state = state_ref[...]
    ys = _dot_nt(r_t, state) + y_t
    state_ref[...] = state * jnp.exp(ltot) + _dot(state, m_t) + z
    yield
    y = ys[:C] + ys[C:]
    mean = headsum(y) * (1.0 / R_HEADDIM)
    d = y - mean
    var = headsum(d * d) * (1.0 / R_HEADDIM)
    yn = d * lax.rsqrt(var + R_GN_EPS) * gnw + gnb
    y_ref[...] = ((yn + headsum(r * k * rk) * v) * g).astype(y_ref.dtype)


def _interleave(generators):
    live = list(generators)
    while live:
        nxt = []
        for gen in live:
            try:
                next(gen)
                nxt.append(gen)
            except StopIteration:
                pass
        live = nxt


def _wkv_prompt_kernel(*refs, steps_per_seq, n_aliased):
    (r_ref, k_ref, v_ref, kkr_ref, a_ref, w_ref, g_ref, rk_ref, gnw_ref, gnb_ref, incl_ref,
     y_ref, s_out_ref, s_ref) = refs[n_aliased:]
    c = pl.program_id(0) % steps_per_seq
    npair = r_ref.shape[0]
    C = R_CHUNK

    @pl.when(c == 0)
    def _():
        s_ref[...] = jnp.zeros(s_ref.shape, F32)

    def chunk(p, q):
        rows = slice(q * C, (q + 1) * C)
        return _wkv_pair_chunk(r_ref[p, rows], k_ref[p, rows], v_ref[p, rows], kkr_ref[p, rows], a_ref[p, rows],
                               w_ref[p, rows], g_ref[p, rows], rk_ref[p], gnw_ref[p], gnb_ref[p], incl_ref[...],
                               s_ref.at[p], y_ref.at[p, pl.ds(q * C, C)])

    _interleave(chunk(p, q) for q in range(r_ref.shape[1] // C) for p in range(npair))

    @pl.when(c == steps_per_seq - 1)
    def _():
        for p in range(npair):
            s = s_ref[p]
            s_out_ref[0, 2 * p:2 * p + 2] = (s[:, :R_HEADDIM] + s[:, R_HEADDIM:]).reshape(2, R_HEADDIM, R_HEADDIM)


def _wkv_prompt(streams, rk, gnw, gnb, B, L, layer, n_layers, prev):
    npair, T, _ = streams[0].shape
    C = R_CHUNK
    rows = R_PROMPT_CHUNKS_PER_STEP * C
    cps = L // rows
    blk = pl.BlockSpec((npair, rows, LANES), lambda i: (0, i, 0))
    vec = _resident((npair, 1, LANES))
    prev, prev_specs, aliases = _alias_prev(prev, first_out=1)
    return pl.pallas_call(
        functools.partial(_wkv_prompt_kernel, steps_per_seq=cps, n_aliased=len(prev)),
        out_shape=(jax.ShapeDtypeStruct((npair, T, LANES), BF16),
                   jax.ShapeDtypeStruct((n_layers, B, 2 * npair, R_HEADDIM, R_HEADDIM), F32)),
        grid=(B * cps,),
        in_specs=prev_specs + [blk] * 7 + [vec] * 3 + [_resident((C, 3 * C))],
        out_specs=(blk, pl.BlockSpec((None, 1, 2 * npair, R_HEADDIM, R_HEADDIM),
                                     lambda i: (layer, i // cps, 0, 0, 0))),
        scratch_shapes=[pltpu.VMEM((npair, LANES, LANES), F32)],
        input_output_aliases=aliases,
        compiler_params=_cparams(1),
        name="wkv_prompt",
    )(*prev, *streams, rk, gnw, gnb, _within_sequence_selectors(C, C)[0])


def _wkv_sample_kernel(*refs, seq, n_aliased):
    (r_ref, k_ref, v_ref, kkr_ref, a_ref, w_ref, g_ref, rk_ref, gnw_ref, gnb_ref, incl_ref, same_ref,
     s0_ref, y_ref, s_out_ref) = refs[n_aliased:]
    npair = r_ref.shape[0]
    ns = R_CHUNK // seq
    lane = lax.broadcasted_iota(jnp.int32, (LANES, LANES), 1) >= R_HEADDIM
    rowh = lax.broadcasted_iota(jnp.int32, (LANES, LANES), 0) >= R_HEADDIM
    same_head = lane == rowh

    def pair(p):
        states = []
        for s in range(ns):
            s2 = s0_ref[s, 2 * p:2 * p + 2].reshape(LANES, R_HEADDIM)
            states.append(jnp.where(same_head, jnp.concatenate([s2, s2], axis=1), 0.0))
        out, new = yield from _wkv_pair(r_ref[p], k_ref[p], v_ref[p], kkr_ref[p], a_ref[p], w_ref[p], g_ref[p],
                                        rk_ref[p], gnw_ref[p], gnb_ref[p], incl_ref[...], same_ref[...],
                                        states, seq)
        y_ref[p] = out.astype(y_ref.dtype)
        for s in range(ns):
            s_out_ref[s, 2 * p:2 * p + 2] = (new[s][:, :R_HEADDIM] + new[s][:, R_HEADDIM:]).reshape(
                2, R_HEADDIM, R_HEADDIM)

    _interleave(pair(p) for p in range(npair))


def _wkv_sample(y_all, streams, rk, gnw, gnb, s0, row0, seq, layer, prev):
    npair, T, _ = streams[0].shape
    C = R_CHUNK
    ns = C // seq
    DB = s0.shape[1]
    blk0 = row0 // C
    blk = pl.BlockSpec((npair, C, LANES), lambda i: (0, blk0 + i, 0))
    vec = _resident((npair, 1, LANES))
    sblk = pl.BlockSpec((None, ns, 2 * npair, R_HEADDIM, R_HEADDIM), lambda i: (layer, i, 0, 0, 0))
    aliased, alias_specs, aliases = _alias_prev((y_all,) + tuple(prev or ()), first_out=0)
    return pl.pallas_call(
        functools.partial(_wkv_sample_kernel, seq=seq, n_aliased=len(aliased)),
        out_shape=(jax.ShapeDtypeStruct((npair, T, LANES), BF16), jax.ShapeDtypeStruct(s0.shape, F32)),
        grid=(DB // ns,),
        in_specs=alias_specs + [blk] * 7 + [vec] * 3 + [_resident((C, 3 * C))] * 2 + [sblk],
        out_specs=(blk, sblk),
        input_output_aliases=aliases,
        compiler_params=_cparams(1),
        name="wkv_sample",
    )(*aliased, *streams, rk, gnw, gnb, *_within_sequence_selectors(C, seq), s0)


def kernel(x_prompt, x_sample, p_prompt, p_sample, state_ssm, state_conv, state_wkv, state_shift, norm_ffn1, ffn1_gate_up, ffn1_down, norm_mix, norm_ffn2, ffn2_gate_up, ffn2_down, norm_ple, ple_in, ple_gate, norm_final, m_in_proj, m_conv_w, m_conv_b, m_dt_bias, m_A_log, m_D, m_norm, m_out_proj, r_mu, r_wr, r_wk, r_wv, r_wo, r_w0, r_w1, r_w2, r_a0, r_a1, r_a2, r_g1, r_g2, r_k_k, r_k_a, r_r_k, r_gn_w, r_gn_b, r_v0, r_v1, r_v2):
    B, L, D = x_prompt.shape
    DB, DL, _ = x_sample.shape
    depth = norm_ffn1.shape[0]
    TP, TS = B * L, DB * DL
    n_heads = m_A_log.shape[1]
    d_inner = n_heads * M_HEADDIM
    conv_dim = m_conv_w.shape[2]
    npair = D // LANES

    h = (x_prompt.reshape(TP, D), x_sample.reshape(TS, D))
    pp = p_prompt.reshape(depth, TP, -1)
    ps = p_sample.reshape(depth, TS, -1)
    e64, e128 = _head_expanders(n_heads)
    n_mamba = state_ssm.shape[0]
    n_rwkv = state_wkv.shape[0]

    tri = _within_sequence_selectors(M_CHUNK, M_CHUNK)[0]
    ffn1_gu, ffn1_dn = ffn1_gate_up.astype(BF16), ffn1_down.astype(BF16)
    ffn2_gu, ffn2_dn = ffn2_gate_up.astype(BF16), ffn2_down.astype(BF16)
    ple_in_b, ple_gate_b = ple_in.astype(BF16), ple_gate.astype(BF16)
    mix_out = (m_out_proj.astype(BF16), r_wo.astype(BF16))
    r_wr_b, r_wk_b, r_wv_b = r_wr.astype(BF16), r_wk.astype(BF16), r_wv.astype(BF16)

    m_prompt = m_sample = wkv_p = wkv_s = None
    shift_p, shift_s = [], []
    v_first = None
    for i in range(depth):
        j = i // 2
        h = _ffn(h, norm_ffn1[i], ffn1_gu, ffn1_dn, i)
        if i % 2 == 0:
            prm = dict(conv_w=m_conv_w[j], conv_b=m_conv_b[j].reshape(1, conv_dim),
                       dt_bias=jnp.pad(m_dt_bias[j], (0, LANES - n_heads)).reshape(1, LANES),
                       a_log=jnp.pad(m_A_log[j], (0, LANES - n_heads)).reshape(1, LANES),
                       d_full=jnp.repeat(m_D[j], M_HEADDIM).reshape(1, d_inner),
                       norm_w=m_norm[j].reshape(1, d_inner), e64=e64, e128=e128, tri=tri)
            z, xbc, dt = _m_in_proj(h, norm_mix[i], m_in_proj[j], d_inner, conv_dim, n_heads)
            y, *m_prompt = _ssd_prompt(z, xbc, dt, prm, B, L, n_heads, j, n_mamba, m_prompt)
            y, *m_sample = _ssd_sample(y, z, xbc, dt, state_ssm, state_conv, prm, TP, DL, n_heads, j, m_sample)
        else:
            rp = dict(mu=r_mu[j], wr=r_wr_b, wk=r_wk_b, wv=r_wv_b, w0=r_w0[j], w1=r_w1[j], w2=r_w2[j],
                      a0=r_a0[j], a1=r_a1[j], a2=r_a2[j], g1=r_g1[j], g2=r_g2[j], k_k=r_k_k[j], k_a=r_k_a[j])
            if j > 0:
                rp.update(v0=r_v0[j - 1], v1=r_v1[j - 1], v2=r_v2[j - 1])
            pshift = jnp.zeros((B, D), F32)
            srows = jnp.zeros((DB, DL, D), F32).at[:, 0].set(state_shift[j]).reshape(TS, D)
            outs = _r_proj(h, norm_mix[i], pshift, srows, rp, v_first if j > 0 else None, L, TP, DL, j)
            streams, u_tail, u_sample = outs[:7], outs[7], outs[8]
            if j == 0:
                v_first = streams[2]
            rk = r_r_k[j].reshape(npair, 1, LANES)
            gnw = r_gn_w[j].reshape(npair, 1, LANES)
            gnb = r_gn_b[j].reshape(npair, 1, LANES)
            y, wkv_p = _wkv_prompt(streams, rk, gnw, gnb, B, L, j, n_rwkv, None if wkv_p is None else (wkv_p,))
            y, wkv_s = _wkv_sample(y, streams, rk, gnw, gnb, state_wkv, TP, DL, j, None if wkv_s is None else (wkv_s,))
            shift_p.append(u_tail[:, 7])
            shift_s.append(u_sample.reshape(DB, DL, D)[:, DL - 1])
        h = _layer_tail(h, y, mix_out[i % 2], j, norm_ffn2[i], ffn2_gu, ffn2_dn, pp, ps, i, norm_ple[i],
                        ple_in_b, ple_gate_b, norm_final, final=(i == depth - 1))

    y_prompt, y_sample = h
    return (y_prompt.reshape(B, L, D), y_sample.reshape(DB, DL, D),
            m_prompt[0], m_prompt[1], wkv_p, jnp.stack(shift_p),
            m_sample[0], m_sample[1], wkv_s, jnp.stack(shift_s))
```

```python
import functools
import math

import jax
import jax.numpy as jnp
from jax import lax
from jax.experimental import pallas as pl
from jax.experimental.pallas import tpu as pltpu

F32 = jnp.float32
BF16 = jnp.bfloat16

NORM_EPS = 1e-6
M_NORM_EPS = 1e-5
R_GN_EPS = 64e-5

LANES = 128
M_HEADDIM = 64
M_STATE = 128
M_GROUPS = 8
M_CHUNK = 128
M_CONV_W = 4
R_HEADDIM = 64
M_PROMPT_CHUNKS_PER_STEP = 2
R_CHUNK = 64
R_PROMPT_CHUNKS_PER_STEP = 4
VMEM_LIMIT_BYTES = 56 * 1024 * 1024


def _cparams(n_axes):
    return pltpu.CompilerParams(dimension_semantics=("arbitrary",) * n_axes,
                                vmem_limit_bytes=VMEM_LIMIT_BYTES)


def _resident(shape):
    zeros = (0,) * len(shape)
    return pl.BlockSpec(shape, lambda *_: zeros, pipeline_mode=pl.Buffered(1))


def _resident_layer(stack, layer):
    index = (layer,) + (0,) * (stack.ndim - 1)
    return pl.BlockSpec((None,) + stack.shape[1:], lambda *_: index, pipeline_mode=pl.Buffered(1))


def _dot(a, b):
    return jnp.dot(a.astype(BF16), b.astype(BF16), preferred_element_type=F32)


def _dot_nt(a, b):
    return lax.dot_general(a.astype(BF16), b.astype(BF16), (((1,), (1,)), ((), ())),
                           preferred_element_type=F32)


def _split3(x):
    hi = x.astype(BF16)
    r1 = x - hi.astype(F32)
    mid = r1.astype(BF16)
    lo = (r1 - mid.astype(F32)).astype(BF16)
    return hi, mid, lo


def _dot_sel_l(sel3, x):
    return jnp.dot(sel3, jnp.concatenate(_split3(x), axis=0), preferred_element_type=F32)


def _dot_sel_r(x, sel3):
    return jnp.dot(jnp.concatenate(_split3(x), axis=1), sel3, preferred_element_type=F32)


def _within_sequence_selectors(rows, seq):
    t = jnp.arange(rows)[:, None]
    s = jnp.arange(rows)[None, :]
    same = (t // seq) == (s // seq)
    incl = jnp.logical_and(same, s <= t)
    return jnp.tile(incl.astype(BF16), (1, 3)), jnp.tile(same.astype(BF16), (1, 3))


def _rms(x, g):
    ms = jnp.mean(x * x, axis=-1, keepdims=True)
    return (x * lax.rsqrt(ms + NORM_EPS)) * g


def _silu(x):
    return x * jax.nn.sigmoid(x)


def _softplus(x):
    return jnp.maximum(x, 0.0) + jnp.log1p(jnp.exp(-jnp.abs(x)))


def _split_rows_specs(tm, width, n_first_blocks, lead=()):
    none = (None,) * len(lead)
    first = pl.BlockSpec(none + (tm, width), lambda i: lead + (jnp.minimum(i, n_first_blocks - 1), 0))
    second = pl.BlockSpec(none + (tm, width), lambda i: lead + (jnp.maximum(i - n_first_blocks, 0), 0))
    return first, second


def _pick_rows(first_ref, second_ref, n_first_blocks):
    return jnp.where(pl.program_id(0) < n_first_blocks, first_ref[...], second_ref[...])


def _ffn_rows(x, g_ref, wgu_ref, wd_ref, ub_ref, fc):
    ub_ref[...] = _rms(x, g_ref[...]).astype(BF16)
    F = wd_ref.shape[0]
    acc = x
    for c in range(F // fc):
        ub = ub_ref[...]
        gate = jnp.dot(ub, wgu_ref[:, c * fc:(c + 1) * fc], preferred_element_type=F32)
        up = jnp.dot(ub, wgu_ref[:, F + c * fc:F + (c + 1) * fc], preferred_element_type=F32)
        act = (_silu(gate) * up).astype(BF16)
        acc = acc + 0.5 * jnp.dot(act, wd_ref[c * fc:(c + 1) * fc, :], preferred_element_type=F32)
    return acc


def _ffn_kernel(*refs, fc, n_first_blocks):
    if n_first_blocks is None:
        h_ref, g_ref, wgu_ref, wd_ref, o_ref, ub_ref = refs
        x = h_ref[...]
    else:
        hp_ref, hs_ref, g_ref, wgu_ref, wd_ref, o_ref, ub_ref = refs
        x = _pick_rows(hp_ref, hs_ref, n_first_blocks)
    o_ref[...] = _ffn_rows(x, g_ref, wgu_ref, wd_ref, ub_ref, fc)


def _ffn(h, gain, w_gate_up, w_down, layer, *, tm=512, fc=256):
    parts = h if isinstance(h, tuple) else (h,)
    T = sum(p.shape[0] for p in parts)
    D = parts[0].shape[1]
    if len(parts) == 2:
        nfb = parts[0].shape[0] // tm
        h_specs = list(_split_rows_specs(tm, D, nfb))
    else:
        nfb = None
        h_specs = [pl.BlockSpec((tm, D), lambda i: (i, 0))]
    return pl.pallas_call(
        functools.partial(_ffn_kernel, fc=fc, n_first_blocks=nfb),
        out_shape=jax.ShapeDtypeStruct((T, D), F32),
        grid=(T // tm,),
        in_specs=h_specs + [_resident((1, D)), _resident_layer(w_gate_up, layer), _resident_layer(w_down, layer)],
        out_specs=pl.BlockSpec((tm, D), lambda i: (i, 0)),
        scratch_shapes=[pltpu.VMEM((tm, D), BF16)],
        compiler_params=_cparams(1),
        name="ffn",
    )(*parts, gain.reshape(1, D), w_gate_up, w_down)


def _tail_kernel(*refs, fc, final, n_first_blocks, y_pairs):
    (h_ref, y_ref, wm_ref, g2_ref, wgu_ref, wd_ref, pp_ref, ps_ref, gp_ref, wi_ref, wg_ref) = refs[:11]
    if final:
        gf_ref, op_ref, os_ref, ub_ref = refs[11:]
    else:
        o_ref, ub_ref = refs[11:]
    y = jnp.concatenate([y_ref[p] for p in range(y_pairs)], axis=1) if y_pairs else y_ref[...]
    x = h_ref[...] + jnp.dot(y, wm_ref[...], preferred_element_type=F32)
    x = _ffn_rows(x, g2_ref, wgu_ref, wd_ref, ub_ref, fc)
    gate = jax.nn.sigmoid(_dot(_rms(x, gp_ref[...]), wg_ref[...]))
    x = x + _dot(_pick_rows(pp_ref, ps_ref, n_first_blocks), wi_ref[...]) * gate
    if not final:
        o_ref[...] = x
        return
    x = _rms(x, gf_ref[...])
    i = pl.program_id(0)

    @pl.when(i < n_first_blocks)
    def _():
        op_ref[...] = x

    @pl.when(i >= n_first_blocks)
    def _():
        os_ref[...] = x


def _layer_tail(h, y, w_mix, mixer_layer, gain_ffn, w_gate_up, w_down, p_first, p_second, layer, gain_ple, w_in,
                w_gate, gain_final, *, final, tm=512, fc=256):
    T, D = h.shape
    P = p_first.shape[2]
    nfb = p_first.shape[1] // tm
    row = pl.BlockSpec((tm, D), lambda i: (i, 0))
    y_pairs = y.shape[0] if y.ndim == 3 else 0
    y_spec = (pl.BlockSpec((y_pairs, tm, LANES), lambda i: (0, i, 0)) if y_pairs
              else pl.BlockSpec((tm, y.shape[1]), lambda i: (i, 0)))
    ops = [h, y, w_mix, gain_ffn.reshape(1, D), w_gate_up, w_down,
           p_first, p_second, gain_ple.reshape(1, D), w_in, w_gate]
    specs = [row, y_spec, _resident_layer(w_mix, mixer_layer), _resident((1, D)),
             _resident_layer(w_gate_up, layer), _resident_layer(w_down, layer),
             *_split_rows_specs(tm, P, nfb, lead=(layer,)), _resident((1, D)),
             _resident_layer(w_in, layer), _resident_layer(w_gate, layer)]
    if final:
        ops.append(gain_final.reshape(1, D))
        specs.append(_resident((1, D)))
        out_shape = (jax.ShapeDtypeStruct((p_first.shape[1], D), F32), jax.ShapeDtypeStruct((p_second.shape[1], D), F32))
        out_specs = _split_rows_specs(tm, D, nfb)
    else:
        out_shape = jax.ShapeDtypeStruct((T, D), F32)
        out_specs = row
    return pl.pallas_call(
        functools.partial(_tail_kernel, fc=fc, final=final, n_first_blocks=nfb, y_pairs=y_pairs),
        out_shape=out_shape,
        grid=(T // tm,),
        in_specs=specs,
        out_specs=out_specs,
        scratch_shapes=[pltpu.VMEM((tm, D), BF16)],
        compiler_params=_cparams(1),
        name="layer_tail",
    )(*ops)


def _m_in_kernel(h_ref, g_ref, wz_ref, wx_ref, wdt_ref, z_ref, xbc_ref, dt_ref):
    u = _rms(h_ref[...], g_ref[...]).astype(BF16)
    z_ref[...] = jnp.dot(u, wz_ref[...], preferred_element_type=F32)
    xbc_ref[...] = jnp.dot(u, wx_ref[...], preferred_element_type=F32)
    dt_ref[...] = jnp.dot(u, wdt_ref[...], preferred_element_type=F32)


def _m_in_proj(h, gain, w_in, d_inner, conv_dim, n_heads, *, tm=256):
    T, D = h.shape
    wz = w_in[:, :d_inner].astype(BF16)
    wx = w_in[:, d_inner:d_inner + conv_dim].astype(BF16)
    wdt = jnp.pad(w_in[:, d_inner + conv_dim:], ((0, 0), (0, LANES - n_heads))).astype(BF16)
    return pl.pallas_call(
        _m_in_kernel,
        out_shape=(jax.ShapeDtypeStruct((T, d_inner), F32), jax.ShapeDtypeStruct((T, conv_dim), F32),
                   jax.ShapeDtypeStruct((T, LANES), F32)),
        grid=(T // tm,),
        in_specs=[pl.BlockSpec((tm, D), lambda i: (i, 0)), _resident((1, D)),
                  _resident((D, d_inner)), _resident((D, conv_dim)), _resident((D, LANES))],
        out_specs=(pl.BlockSpec((tm, d_inner), lambda i: (i, 0)), pl.BlockSpec((tm, conv_dim), lambda i: (i, 0)),
                   pl.BlockSpec((tm, LANES), lambda i: (i, 0))),
        compiler_params=_cparams(1),
        name="m_in_proj",
    )(h, gain.reshape(1, D), wz, wx, wdt)


def _head_expanders(n_heads):
    hh = jnp.arange(LANES)[:, None]
    e64 = (hh == (jnp.arange(n_heads * M_HEADDIM)[None, :] // M_HEADDIM)).astype(BF16)
    e128 = (hh == (jnp.arange(n_heads * LANES)[None, :] // LANES)).astype(BF16)
    return jnp.tile(e64, (3, 1)), jnp.tile(e128, (3, 1))


def _ssd_prompt_kernel(*refs, n_heads, n_aliased):
    (z_ref, xbc_ref, dt_ref, cw_ref, cb_ref, dtb_ref, alog_ref, dsk_ref, nw_ref,
     e64_ref, e128_ref, tri_ref, y_ref, ssm_ref, conv_ref, xpad_ref, s_ref) = refs[n_aliased:]
    c = pl.program_id(1)
    nc = pl.num_programs(1)
    Q = M_CHUNK
    d_inner = n_heads * M_HEADDIM
    gw = d_inner // M_GROUPS
    bc0 = d_inner
    cc0 = d_inner + M_GROUPS * M_STATE

    @pl.when(c == 0)
    def _():
        xpad_ref[:, 0:8, :] = jnp.zeros((xpad_ref.shape[0], 8, LANES), F32)
        s_ref[...] = jnp.zeros(s_ref.shape, F32)

    R = xbc_ref.shape[0]
    slabs = []
    for s in range(xpad_ref.shape[0]):
        lanes = slice(s * LANES, (s + 1) * LANES)
        xpad_ref[s, 8:8 + R, :] = xbc_ref[:, lanes]
        xc = None
        for k in range(M_CONV_W):
            lo = 8 - (M_CONV_W - 1) + k
            term = xpad_ref[s, lo:lo + R, :] * cw_ref[k:k + 1, lanes]
            xc = term if xc is None else xc + term
        slabs.append(_silu(xc + cb_ref[:, lanes]))
        xpad_ref[s, 0:8, :] = xpad_ref[s, R:R + 8, :]
    xs_all = jnp.concatenate(slabs, axis=1)

    ri = lax.broadcasted_iota(jnp.int32, (Q, Q), 0)
    ci = lax.broadcasted_iota(jnp.int32, (Q, Q), 1)
    causal = ri >= ci
    lane = lax.broadcasted_iota(jnp.int32, (Q, LANES), 1)
    first_head = lane < M_HEADDIM
    hpg = n_heads // M_GROUPS

    def chunk_groups(q):
        rows = slice(q * Q, (q + 1) * Q)
        xs = xs_all[rows]
        xh = xs[:, :d_inner]
        dt = _softplus(dt_ref[rows, :] + dtb_ref[...])
        dta = dt * (-jnp.exp(alog_ref[...]))
        acum = _dot_sel_l(tri_ref[...], dta)
        a_full = _dot_sel_r(acum, e64_ref[...])
        dt_full = _dot_sel_r(dt, e64_ref[...])
        a_col = _dot_sel_r(acum, e128_ref[...])
        acum_t = acum.T
        a_last = a_full[Q - 1:Q, :]
        exp_a = jnp.exp(a_full)
        xdt = xh * dt_full
        xw = xdt * jnp.exp(a_last - a_full)
        chunk_decay = jnp.exp(a_last)

        def group(g):
            bg = xs[:, bc0 + g * M_STATE: bc0 + (g + 1) * M_STATE]
            cg = xs[:, cc0 + g * M_STATE: cc0 + (g + 1) * M_STATE]
            cb = _dot_nt(cg, bg)
            sl = slice(g * gw, (g + 1) * gw)
            s_g = s_ref[:, sl]
            y_g = _dot(cg, s_g) * exp_a[:, sl]
            s_ref[:, sl] = s_g * chunk_decay[:, sl] + _dot(bg.T, xw[:, sl])
            yield
            parts = []
            for pr in range(hpg // 2):
                h0 = g * hpg + 2 * pr
                xp = xdt[:, h0 * M_HEADDIM: h0 * M_HEADDIM + LANES]
                ys = []
                for hh in (h0, h0 + 1):
                    seg = a_col[:, hh * LANES:(hh + 1) * LANES] - acum_t[hh:hh + 1, :]
                    m = cb * jnp.exp(jnp.where(causal, seg, -jnp.inf))
                    ys.append(_dot(m, xp))
                parts.append(jnp.where(first_head, ys[0], ys[1]))
                yield
            y_g = y_g + jnp.concatenate(parts, axis=1) + dsk_ref[:, sl] * xh[:, sl]
            y_g = y_g * _silu(z_ref[rows, sl])
            y_g = y_g * lax.rsqrt(jnp.mean(y_g * y_g, axis=-1, keepdims=True) + M_NORM_EPS)
            y_ref[rows, sl] = (y_g * nw_ref[:, sl]).astype(y_ref.dtype)

        return [group(g) for g in range(M_GROUPS)]

    _interleave(gen for q in range(R // Q) for gen in chunk_groups(q))

    @pl.when(c == nc - 1)
    def _():
        ssm_ref[0] = s_ref[...].T.reshape(n_heads, M_HEADDIM, M_STATE)
        conv_ref[0] = xbc_ref[R - (M_CONV_W - 1):R, :]


def _alias_prev(prev, first_out):
    prev = tuple(prev or ())
    specs = [pl.BlockSpec(memory_space=pl.ANY)] * len(prev)
    return prev, specs, {k: first_out + k for k in range(len(prev))}


def _ssd_prompt(z, xbc, dt, prm, B, L, n_heads, layer, n_layers, prev):
    T, d_inner = z.shape
    conv_dim = xbc.shape[1]
    Q = M_CHUNK
    R = M_PROMPT_CHUNKS_PER_STEP * Q
    nc = L // R
    row = lambda b, c: (b * nc + c, 0)
    prev, prev_specs, aliases = _alias_prev(prev, first_out=1)
    kern = functools.partial(_ssd_prompt_kernel, n_heads=n_heads, n_aliased=len(prev))
    return pl.pallas_call(
        kern,
        out_shape=(jax.ShapeDtypeStruct((T, d_inner), BF16),
                   jax.ShapeDtypeStruct((n_layers, B, n_heads, M_HEADDIM, M_STATE), F32),
                   jax.ShapeDtypeStruct((n_layers, B, M_CONV_W - 1, conv_dim), F32)),
        grid=(B, nc),
        in_specs=prev_specs + [
            pl.BlockSpec((R, d_inner), row), pl.BlockSpec((R, conv_dim), row), pl.BlockSpec((R, LANES), row),
            _resident((M_CONV_W, conv_dim)), _resident((1, conv_dim)),
            _resident((1, LANES)), _resident((1, LANES)), _resident((1, d_inner)), _resident((1, d_inner)),
            _resident((3 * LANES, d_inner)), _resident((3 * LANES, n_heads * LANES)), _resident((Q, 3 * Q))],
        out_specs=(pl.BlockSpec((R, d_inner), row),
                   pl.BlockSpec((None, 1, n_heads, M_HEADDIM, M_STATE), lambda b, c: (layer, b, 0, 0, 0)),
                   pl.BlockSpec((None, 1, M_CONV_W - 1, conv_dim), lambda b, c: (layer, b, 0, 0))),
        scratch_shapes=[pltpu.VMEM((conv_dim // LANES, R + 8, LANES), F32), pltpu.VMEM((M_STATE, d_inner), F32)],
        input_output_aliases=aliases,
        compiler_params=_cparams(2),
        name="ssd_prompt",
    )(*prev, z, xbc, dt, prm["conv_w"], prm["conv_b"], prm["dt_bias"], prm["a_log"], prm["d_full"], prm["norm_w"],
      prm["e64"], prm["e128"], prm["tri"])


def _ssd_sample_kernel(*refs, n_heads, seq, nb, n_aliased):
    (z_ref, xbc_ref, dt_ref, s0_ref, c0_ref, cw_ref, cb_ref, dtb_ref, alog_ref,
     dsk_ref, nw_ref, e64_ref, e128_ref, rowsel_ref, y_ref, ssm_ref, conv_ref, hist_ref) = refs[n_aliased:]
    d_inner = n_heads * M_HEADDIM
    gw = d_inner // M_GROUPS
    bc0 = d_inner
    cc0 = d_inner + M_GROUPS * M_STATE
    hpg = n_heads // M_GROUPS
    rows = nb * seq
    row = lax.broadcasted_iota(jnp.int32, (rows, 1), 0)
    tok = jnp.bitwise_and(row, seq - 1)
    seq_of_row = jnp.right_shift(row, int(math.log2(seq)))
    seq_of_col = jnp.right_shift(lax.broadcasted_iota(jnp.int32, (1, rows), 1), int(math.log2(seq)))
    eye = (lax.broadcasted_iota(jnp.int32, (LANES, LANES), 0)
           == lax.broadcasted_iota(jnp.int32, (LANES, LANES), 1)).astype(BF16)
    top_head = lax.broadcasted_iota(jnp.int32, (LANES, 1), 0) < M_HEADDIM

    @pl.when(pl.program_id(0) == 0)
    def _():
        hist_ref[...] = jnp.zeros(hist_ref.shape, F32)

    x = xbc_ref[...]
    for s in range(1, M_CONV_W):
        for bi in range(nb):
            hist_ref[s - 1, bi * seq:bi * seq + s, :] = c0_ref[bi, M_CONV_W - 1 - s:M_CONV_W - 1, :]
    xc = None
    for k in range(M_CONV_W):
        s = M_CONV_W - 1 - k
        tap = x if s == 0 else jnp.where(tok >= s, pltpu.roll(x, s, 0), hist_ref[s - 1])
        term = tap * cw_ref[k:k + 1, :]
        xc = term if xc is None else xc + term
    xs = _silu(xc + cb_ref[...])
    xh = xs[:, :d_inner]
    for bi in range(nb):
        conv_ref[bi] = xbc_ref[(bi + 1) * seq - (M_CONV_W - 1):(bi + 1) * seq, :]

    dt = _softplus(dt_ref[...] + dtb_ref[...])
    acum = dt * (-jnp.exp(alog_ref[...]))
    s = 1
    while s < seq:
        acum = acum + jnp.where(tok >= s, pltpu.roll(acum, s, 0), 0.0)
        s *= 2
    a_full = _dot_sel_r(acum, e64_ref[...])
    dt_full = _dot_sel_r(dt, e64_ref[...])
    a_col = _dot_sel_r(acum, e128_ref[...])
    exp_a = jnp.exp(a_full)
    xdt = xh * dt_full
    packed = jnp.concatenate([a_full, xdt, xs[:, bc0:cc0]], axis=1)
    from_tok = [_dot_sel_l(rowsel_ref[j], packed) for j in range(seq)]
    aux = xdt * jnp.exp(from_tok[seq - 1][:, :d_inner] - a_full)

    y = dsk_ref[...] * xh
    for j in range(seq):
        a_j = from_tok[j][:, :d_inner]
        wj = jnp.where(tok >= j, jnp.exp(a_full - a_j), 0.0) * from_tok[j][:, d_inner:2 * d_inner]
        parts = []
        for g in range(M_GROUPS):
            cgb = (xs[:, cc0 + g * M_STATE: cc0 + (g + 1) * M_STATE]
                   * from_tok[j][:, 2 * d_inner + g * M_STATE: 2 * d_inner + (g + 1) * M_STATE])
            parts.append(jnp.sum(cgb, axis=-1, keepdims=True) * wj[:, g * gw:(g + 1) * gw])
        y = y + jnp.concatenate(parts, axis=1)

    inter = [None] * (n_heads // 2)

    def head_pair(pr):
        g = (2 * pr) // hpg
        cg = xs[:, cc0 + g * M_STATE: cc0 + (g + 1) * M_STATE]
        bg = xs[:, bc0 + g * M_STATE: bc0 + (g + 1) * M_STATE]
        aux_t = _dot_nt(eye, aux[:, pr * LANES:(pr + 1) * LANES])
        h0 = [s0_ref[bi, 2 * pr:2 * pr + 2].reshape(2 * M_HEADDIM, M_STATE) for bi in range(nb)]
        reads = [_dot_nt(cg, h) for h in h0]
        yield
        acc = None
        for bi in range(nb):
            t = jnp.where(seq_of_row == bi, reads[bi], 0.0)
            acc = t if acc is None else acc + t
        inter[pr] = acc
        upd = [_dot(jnp.where(seq_of_col == bi, aux_t, 0.0), bg) for bi in range(nb)]
        yield
        for bi in range(nb):
            last = (bi + 1) * seq - 1
            dec = jnp.where(top_head, jnp.exp(a_col[last:last + 1, (2 * pr) * LANES:(2 * pr + 1) * LANES]),
                            jnp.exp(a_col[last:last + 1, (2 * pr + 1) * LANES:(2 * pr + 2) * LANES]))
            ssm_ref[bi, 2 * pr:2 * pr + 2] = (h0[bi] * dec + upd[bi]).reshape(2, M_HEADDIM, M_STATE)

    _interleave(head_pair(pr) for pr in range(n_heads // 2))
    y = y + jnp.concatenate(inter, axis=1) * exp_a

    y = y * _silu(z_ref[...])
    outs = []
    for g in range(M_GROUPS):
        yg = y[:, g * gw:(g + 1) * gw]
        outs.append(yg * lax.rsqrt(jnp.mean(yg * yg, axis=-1, keepdims=True) + M_NORM_EPS))
    y_ref[...] = (jnp.concatenate(outs, axis=1) * nw_ref[...]).astype(y_ref.dtype)


def _ssd_sample(y_all, z, xbc, dt, s0, c0, prm, row0, seq, n_heads, layer, prev, *, nb=4):
    T, d_inner = z.shape
    conv_dim = xbc.shape[1]
    DB = s0.shape[1]
    assert seq & (seq - 1) == 0 and seq >= M_CONV_W - 1
    rows = nb * seq
    r = jnp.arange(rows)
    rowsel = jnp.stack([jnp.tile((r[None, :] == ((r // seq) * seq + j)[:, None]).astype(BF16), (1, 3))
                        for j in range(seq)])
    blk0 = row0 // rows
    row = lambda i: (blk0 + i, 0)
    s_spec = pl.BlockSpec((None, nb, n_heads, M_HEADDIM, M_STATE), lambda i: (layer, i, 0, 0, 0))
    c_spec = pl.BlockSpec((None, nb, M_CONV_W - 1, conv_dim), lambda i: (layer, i, 0, 0))
    aliased, alias_specs, aliases = _alias_prev((y_all,) + tuple(prev or ()), first_out=0)
    kern = functools.partial(_ssd_sample_kernel, n_heads=n_heads, seq=seq, nb=nb, n_aliased=len(aliased))
    return pl.pallas_call(
        kern,
        out_shape=(jax.ShapeDtypeStruct((T, d_inner), y_all.dtype),
                   jax.ShapeDtypeStruct(s0.shape, F32), jax.ShapeDtypeStruct(c0.shape, F32)),
        grid=(DB // nb,),
        in_specs=alias_specs + [
            pl.BlockSpec((rows, d_inner), row), pl.BlockSpec((rows, conv_dim), row),
            pl.BlockSpec((rows, LANES), row), s_spec, c_spec,
            _resident((M_CONV_W, conv_dim)), _resident((1, conv_dim)), _resident((1, LANES)),
            _resident((1, LANES)), _resident((1, d_inner)), _resident((1, d_inner)),
            _resident((3 * LANES, d_inner)), _resident((3 * LANES, n_heads * LANES)),
            _resident((seq, rows, 3 * rows))],
        out_specs=(pl.BlockSpec((rows, d_inner), row), s_spec, c_spec),
        scratch_shapes=[pltpu.VMEM((M_CONV_W - 1, rows, conv_dim), F32)],
        input_output_aliases=aliases,
        compiler_params=_cparams(1),
        name="ssd_sample",
    )(*aliased, z, xbc, dt, s0, c0, prm["conv_w"], prm["conv_b"], prm["dt_bias"], prm["a_log"], prm["d_full"],
      prm["norm_w"], prm["e64"], prm["e128"], rowsel)


def _r_proj_kernel(*refs, tm, blocks_per_seq, n_prompt_blocks, sample_seq, has_vres):
    if has_vres:
        (h_ref, g_ref, pshift_ref, srows_ref, mu_ref, wr_ref, wk_ref, wv_ref, w1_ref, w2_ref, a1_ref, a2_ref,
         g1_ref, g2_ref, w0_ref, a0_ref, kk_ref, ka_ref, vf_ref, v0_ref, v1_ref, v2_ref,
         r_o, k_o, v_o, kkr_o, a_o, w_o, g_o, up_o, us_o, sc_ref, xx_ref) = refs
    else:
        (h_ref, g_ref, pshift_ref, srows_ref, mu_ref, wr_ref, wk_ref, wv_ref, w1_ref, w2_ref, a1_ref, a2_ref,
         g1_ref, g2_ref, w0_ref, a0_ref, kk_ref, ka_ref,
         r_o, k_o, v_o, kkr_o, a_o, w_o, g_o, up_o, us_o, sc_ref, xx_ref) = refs
    i = pl.program_id(0)
    u = _rms(h_ref[...], g_ref[...])
    sc_ref[8:8 + tm, :] = u

    @pl.when(jnp.logical_and(i % blocks_per_seq == blocks_per_seq - 1, i < n_prompt_blocks))
    def _():
        up_o[0] = u[tm - 8:tm, :]

    @pl.when(i >= n_prompt_blocks)
    def _():
        us_o[...] = u

    @pl.when(jnp.logical_and(i % blocks_per_seq == 0, i < n_prompt_blocks))
    def _():
        sc_ref[7:8, :] = pshift_ref[pl.ds(i // blocks_per_seq, 1), :]

    prev = sc_ref[7:7 + tm, :]
    row = lax.broadcasted_iota(jnp.int32, (tm, 1), 0)
    use_state = jnp.logical_and(i >= n_prompt_blocks, jnp.bitwise_and(row, sample_seq - 1) == 0)
    prev = jnp.where(use_state, srows_ref[...], prev)
    sc_ref[7:8, :] = sc_ref[7 + tm:8 + tm, :]

    xx_ref[...] = prev - u
    npair = r_o.shape[0]

    def rows(r0, n):
        u_h = sc_ref[8 + r0:8 + r0 + n, :]
        xx_h = xx_ref[r0:r0 + n, :]

        def mix(m):
            return u_h + xx_h * mu_ref[m:m + 1, :]

        xv = mix(3)
        r = _dot(mix(0), wr_ref[...])
        k = _dot(mix(2), wk_ref[...])
        v = _dot(xv, wv_ref[...])
        yield
        lw = _dot(mix(1), w1_ref[...])
        la = _dot(mix(4), a1_ref[...])
        lg = _dot(mix(5), g1_ref[...])
        lv = _dot(xv, v1_ref[...]) if has_vres else None
        yield
        w = -_softplus(-(w0_ref[...] + _dot(jnp.tanh(lw), w2_ref[...]))) - 0.5
        a = jax.nn.sigmoid(a0_ref[...] + _dot(la, a2_ref[...]))
        g = _dot(jax.nn.sigmoid(lg), g2_ref[...])
        if has_vres:
            vf = jnp.concatenate([vf_ref[p, r0:r0 + n, :] for p in range(npair)], axis=1)
            v = v + (vf - v) * jax.nn.sigmoid(v0_ref[...] + _dot(lv, v2_ref[...]))
        yield
        kkr = k * kk_ref[...]
        k = k * (1.0 + (a - 1.0) * ka_ref[...])
        for p in range(npair):
            sl = slice(p * LANES, (p + 1) * LANES)
            r_o[p, r0:r0 + n, :] = r[:, sl]
            k_o[p, r0:r0 + n, :] = k[:, sl]
            v_o[p, r0:r0 + n, :] = v[:, sl]
            kkr_o[p, r0:r0 + n, :] = kkr[:, sl]
            a_o[p, r0:r0 + n, :] = a[:, sl]
            w_o[p, r0:r0 + n, :] = w[:, sl]
            g_o[p, r0:r0 + n, :] = g[:, sl]

    n_groups = 2
    _interleave((rows(q * (tm // n_groups), tm // n_groups) for q in range(n_groups)))


def _pad_cols(w, n):
    return jnp.pad(w, ((0, 0), (0, n - w.shape[1])))


def _pad_rows(w, n):
    return jnp.pad(w, ((0, n - w.shape[0]), (0, 0)))


def _round_up(n, m):
    return -(-n // m) * m


def _r_proj(h, gain, pshift, srows, rp, v_first, L, n_prompt_rows, sample_seq, layer, *, tm=256):
    T, D = h.shape
    npair = D // LANES
    has_vres = v_first is not None
    row2 = lambda i: (i, 0)
    npb = n_prompt_rows // tm
    lw = _round_up(rp["w1"].shape[1], LANES)
    la = _round_up(rp["a1"].shape[1], LANES)
    lg = _round_up(rp["g1"].shape[1], LANES)
    ops = [h, gain.reshape(1, D), pshift, srows, rp["mu"],
           rp["wr"], rp["wk"], rp["wv"],
           _pad_cols(rp["w1"], lw).astype(BF16), _pad_rows(rp["w2"], lw).astype(BF16),
           _pad_cols(rp["a1"], la).astype(BF16), _pad_rows(rp["a2"], la).astype(BF16),
           _pad_cols(rp["g1"], lg).astype(BF16), _pad_rows(rp["g2"], lg).astype(BF16),
           rp["w0"].reshape(1, D), rp["a0"].reshape(1, D), rp["k_k"].reshape(1, D), rp["k_a"].reshape(1, D)]
    specs = [pl.BlockSpec((tm, D), row2), _resident((1, D)), _resident(pshift.shape),
             pl.BlockSpec((tm, D), lambda i: (jnp.maximum(i - npb, 0), 0)), _resident((6, D)),
             _resident_layer(rp["wr"], layer), _resident_layer(rp["wk"], layer), _resident_layer(rp["wv"], layer),
             _resident((D, lw)), _resident((lw, D)), _resident((D, la)), _resident((la, D)),
             _resident((D, lg)), _resident((lg, D)),
             _resident((1, D)), _resident((1, D)), _resident((1, D)), _resident((1, D))]
    pair_spec = pl.BlockSpec((npair, tm, LANES), lambda i: (0, i, 0))
    if has_vres:
        lv = _round_up(rp["v1"].shape[1], LANES)
        ops += [v_first, rp["v0"].reshape(1, D), _pad_cols(rp["v1"], lv).astype(BF16),
                _pad_rows(rp["v2"], lv).astype(BF16)]
        specs += [pair_spec, _resident((1, D)), _resident((D, lv)), _resident((lv, D))]
    pair_shape = jax.ShapeDtypeStruct((npair, T, LANES), F32)
    bps = L // tm
    n_seq = n_prompt_rows // L
    kern = functools.partial(_r_proj_kernel, tm=tm, blocks_per_seq=bps, n_prompt_blocks=npb,
                             sample_seq=sample_seq, has_vres=has_vres)
    return pl.pallas_call(
        kern,
        out_shape=(pair_shape,) * 7 + (jax.ShapeDtypeStruct((n_seq, 8, D), F32),
                                       jax.ShapeDtypeStruct((T - n_prompt_rows, D), F32)),
        grid=(T // tm,),
        in_specs=specs,
        out_specs=(pair_spec,) * 7 + (
            pl.BlockSpec((1, 8, D), lambda i: (jnp.minimum(i // bps, n_seq - 1), 0, 0)),
            pl.BlockSpec((tm, D), lambda i: (jnp.maximum(i - npb, 0), 0))),
        scratch_shapes=[pltpu.VMEM((tm + 8, D), F32), pltpu.VMEM((tm, D), F32)],
        compiler_params=_cparams(1),
        name="r_proj",
    )(*ops)


def _wkv_pair(r, k, v, kkr, a, w, g, rk, gnw, gnb, incl3, same3, states, lseq):
    C = r.shape[0]
    ns = C // lseq
    lane = lax.broadcasted_iota(jnp.int32, (1, LANES), 1)
    m0 = lane < R_HEADDIM

    def headsum(x):
        s0 = jnp.sum(jnp.where(m0, x, 0.0), axis=-1, keepdims=True)
        s1 = jnp.sum(jnp.where(m0, 0.0, x), axis=-1, keepdims=True)
        return jnp.where(m0, s0, s1)

    kk = kkr * lax.rsqrt(jnp.maximum(headsum(kkr * kkr), 1e-24))
    alpha = -kk
    beta = kk * a
    logd = -jnp.exp(w)

    ti = lax.broadcasted_iota(jnp.int32, (C, C), 0)
    si = lax.broadcasted_iota(jnp.int32, (C, C), 1)
    sh = int(math.log2(lseq))
    same = jnp.right_shift(ti, sh) == jnp.right_shift(si, sh)
    incl = jnp.logical_and(same, si <= ti)
    strict = jnp.logical_and(same, si < ti)
    lg = _dot_sel_l(incl3, logd)
    ltot = _dot_sel_l(same3, logd)
    yield
    g_in = jnp.exp(lg)
    ginv = jnp.exp(-lg)
    gend = jnp.exp(ltot - lg)
    w0 = alpha * jnp.exp(lg - logd)
    rg = r * g_in
    bi = beta * ginv
    ki = k * ginv
    wr = jnp.concatenate([w0, rg], axis=0)
    rsel = jnp.right_shift(jnp.bitwise_and(lax.broadcasted_iota(jnp.int32, (2 * C, 1), 0), C - 1), sh)

    xy0 = None
    for s in range(ns):
        t = _dot_nt(wr, states[s])
        if ns > 1:
            t = jnp.where(rsel == s, t, 0.0)
        xy0 = t if xy0 is None else xy0 + t
    x0 = xy0[:C]
    y0 = xy0[C:]
    yield

    eye = (ti == si).astype(F32)
    heads = (m0, jnp.logical_not(m0))
    wrm = [jnp.where(mh, wr, 0.0) for mh in heads]
    gb = [_dot_nt(x, bi) for x in wrm]
    gk = [_dot_nt(x, ki) for x in wrm]
    yield
    a_ab = [jnp.where(strict, x[:C], 0.0) for x in gb]
    a_ak = [jnp.where(strict, x[:C], 0.0) for x in gk]
    b_rb = [jnp.where(incl, x[C:], 0.0) for x in gb]
    b_rk = [jnp.where(incl, x[C:], 0.0) for x in gk]
    tinv = [eye + x for x in a_ab]
    pw = a_ab
    for _ in range(int(math.log2(lseq)) - 1):
        pw = [_dot(x, x) for x in pw]
        yield
        tinv = [t + _dot(t, x) for t, x in zip(tinv, pw)]
    xh = [x0 + _dot(x, v) for x in a_ak]
    yv = [_dot(x, v) for x in b_rk]
    yield
    uh = [_dot(t, x) for t, x in zip(tinv, xh)]
    yield
    yh = [y0 + _dot(b, x) + c for b, x, c in zip(b_rb, uh, yv)]
    u = jnp.where(m0, uh[0], uh[1])
    y = jnp.where(m0, yh[0], yh[1])
    yield

    uvt = jnp.concatenate([u, v], axis=0).T
    bkd = jnp.concatenate([beta * gend, k * gend], axis=0)
    bi_r = lax.broadcasted_iota(jnp.int32, (LANES, LANES), 0) >= R_HEADDIM
    bi_c = lax.broadcasted_iota(jnp.int32, (LANES, LANES), 1) >= R_HEADDIM
    same_head = bi_r == bi_c
    new_states = []
    for s in range(ns):
        rhs = bkd if ns == 1 else jnp.where(rsel == s, bkd, 0.0)
        ds = jnp.where(same_head, _dot(uvt, rhs), 0.0)
        new_states.append(states[s] * jnp.exp(ltot[s * lseq:s * lseq + 1, :]) + ds)
    yield

    mean = headsum(y) * (1.0 / R_HEADDIM)
    d = y - mean
    var = headsum(d * d) * (1.0 / R_HEADDIM)
    yn = d * lax.rsqrt(var + R_GN_EPS) * gnw + gnb
    out = (yn + headsum(r * k * rk) * v) * g
    return out, new_states


def _wkv_pair_chunk(r, k, v, kkr, a, w, g, rk, gnw, gnb, incl3, state_ref, y_ref):
    C = r.shape[0]
    H = 2 * C
    lane = lax.broadcasted_iota(jnp.int32, (1, LANES), 1)
    m0 = lane < R_HEADDIM

    def headsum(x):
        s0 = jnp.sum(jnp.where(m0, x, 0.0), axis=-1, keepdims=True)
        s1 = jnp.sum(jnp.where(m0, 0.0, x), axis=-1, keepdims=True)
        return jnp.where(m0, s0, s1)

    def stack(x):
        return jnp.concatenate([jnp.where(m0, x, 0.0), jnp.where(m0, 0.0, x)], axis=0)

    kk = kkr * lax.rsqrt(jnp.maximum(headsum(kkr * kkr), 1e-24))
    beta = kk * a
    logd = -jnp.exp(w)
    lg = _dot_sel_l(incl3, logd)
    yield
    ltot = lg[C - 1:C, :]
    ginv = jnp.exp(-lg)
    gend = jnp.exp(ltot - lg)
    w0s = stack(-kk * jnp.exp(lg - logd))
    rgs = stack(r * jnp.exp(lg))
    vs = stack(v)
    bds = stack(beta * gend)
    kds = stack(k * gend)
    gram = _dot_nt(jnp.concatenate([w0s, rgs], axis=0),
                   jnp.concatenate([stack(beta * ginv), stack(k * ginv)], axis=0))
    yield
    tr = jnp.bitwise_and(lax.broadcasted_iota(jnp.int32, (H, H), 0), C - 1)
    sc = jnp.bitwise_and(lax.broadcasted_iota(jnp.int32, (H, H), 1), C - 1)
    strict = sc < tr
    incl = sc <= tr
    a_ab = jnp.where(strict, gram[:H, :H], 0.0)
    a_ak = jnp.where(strict, gram[:H, H:], 0.0)
    b_rb = jnp.where(incl, gram[H:, :H], 0.0)
    b_rk = jnp.where(incl, gram[H:, H:], 0.0)
    eye = (lax.broadcasted_iota(jnp.int32, (H, H), 0) == lax.broadcasted_iota(jnp.int32, (H, H), 1)).astype(F32)
    tinv = eye + a_ab
    pw = _dot(a_ab, a_ab)
    av = _dot(a_ak, vs)
    bv = _dot(b_rk, vs)
    yield
    for _ in range(int(math.log2(C)) - 2):
        tinv = tinv + _dot(tinv, pw)
        pw = _dot(pw, pw)
        yield
    tinv = tinv + _dot(tinv, pw)
    yield
    tw = _dot(tinv, jnp.concatenate([w0s, av], axis=1))
    yield
    br = _dot(b_rb, tw)
    m_t = _dot(tw[:, :LANES].T, bds)
    z = _dot(jnp.concatenate([tw[:, LANES:], vs], axis=0).T, jnp.concatenate([bds, kds], axis=0))
    yield
    r_t = rgs + br[:, :LANES]
    y_t = br[:, LANES:] + bv
    state = state_ref[...]
    ys = _dot_nt(r_t, state) + y_t
    state_ref[...] = state * jnp.exp(ltot) + _dot(state, m_t) + z
    yield
    y = ys[:C] + ys[C:]
    mean = headsum(y) * (1.0 / R_HEADDIM)
    d = y - mean
    var = headsum(d * d) * (1.0 / R_HEADDIM)
    yn = d * lax.rsqrt(var + R_GN_EPS) * gnw + gnb
    y_ref[...] = ((yn + headsum(r * k * rk) * v) * g).astype(y_ref.dtype)


def _interleave(generators):
    live = list(generators)
    while live:
        nxt = []
        for gen in live:
            try:
                next(gen)
                nxt.append(gen)
            except StopIteration:
                pass
        live = nxt


def _wkv_prompt_kernel(*refs, steps_per_seq, n_aliased):
    (r_ref, k_ref, v_ref, kkr_ref, a_ref, w_ref, g_ref, rk_ref, gnw_ref, gnb_ref, incl_ref,
     y_ref, s_out_ref, s_ref) = refs[n_aliased:]
    c = pl.program_id(0) % steps_per_seq
    npair = r_ref.shape[0]
    C = R_CHUNK

    @pl.when(c == 0)
    def _():
        s_ref[...] = jnp.zeros(s_ref.shape, F32)

    def chunk(p, q):
        rows = slice(q * C, (q + 1) * C)
        return _wkv_pair_chunk(r_ref[p, rows], k_ref[p, rows], v_ref[p, rows], kkr_ref[p, rows], a_ref[p, rows],
                               w_ref[p, rows], g_ref[p, rows], rk_ref[p], gnw_ref[p], gnb_ref[p], incl_ref[...],
                               s_ref.at[p], y_ref.at[p, pl.ds(q * C, C)])

    _interleave(chunk(p, q) for q in range(r_ref.shape[1] // C) for p in range(npair))

    @pl.when(c == steps_per_seq - 1)
    def _():
        for p in range(npair):
            s = s_ref[p]
            s_out_ref[0, 2 * p:2 * p + 2] = (s[:, :R_HEADDIM] + s[:, R_HEADDIM:]).reshape(2, R_HEADDIM, R_HEADDIM)


def _wkv_prompt(streams, rk, gnw, gnb, B, L, layer, n_layers, prev):
    npair, T, _ = streams[0].shape
    C = R_CHUNK
    rows = R_PROMPT_CHUNKS_PER_STEP * C
    cps = L // rows
    blk = pl.BlockSpec((npair, rows, LANES), lambda i: (0, i, 0))
    vec = _resident((npair, 1, LANES))
    prev, prev_specs, aliases = _alias_prev(prev, first_out=1)
    return pl.pallas_call(
        functools.partial(_wkv_prompt_kernel, steps_per_seq=cps, n_aliased=len(prev)),
        out_shape=(jax.ShapeDtypeStruct((npair, T, LANES), BF16),
                   jax.ShapeDtypeStruct((n_layers, B, 2 * npair, R_HEADDIM, R_HEADDIM), F32)),
        grid=(B * cps,),
        in_specs=prev_specs + [blk] * 7 + [vec] * 3 + [_resident((C, 3 * C))],
        out_specs=(blk, pl.BlockSpec((None, 1, 2 * npair, R_HEADDIM, R_HEADDIM),
                                     lambda i: (layer, i // cps, 0, 0, 0))),
        scratch_shapes=[pltpu.VMEM((npair, LANES, LANES), F32)],
        input_output_aliases=aliases,
        compiler_params=_cparams(1),
        name="wkv_prompt",
    )(*prev, *streams, rk, gnw, gnb, _within_sequence_selectors(C, C)[0])


def _wkv_sample_kernel(*refs, seq, n_aliased):
    (r_ref, k_ref, v_ref, kkr_ref, a_ref, w_ref, g_ref, rk_ref, gnw_ref, gnb_ref, incl_ref, same_ref,
     s0_ref, y_ref, s_out_ref) = refs[n_aliased:]
    npair = r_ref.shape[0]
    ns = R_CHUNK // seq
    lane = lax.broadcasted_iota(jnp.int32, (LANES, LANES), 1) >= R_HEADDIM
    rowh = lax.broadcasted_iota(jnp.int32, (LANES, LANES), 0) >= R_HEADDIM
    same_head = lane == rowh

    def pair(p):
        states = []
        for s in range(ns):
            s2 = s0_ref[s, 2 * p:2 * p + 2].reshape(LANES, R_HEADDIM)
            states.append(jnp.where(same_head, jnp.concatenate([s2, s2], axis=1), 0.0))
        out, new = yield from _wkv_pair(r_ref[p], k_ref[p], v_ref[p], kkr_ref[p], a_ref[p], w_ref[p], g_ref[p],
                                        rk_ref[p], gnw_ref[p], gnb_ref[p], incl_ref[...], same_ref[...],
                                        states, seq)
        y_ref[p] = out.astype(y_ref.dtype)
        for s in range(ns):
            s_out_ref[s, 2 * p:2 * p + 2] = (new[s][:, :R_HEADDIM] + new[s][:, R_HEADDIM:]).reshape(
                2, R_HEADDIM, R_HEADDIM)

    _interleave(pair(p) for p in range(npair))


def _wkv_sample(y_all, streams, rk, gnw, gnb, s0, row0, seq, layer, prev):
    npair, T, _ = streams[0].shape
    C = R_CHUNK
    ns = C // seq
    DB = s0.shape[1]
    blk0 = row0 // C
    blk = pl.BlockSpec((npair, C, LANES), lambda i: (0, blk0 + i, 0))
    vec = _resident((npair, 1, LANES))
    sblk = pl.BlockSpec((None, ns, 2 * npair, R_HEADDIM, R_HEADDIM), lambda i: (layer, i, 0, 0, 0))
    aliased, alias_specs, aliases = _alias_prev((y_all,) + tuple(prev or ()), first_out=0)
    return pl.pallas_call(
        functools.partial(_wkv_sample_kernel, seq=seq, n_aliased=len(aliased)),
        out_shape=(jax.ShapeDtypeStruct((npair, T, LANES), BF16), jax.ShapeDtypeStruct(s0.shape, F32)),
        grid=(DB // ns,),
        in_specs=alias_specs + [blk] * 7 + [vec] * 3 + [_resident((C, 3 * C))] * 2 + [sblk],
        out_specs=(blk, sblk),
        input_output_aliases=aliases,
        compiler_params=_cparams(1),
        name="wkv_sample",
    )(*aliased, *streams, rk, gnw, gnb, *_within_sequence_selectors(C, seq), s0)


def kernel(x_prompt, x_sample, p_prompt, p_sample, state_ssm, state_conv, state_wkv, state_shift, norm_ffn1, ffn1_gate_up, ffn1_down, norm_mix, norm_ffn2, ffn2_gate_up, ffn2_down, norm_ple, ple_in, ple_gate, norm_final, m_in_proj, m_conv_w, m_conv_b, m_dt_bias, m_A_log, m_D, m_norm, m_out_proj, r_mu, r_wr, r_wk, r_wv, r_wo, r_w0, r_w1, r_w2, r_a0, r_a1, r_a2, r_g1, r_g2, r_k_k, r_k_a, r_r_k, r_gn_w, r_gn_b, r_v0, r_v1, r_v2):
    B, L, D = x_prompt.shape
    DB, DL, _ = x_sample.shape
    depth = norm_ffn1.shape[0]
    TP, TS = B * L, DB * DL
    n_heads = m_A_log.shape[1]
    d_inner = n_heads * M_HEADDIM
    conv_dim = m_conv_w.shape[2]
    npair = D // LANES

    h = (x_prompt.reshape(TP, D), x_sample.reshape(TS, D))
    pp = p_prompt.reshape(depth, TP, -1)
    ps = p_sample.reshape(depth, TS, -1)
    e64, e128 = _head_expanders(n_heads)
    n_mamba = state_ssm.shape[0]
    n_rwkv = state_wkv.shape[0]

    tri = _within_sequence_selectors(M_CHUNK, M_CHUNK)[0]
    ffn1_gu, ffn1_dn = ffn1_gate_up.astype(BF16), ffn1_down.astype(BF16)
    ffn2_gu, ffn2_dn = ffn2_gate_up.astype(BF16), ffn2_down.astype(BF16)
    ple_in_b, ple_gate_b = ple_in.astype(BF16), ple_gate.astype(BF16)
    mix_out = (m_out_proj.astype(BF16), r_wo.astype(BF16))
    r_wr_b, r_wk_b, r_wv_b = r_wr.astype(BF16), r_wk.astype(BF16), r_wv.astype(BF16)

    m_prompt = m_sample = wkv_p = wkv_s = None
    shift_p, shift_s = [], []
    v_first = None
    for i in range(depth):
        j = i // 2
        h = _ffn(h, norm_ffn1[i], ffn1_gu, ffn1_dn, i)
        if i % 2 == 0:
            prm = dict(conv_w=m_conv_w[j], conv_b=m_conv_b[j].reshape(1, conv_dim),
                       dt_bias=jnp.pad(m_dt_bias[j], (0, LANES - n_heads)).reshape(1, LANES),
                       a_log=jnp.pad(m_A_log[j], (0, LANES - n_heads)).reshape(1, LANES),
                       d_full=jnp.repeat(m_D[j], M_HEADDIM).reshape(1, d_inner),
                       norm_w=m_norm[j].reshape(1, d_inner), e64=e64, e128=e128, tri=tri)
            z, xbc, dt = _m_in_proj(h, norm_mix[i], m_in_proj[j], d_inner, conv_dim, n_heads)
            y, *m_prompt = _ssd_prompt(z, xbc, dt, prm, B, L, n_heads, j, n_mamba, m_prompt)
            y, *m_sample = _ssd_sample(y, z, xbc, dt, state_ssm, state_conv, prm, TP, DL, n_heads, j, m_sample)
        else:
            rp = dict(mu=r_mu[j], wr=r_wr_b, wk=r_wk_b, wv=r_wv_b, w0=r_w0[j], w1=r_w1[j], w2=r_w2[j],
                      a0=r_a0[j], a1=r_a1[j], a2=r_a2[j], g1=r_g1[j], g2=r_g2[j], k_k=r_k_k[j], k_a=r_k_a[j])
            if j > 0:
                rp.update(v0=r_v0[j - 1], v1=r_v1[j - 1], v2=r_v2[j - 1])
            pshift = jnp.zeros((B, D), F32)
            srows = jnp.zeros((DB, DL, D), F32).at[:, 0].set(state_shift[j]).reshape(TS, D)
            outs = _r_proj(h, norm_mix[i], pshift, srows, rp, v_first if j > 0 else None, L, TP, DL, j)
            streams, u_tail, u_sample = outs[:7], outs[7], outs[8]
            if j == 0:
                v_first = streams[2]
            rk = r_r_k[j].reshape(npair, 1, LANES)
            gnw = r_gn_w[j].reshape(npair, 1, LANES)
            gnb = r_gn_b[j].reshape(npair, 1, LANES)
            y, wkv_p = _wkv_prompt(streams, rk, gnw, gnb, B, L, j, n_rwkv, None if wkv_p is None else (wkv_p,))
            y, wkv_s = _wkv_sample(y, streams, rk, gnw, gnb, state_wkv, TP, DL, j, None if wkv_s is None else (wkv_s,))
            shift_p.append(u_tail[:, 7])
            shift_s.append(u_sample.reshape(DB, DL, D)[:, DL - 1])
        h = _layer_tail(h, y, mix_out[i % 2], j, norm_ffn2[i], ffn2_gu, ffn2_dn, pp, ps, i, norm_ple[i],
                        ple_in_b, ple_gate_b, norm_final, final=(i == depth - 1))

    y_prompt, y_sample = h
    return (y_prompt.reshape(B, L, D), y_sample.reshape(DB, DL, D),
            m_prompt[0], m_prompt[1], wkv_p, jnp.stack(shift_p),
            m_sample[0], m_sample[1], wkv_s, jnp.stack(shift_s))
```

```python
import functools
import math

import jax
import jax.numpy as jnp
from jax import lax
from jax.experimental import pallas as pl
from jax.experimental.pallas import tpu as pltpu

F32 = jnp.float32
BF16 = jnp.bfloat16

NORM_EPS = 1e-6
M_NORM_EPS = 1e-5
R_GN_EPS = 64e-5

LANES = 128
M_HEADDIM = 64
M_STATE = 128
M_GROUPS = 8
M_CHUNK = 128
M_CONV_W = 4
R_HEADDIM = 64
M_PROMPT_CHUNKS_PER_STEP = 2
R_CHUNK = 64
R_PROMPT_CHUNKS_PER_STEP = 4
VMEM_LIMIT_BYTES = 56 * 1024 * 1024


def _cparams(n_axes):
    return pltpu.CompilerParams(dimension_semantics=("arbitrary",) * n_axes,
                                vmem_limit_bytes=VMEM_LIMIT_BYTES)


def _resident(shape):
    zeros = (0,) * len(shape)
    return pl.BlockSpec(shape, lambda *_: zeros, pipeline_mode=pl.Buffered(1))


def _resident_layer(stack, layer):
    index = (layer,) + (0,) * (stack.ndim - 1)
    return pl.BlockSpec((None,) + stack.shape[1:], lambda *_: index, pipeline_mode=pl.Buffered(1))


def _dot(a, b):
    return jnp.dot(a.astype(BF16), b.astype(BF16), preferred_element_type=F32)


def _dot_nt(a, b):
    return lax.dot_general(a.astype(BF16), b.astype(BF16), (((1,), (1,)), ((), ())),
                           preferred_element_type=F32)


def _split3(x):
    hi = x.astype(BF16)
    r1 = x - hi.astype(F32)
    mid = r1.astype(BF16)
    lo = (r1 - mid.astype(F32)).astype(BF16)
    return hi, mid, lo


def _dot_sel_l(sel3, x):
    return jnp.dot(sel3, jnp.concatenate(_split3(x), axis=0), preferred_element_type=F32)


def _dot_sel_r(x, sel3):
    return jnp.dot(jnp.concatenate(_split3(x), axis=1), sel3, preferred_element_type=F32)


def _within_sequence_selectors(rows, seq):
    t = jnp.arange(rows)[:, None]
    s = jnp.arange(rows)[None, :]
    same = (t // seq) == (s // seq)
    incl = jnp.logical_and(same, s <= t)
    return jnp.tile(incl.astype(BF16), (1, 3)), jnp.tile(same.astype(BF16), (1, 3))


def _rms(x, g):
    ms = jnp.mean(x * x, axis=-1, keepdims=True)
    return (x * lax.rsqrt(ms + NORM_EPS)) * g


def _silu(x):
    return x * jax.nn.sigmoid(x)


def _softplus(x):
    return jnp.maximum(x, 0.0) + jnp.log1p(jnp.exp(-jnp.abs(x)))


def _split_rows_specs(tm, width, n_first_blocks, lead=()):
    none = (None,) * len(lead)
    first = pl.BlockSpec(none + (tm, width), lambda i: lead + (jnp.minimum(i, n_first_blocks - 1), 0))
    second = pl.BlockSpec(none + (tm, width), lambda i: lead + (jnp.maximum(i - n_first_blocks, 0), 0))
    return first, second


def _pick_rows(first_ref, second_ref, n_first_blocks):
    return jnp.where(pl.program_id(0) < n_first_blocks, first_ref[...], second_ref[...])


def _ffn_rows(x, g_ref, wgu_ref, wd_ref, ub_ref, fc):
    ub_ref[...] = _rms(x, g_ref[...]).astype(BF16)
    F = wd_ref.shape[0]
    acc = x
    for c in range(F // fc):
        ub = ub_ref[...]
        gate = jnp.dot(ub, wgu_ref[:, c * fc:(c + 1) * fc], preferred_element_type=F32)
        up = jnp.dot(ub, wgu_ref[:, F + c * fc:F + (c + 1) * fc], preferred_element_type=F32)
        act = (_silu(gate) * up).astype(BF16)
        acc = acc + 0.5 * jnp.dot(act, wd_ref[c * fc:(c + 1) * fc, :], preferred_element_type=F32)
    return acc


def _ffn_kernel(*refs, fc, n_first_blocks):
    if n_first_blocks is None:
        h_ref, g_ref, wgu_ref, wd_ref, o_ref, ub_ref = refs
        x = h_ref[...]
    else:
        hp_ref, hs_ref, g_ref, wgu_ref, wd_ref, o_ref, ub_ref = refs
        x = _pick_rows(hp_ref, hs_ref, n_first_blocks)
    o_ref[...] = _ffn_rows(x, g_ref, wgu_ref, wd_ref, ub_ref, fc)


def _ffn(h, gain, w_gate_up, w_down, layer, *, tm=512, fc=256):
    parts = h if isinstance(h, tuple) else (h,)
    T = sum(p.shape[0] for p in parts)
    D = parts[0].shape[1]
    if len(parts) == 2:
        nfb = parts[0].shape[0] // tm
        h_specs = list(_split_rows_specs(tm, D, nfb))
    else:
        nfb = None
        h_specs = [pl.BlockSpec((tm, D), lambda i: (i, 0))]
    return pl.pallas_call(
        functools.partial(_ffn_kernel, fc=fc, n_first_blocks=nfb),
        out_shape=jax.ShapeDtypeStruct((T, D), F32),
        grid=(T // tm,),
        in_specs=h_specs + [_resident((1, D)), _resident_layer(w_gate_up, layer), _resident_layer(w_down, layer)],
        out_specs=pl.BlockSpec((tm, D), lambda i: (i, 0)),
        scratch_shapes=[pltpu.VMEM((tm, D), BF16)],
        compiler_params=_cparams(1),
        name="ffn",
    )(*parts, gain.reshape(1, D), w_gate_up, w_down)


def _tail_kernel(*refs, fc, final, n_first_blocks, y_pairs):
    (h_ref, y_ref, wm_ref, g2_ref, wgu_ref, wd_ref, pp_ref, ps_ref, gp_ref, wi_ref, wg_ref) = refs[:11]
    if final:
        gf_ref, op_ref, os_ref, ub_ref = refs[11:]
    else:
        o_ref, ub_ref = refs[11:]
    y = jnp.concatenate([y_ref[p] for p in range(y_pairs)], axis=1) if y_pairs else y_ref[...]
    x = h_ref[...] + jnp.dot(y, wm_ref[...], preferred_element_type=F32)
    x = _ffn_rows(x, g2_ref, wgu_ref, wd_ref, ub_ref, fc)
    gate = jax.nn.sigmoid(_dot(_rms(x, gp_ref[...]), wg_ref[...]))
    x = x + _dot(_pick_rows(pp_ref, ps_ref, n_first_blocks), wi_ref[...]) * gate
    if not final:
        o_ref[...] = x
        return
    x = _rms(x, gf_ref[...])
    i = pl.program_id(0)

    @pl.when(i < n_first_blocks)
    def _():
        op_ref[...] = x

    @pl.when(i >= n_first_blocks)
    def _():
        os_ref[...] = x


def _layer_tail(h, y, w_mix, mixer_layer, gain_ffn, w_gate_up, w_down, p_first, p_second, layer, gain_ple, w_in,
                w_gate, gain_final, *, final, tm=512, fc=256):
    T, D = h.shape
    P = p_first.shape[2]
    nfb = p_first.shape[1] // tm
    row = pl.BlockSpec((tm, D), lambda i: (i, 0))
    y_pairs = y.shape[0] if y.ndim == 3 else 0
    y_spec = (pl.BlockSpec((y_pairs, tm, LANES), lambda i: (0, i, 0)) if y_pairs
              else pl.BlockSpec((tm, y.shape[1]), lambda i: (i, 0)))
    ops = [h, y, w_mix, gain_ffn.reshape(1, D), w_gate_up, w_down,
           p_first, p_second, gain_ple.reshape(1, D), w_in, w_gate]
    specs = [row, y_spec, _resident_layer(w_mix, mixer_layer), _resident((1, D)),
             _resident_layer(w_gate_up, layer), _resident_layer(w_down, layer),
             *_split_rows_specs(tm, P, nfb, lead=(layer,)), _resident((1, D)),
             _resident_layer(w_in, layer), _resident_layer(w_gate, layer)]
    if final:
        ops.append(gain_final.reshape(1, D))
        specs.append(_resident((1, D)))
        out_shape = (jax.ShapeDtypeStruct((p_first.shape[1], D), F32), jax.ShapeDtypeStruct((p_second.shape[1], D), F32))
        out_specs = _split_rows_specs(tm, D, nfb)
    else:
        out_shape = jax.ShapeDtypeStruct((T, D), F32)
        out_specs = row
    return pl.pallas_call(
        functools.partial(_tail_kernel, fc=fc, final=final, n_first_blocks=nfb, y_pairs=y_pairs),
        out_shape=out_shape,
        grid=(T // tm,),
        in_specs=specs,
        out_specs=out_specs,
        scratch_shapes=[pltpu.VMEM((tm, D), BF16)],
        compiler_params=_cparams(1),
        name="layer_tail",
    )(*ops)


def _m_in_kernel(h_ref, g_ref, wz_ref, wx_ref, wdt_ref, z_ref, xbc_ref, dt_ref):
    u = _rms(h_ref[...], g_ref[...]).astype(BF16)
    z_ref[...] = jnp.dot(u, wz_ref[...], preferred_element_type=F32)
    xbc_ref[...] = jnp.dot(u, wx_ref[...], preferred_element_type=F32)
    dt_ref[...] = jnp.dot(u, wdt_ref[...], preferred_element_type=F32)


def _m_in_proj(h, gain, w_in, d_inner, conv_dim, n_heads, *, tm=512):
    T, D = h.shape
    wz = w_in[:, :d_inner].astype(BF16)
    wx = w_in[:, d_inner:d_inner + conv_dim].astype(BF16)
    wdt = jnp.pad(w_in[:, d_inner + conv_dim:], ((0, 0), (0, LANES - n_heads))).astype(BF16)
    return pl.pallas_call(
        _m_in_kernel,
        out_shape=(jax.ShapeDtypeStruct((T, d_inner), F32), jax.ShapeDtypeStruct((T, conv_dim), F32),
                   jax.ShapeDtypeStruct((T, LANES), F32)),
        grid=(T // tm,),
        in_specs=[pl.BlockSpec((tm, D), lambda i: (i, 0)), _resident((1, D)),
                  _resident((D, d_inner)), _resident((D, conv_dim)), _resident((D, LANES))],
        out_specs=(pl.BlockSpec((tm, d_inner), lambda i: (i, 0)), pl.BlockSpec((tm, conv_dim), lambda i: (i, 0)),
                   pl.BlockSpec((tm, LANES), lambda i: (i, 0))),
        compiler_params=_cparams(1),
        name="m_in_proj",
    )(h, gain.reshape(1, D), wz, wx, wdt)


def _head_expanders(n_heads):
    hh = jnp.arange(LANES)[:, None]
    e64 = (hh == (jnp.arange(n_heads * M_HEADDIM)[None, :] // M_HEADDIM)).astype(BF16)
    e128 = (hh == (jnp.arange(n_heads * LANES)[None, :] // LANES)).astype(BF16)
    return jnp.tile(e64, (3, 1)), jnp.tile(e128, (3, 1))


def _ssd_prompt_kernel(*refs, n_heads, n_aliased):
    (z_ref, xbc_ref, dt_ref, cw_ref, cb_ref, dtb_ref, alog_ref, dsk_ref, nw_ref,
     e64_ref, e128_ref, tri_ref, y_ref, ssm_ref, conv_ref, xpad_ref, s_ref) = refs[n_aliased:]
    c = pl.program_id(1)
    nc = pl.num_programs(1)
    Q = M_CHUNK
    d_inner = n_heads * M_HEADDIM
    gw = d_inner // M_GROUPS
    bc0 = d_inner
    cc0 = d_inner + M_GROUPS * M_STATE

    @pl.when(c == 0)
    def _():
        xpad_ref[:, 0:8, :] = jnp.zeros((xpad_ref.shape[0], 8, LANES), F32)
        s_ref[...] = jnp.zeros(s_ref.shape, F32)

    R = xbc_ref.shape[0]
    slabs = []
    for s in range(xpad_ref.shape[0]):
        lanes = slice(s * LANES, (s + 1) * LANES)
        xpad_ref[s, 8:8 + R, :] = xbc_ref[:, lanes]
        xc = None
        for k in range(M_CONV_W):
            lo = 8 - (M_CONV_W - 1) + k
            term = xpad_ref[s, lo:lo + R, :] * cw_ref[k:k + 1, lanes]
            xc = term if xc is None else xc + term
        slabs.append(_silu(xc + cb_ref[:, lanes]))
        xpad_ref[s, 0:8, :] = xpad_ref[s, R:R + 8, :]
    xs_all = jnp.concatenate(slabs, axis=1)

    ri = lax.broadcasted_iota(jnp.int32, (Q, Q), 0)
    ci = lax.broadcasted_iota(jnp.int32, (Q, Q), 1)
    causal = ri >= ci
    lane = lax.broadcasted_iota(jnp.int32, (Q, LANES), 1)
    first_head = lane < M_HEADDIM
    hpg = n_heads // M_GROUPS

    def chunk_groups(q):
        rows = slice(q * Q, (q + 1) * Q)
        xs = xs_all[rows]
        xh = xs[:, :d_inner]
        dt = _softplus(dt_ref[rows, :] + dtb_ref[...])
        dta = dt * (-jnp.exp(alog_ref[...]))
        acum = _dot_sel_l(tri_ref[...], dta)
        a_full = _dot_sel_r(acum, e64_ref[...])
        dt_full = _dot_sel_r(dt, e64_ref[...])
        a_col = _dot_sel_r(acum, e128_ref[...])
        acum_t = acum.T
        a_last = a_full[Q - 1:Q, :]
        exp_a = jnp.exp(a_full)
        xdt = xh * dt_full
        xw = xdt * jnp.exp(a_last - a_full)
        chunk_decay = jnp.exp(a_last)

        def group(g):
            bg = xs[:, bc0 + g * M_STATE: bc0 + (g + 1) * M_STATE]
            cg = xs[:, cc0 + g * M_STATE: cc0 + (g + 1) * M_STATE]
            cb = _dot_nt(cg, bg)
            sl = slice(g * gw, (g + 1) * gw)
            s_g = s_ref[:, sl]
            y_g = _dot(cg, s_g) * exp_a[:, sl]
            s_ref[:, sl] = s_g * chunk_decay[:, sl] + _dot(bg.T, xw[:, sl])
            yield
            parts = []
            for pr in range(hpg // 2):
                h0 = g * hpg + 2 * pr
                xp = xdt[:, h0 * M_HEADDIM: h0 * M_HEADDIM + LANES]
                ys = []
                for hh in (h0, h0 + 1):
                    seg = a_col[:, hh * LANES:(hh + 1) * LANES] - acum_t[hh:hh + 1, :]
                    m = cb * jnp.exp(jnp.where(causal, seg, -jnp.inf))
                    ys.append(_dot(m, xp))
                parts.append(jnp.where(first_head, ys[0], ys[1]))
                yield
            y_g = y_g + jnp.concatenate(parts, axis=1) + dsk_ref[:, sl] * xh[:, sl]
            y_g = y_g * _silu(z_ref[rows, sl])
            y_g = y_g * lax.rsqrt(jnp.mean(y_g * y_g, axis=-1, keepdims=True) + M_NORM_EPS)
            y_ref[rows, sl] = (y_g * nw_ref[:, sl]).astype(y_ref.dtype)

        return [group(g) for g in range(M_GROUPS)]

    _interleave(gen for q in range(R // Q) for gen in chunk_groups(q))

    @pl.when(c == nc - 1)
    def _():
        ssm_ref[0] = s_ref[...].T.reshape(n_heads, M_HEADDIM, M_STATE)
        conv_ref[0] = xbc_ref[R - (M_CONV_W - 1):R, :]


def _alias_prev(prev, first_out):
    prev = tuple(prev or ())
    specs = [pl.BlockSpec(memory_space=pl.ANY)] * len(prev)
    return prev, specs, {k: first_out + k for k in range(len(prev))}


def _ssd_prompt(z, xbc, dt, prm, B, L, n_heads, layer, n_layers, prev):
    T, d_inner = z.shape
    conv_dim = xbc.shape[1]
    Q = M_CHUNK
    R = M_PROMPT_CHUNKS_PER_STEP * Q
    nc = L // R
    row = lambda b, c: (b * nc + c, 0)
    prev, prev_specs, aliases = _alias_prev(prev, first_out=1)
    kern = functools.partial(_ssd_prompt_kernel, n_heads=n_heads, n_aliased=len(prev))
    return pl.pallas_call(
        kern,
        out_shape=(jax.ShapeDtypeStruct((T, d_inner), BF16),
                   jax.ShapeDtypeStruct((n_layers, B, n_heads, M_HEADDIM, M_STATE), F32),
                   jax.ShapeDtypeStruct((n_layers, B, M_CONV_W - 1, conv_dim), F32)),
        grid=(B, nc),
        in_specs=prev_specs + [
            pl.BlockSpec((R, d_inner), row), pl.BlockSpec((R, conv_dim), row), pl.BlockSpec((R, LANES), row),
            _resident((M_CONV_W, conv_dim)), _resident((1, conv_dim)),
            _resident((1, LANES)), _resident((1, LANES)), _resident((1, d_inner)), _resident((1, d_inner)),
            _resident((3 * LANES, d_inner)), _resident((3 * LANES, n_heads * LANES)), _resident((Q, 3 * Q))],
        out_specs=(pl.BlockSpec((R, d_inner), row),
                   pl.BlockSpec((None, 1, n_heads, M_HEADDIM, M_STATE), lambda b, c: (layer, b, 0, 0, 0)),
                   pl.BlockSpec((None, 1, M_CONV_W - 1, conv_dim), lambda b, c: (layer, b, 0, 0))),
        scratch_shapes=[pltpu.VMEM((conv_dim // LANES, R + 8, LANES), F32), pltpu.VMEM((M_STATE, d_inner), F32)],
        input_output_aliases=aliases,
        compiler_params=_cparams(2),
        name="ssd_prompt",
    )(*prev, z, xbc, dt, prm["conv_w"], prm["conv_b"], prm["dt_bias"], prm["a_log"], prm["d_full"], prm["norm_w"],
      prm["e64"], prm["e128"], prm["tri"])


def _ssd_sample_kernel(*refs, n_heads, seq, nb, n_aliased):
    (z_ref, xbc_ref, dt_ref, s0_ref, c0_ref, cw_ref, cb_ref, dtb_ref, alog_ref,
     dsk_ref, nw_ref, e64_ref, e128_ref, rowsel_ref, y_ref, ssm_ref, conv_ref, hist_ref) = refs[n_aliased:]
    d_inner = n_heads * M_HEADDIM
    gw = d_inner // M_GROUPS
    bc0 = d_inner
    cc0 = d_inner + M_GROUPS * M_STATE
    hpg = n_heads // M_GROUPS
    rows = nb * seq
    row = lax.broadcasted_iota(jnp.int32, (rows, 1), 0)
    tok = jnp.bitwise_and(row, seq - 1)
    seq_of_row = jnp.right_shift(row, int(math.log2(seq)))
    seq_of_col = jnp.right_shift(lax.broadcasted_iota(jnp.int32, (1, rows), 1), int(math.log2(seq)))
    eye = (lax.broadcasted_iota(jnp.int32, (LANES, LANES), 0)
           == lax.broadcasted_iota(jnp.int32, (LANES, LANES), 1)).astype(BF16)
    top_head = lax.broadcasted_iota(jnp.int32, (LANES, 1), 0) < M_HEADDIM

    @pl.when(pl.program_id(0) == 0)
    def _():
        hist_ref[...] = jnp.zeros(hist_ref.shape, F32)

    x = xbc_ref[...]
    for s in range(1, M_CONV_W):
        for bi in range(nb):
            hist_ref[s - 1, bi * seq:bi * seq + s, :] = c0_ref[bi, M_CONV_W - 1 - s:M_CONV_W - 1, :]
    xc = None
    for k in range(M_CONV_W):
        s = M_CONV_W - 1 - k
        tap = x if s == 0 else jnp.where(tok >= s, pltpu.roll(x, s, 0), hist_ref[s - 1])
        term = tap * cw_ref[k:k + 1, :]
        xc = term if xc is None else xc + term
    xs = _silu(xc + cb_ref[...])
    xh = xs[:, :d_inner]
    for bi in range(nb):
        conv_ref[bi] = xbc_ref[(bi + 1) * seq - (M_CONV_W - 1):(bi + 1) * seq, :]

    dt = _softplus(dt_ref[...] + dtb_ref[...])
    acum = dt * (-jnp.exp(alog_ref[...]))
    s = 1
    while s < seq:
        acum = acum + jnp.where(tok >= s, pltpu.roll(acum, s, 0), 0.0)
        s *= 2
    a_full = _dot_sel_r(acum, e64_ref[...])
    dt_full = _dot_sel_r(dt, e64_ref[...])
    a_col = _dot_sel_r(acum, e128_ref[...])
    exp_a = jnp.exp(a_full)
    xdt = xh * dt_full
    packed = jnp.concatenate([a_full, xdt, xs[:, bc0:cc0]], axis=1)
    from_tok = [_dot_sel_l(rowsel_ref[j], packed) for j in range(seq)]
    aux = xdt * jnp.exp(from_tok[seq - 1][:, :d_inner] - a_full)

    y = dsk_ref[...] * xh
    for j in range(seq):
        a_j = from_tok[j][:, :d_inner]
        wj = jnp.where(tok >= j, jnp.exp(a_full - a_j), 0.0) * from_tok[j][:, d_inner:2 * d_inner]
        parts = []
        for g in range(M_GROUPS):
            cgb = (xs[:, cc0 + g * M_STATE: cc0 + (g + 1) * M_STATE]
                   * from_tok[j][:, 2 * d_inner + g * M_STATE: 2 * d_inner + (g + 1) * M_STATE])
            parts.append(jnp.sum(cgb, axis=-1, keepdims=True) * wj[:, g * gw:(g + 1) * gw])
        y = y + jnp.concatenate(parts, axis=1)

    inter = [None] * (n_heads // 2)

    def head_pair(pr):
        g = (2 * pr) // hpg
        cg = xs[:, cc0 + g * M_STATE: cc0 + (g + 1) * M_STATE]
        bg = xs[:, bc0 + g * M_STATE: bc0 + (g + 1) * M_STATE]
        aux_t = _dot_nt(eye, aux[:, pr * LANES:(pr + 1) * LANES])
        h0 = [s0_ref[bi, 2 * pr:2 * pr + 2].reshape(2 * M_HEADDIM, M_STATE) for bi in range(nb)]
        reads = [_dot_nt(cg, h) for h in h0]
        yield
        acc = None
        for bi in range(nb):
            t = jnp.where(seq_of_row == bi, reads[bi], 0.0)
            acc = t if acc is None else acc + t
        inter[pr] = acc
        upd = [_dot(jnp.where(seq_of_col == bi, aux_t, 0.0), bg) for bi in range(nb)]
        yield
        for bi in range(nb):
            last = (bi + 1) * seq - 1
            dec = jnp.where(top_head, jnp.exp(a_col[last:last + 1, (2 * pr) * LANES:(2 * pr + 1) * LANES]),
                            jnp.exp(a_col[last:last + 1, (2 * pr + 1) * LANES:(2 * pr + 2) * LANES]))
            ssm_ref[bi, 2 * pr:2 * pr + 2] = (h0[bi] * dec + upd[bi]).reshape(2, M_HEADDIM, M_STATE)

    _interleave(head_pair(pr) for pr in range(n_heads // 2))
    y = y + jnp.concatenate(inter, axis=1) * exp_a

    y = y * _silu(z_ref[...])
    outs = []
    for g in range(M_GROUPS):
        yg = y[:, g * gw:(g + 1) * gw]
        outs.append(yg * lax.rsqrt(jnp.mean(yg * yg, axis=-1, keepdims=True) + M_NORM_EPS))
    y_ref[...] = (jnp.concatenate(outs, axis=1) * nw_ref[...]).astype(y_ref.dtype)


def _ssd_sample(y_all, z, xbc, dt, s0, c0, prm, row0, seq, n_heads, layer, prev, *, nb=8):
    T, d_inner = z.shape
    conv_dim = xbc.shape[1]
    DB = s0.shape[1]
    assert seq & (seq - 1) == 0 and seq >= M_CONV_W - 1
    rows = nb * seq
    r = jnp.arange(rows)
    rowsel = jnp.stack([jnp.tile((r[None, :] == ((r // seq) * seq + j)[:, None]).astype(BF16), (1, 3))
                        for j in range(seq)])
    blk0 = row0 // rows
    row = lambda i: (blk0 + i, 0)
    s_spec = pl.BlockSpec((None, nb, n_heads, M_HEADDIM, M_STATE), lambda i: (layer, i, 0, 0, 0))
    c_spec = pl.BlockSpec((None, nb, M_CONV_W - 1, conv_dim), lambda i: (layer, i, 0, 0))
    aliased, alias_specs, aliases = _alias_prev((y_all,) + tuple(prev or ()), first_out=0)
    kern = functools.partial(_ssd_sample_kernel, n_heads=n_heads, seq=seq, nb=nb, n_aliased=len(aliased))
    return pl.pallas_call(
        kern,
        out_shape=(jax.ShapeDtypeStruct((T, d_inner), y_all.dtype),
                   jax.ShapeDtypeStruct(s0.shape, F32), jax.ShapeDtypeStruct(c0.shape, F32)),
        grid=(DB // nb,),
        in_specs=alias_specs + [
            pl.BlockSpec((rows, d_inner), row), pl.BlockSpec((rows, conv_dim), row),
            pl.BlockSpec((rows, LANES), row), s_spec, c_spec,
            _resident((M_CONV_W, conv_dim)), _resident((1, conv_dim)), _resident((1, LANES)),
            _resident((1, LANES)), _resident((1, d_inner)), _resident((1, d_inner)),
            _resident((3 * LANES, d_inner)), _resident((3 * LANES, n_heads * LANES)),
            _resident((seq, rows, 3 * rows))],
        out_specs=(pl.BlockSpec((rows, d_inner), row), s_spec, c_spec),
        scratch_shapes=[pltpu.VMEM((M_CONV_W - 1, rows, conv_dim), F32)],
        input_output_aliases=aliases,
        compiler_params=_cparams(1),
        name="ssd_sample",
    )(*aliased, z, xbc, dt, s0, c0, prm["conv_w"], prm["conv_b"], prm["dt_bias"], prm["a_log"], prm["d_full"],
      prm["norm_w"], prm["e64"], prm["e128"], rowsel)


def _r_proj_kernel(*refs, tm, blocks_per_seq, n_prompt_blocks, sample_seq, has_vres):
    if has_vres:
        (h_ref, g_ref, pshift_ref, srows_ref, mu_ref, wr_ref, wk_ref, wv_ref, w1_ref, w2_ref, a1_ref, a2_ref,
         g1_ref, g2_ref, w0_ref, a0_ref, kk_ref, ka_ref, vf_ref, v0_ref, v1_ref, v2_ref,
         r_o, k_o, v_o, kkr_o, a_o, w_o, g_o, up_o, us_o, sc_ref, xx_ref) = refs
    else:
        (h_ref, g_ref, pshift_ref, srows_ref, mu_ref, wr_ref, wk_ref, wv_ref, w1_ref, w2_ref, a1_ref, a2_ref,
         g1_ref, g2_ref, w0_ref, a0_ref, kk_ref, ka_ref,
         r_o, k_o, v_o, kkr_o, a_o, w_o, g_o, up_o, us_o, sc_ref, xx_ref) = refs
    i = pl.program_id(0)
    u = _rms(h_ref[...], g_ref[...])
    sc_ref[8:8 + tm, :] = u

    @pl.when(jnp.logical_and(i % blocks_per_seq == blocks_per_seq - 1, i < n_prompt_blocks))
    def _():
        up_o[0] = u[tm - 8:tm, :]

    @pl.when(i >= n_prompt_blocks)
    def _():
        us_o[...] = u

    @pl.when(jnp.logical_and(i % blocks_per_seq == 0, i < n_prompt_blocks))
    def _():
        sc_ref[7:8, :] = pshift_ref[pl.ds(i // blocks_per_seq, 1), :]

    prev = sc_ref[7:7 + tm, :]
    row = lax.broadcasted_iota(jnp.int32, (tm, 1), 0)
    use_state = jnp.logical_and(i >= n_prompt_blocks, jnp.bitwise_and(row, sample_seq - 1) == 0)
    prev = jnp.where(use_state, srows_ref[...], prev)
    sc_ref[7:8, :] = sc_ref[7 + tm:8 + tm, :]

    xx_ref[...] = prev - u
    npair = r_o.shape[0]

    def rows(r0, n):
        u_h = sc_ref[8 + r0:8 + r0 + n, :]
        xx_h = xx_ref[r0:r0 + n, :]

        def mix(m):
            return u_h + xx_h * mu_ref[m:m + 1, :]

        xv = mix(3)
        r = _dot(mix(0), wr_ref[...])
        k = _dot(mix(2), wk_ref[...])
        v = _dot(xv, wv_ref[...])
        yield
        lw = _dot(mix(1), w1_ref[...])
        la = _dot(mix(4), a1_ref[...])
        lg = _dot(mix(5), g1_ref[...])
        lv = _dot(xv, v1_ref[...]) if has_vres else None
        yield
        w = -_softplus(-(w0_ref[...] + _dot(jnp.tanh(lw), w2_ref[...]))) - 0.5
        a = jax.nn.sigmoid(a0_ref[...] + _dot(la, a2_ref[...]))
        g = _dot(jax.nn.sigmoid(lg), g2_ref[...])
        if has_vres:
            vf = jnp.concatenate([vf_ref[p, r0:r0 + n, :] for p in range(npair)], axis=1)
            v = v + (vf - v) * jax.nn.sigmoid(v0_ref[...] + _dot(lv, v2_ref[...]))
        yield
        kkr = k * kk_ref[...]
        k = k * (1.0 + (a - 1.0) * ka_ref[...])
        for p in range(npair):
            sl = slice(p * LANES, (p + 1) * LANES)
            r_o[p, r0:r0 + n, :] = r[:, sl]
            k_o[p, r0:r0 + n, :] = k[:, sl]
            v_o[p, r0:r0 + n, :] = v[:, sl]
            kkr_o[p, r0:r0 + n, :] = kkr[:, sl]
            a_o[p, r0:r0 + n, :] = a[:, sl]
            w_o[p, r0:r0 + n, :] = w[:, sl]
            g_o[p, r0:r0 + n, :] = g[:, sl]

    n_groups = tm // 128
    _interleave((rows(q * (tm // n_groups), tm // n_groups) for q in range(n_groups)))


def _pad_cols(w, n):
    return jnp.pad(w, ((0, 0), (0, n - w.shape[1])))


def _pad_rows(w, n):
    return jnp.pad(w, ((0, n - w.shape[0]), (0, 0)))


def _round_up(n, m):
    return -(-n // m) * m


def _r_proj(h, gain, pshift, srows, rp, v_first, L, n_prompt_rows, sample_seq, layer, *, tm=256):
    T, D = h.shape
    npair = D // LANES
    has_vres = v_first is not None
    row2 = lambda i: (i, 0)
    npb = n_prompt_rows // tm
    lw = _round_up(rp["w1"].shape[1], LANES)
    la = _round_up(rp["a1"].shape[1], LANES)
    lg = _round_up(rp["g1"].shape[1], LANES)
    ops = [h, gain.reshape(1, D), pshift, srows, rp["mu"],
           rp["wr"], rp["wk"], rp["wv"],
           _pad_cols(rp["w1"], lw).astype(BF16), _pad_rows(rp["w2"], lw).astype(BF16),
           _pad_cols(rp["a1"], la).astype(BF16), _pad_rows(rp["a2"], la).astype(BF16),
           _pad_cols(rp["g1"], lg).astype(BF16), _pad_rows(rp["g2"], lg).astype(BF16),
           rp["w0"].reshape(1, D), rp["a0"].reshape(1, D), rp["k_k"].reshape(1, D), rp["k_a"].reshape(1, D)]
    specs = [pl.BlockSpec((tm, D), row2), _resident((1, D)), _resident(pshift.shape),
             pl.BlockSpec((tm, D), lambda i: (jnp.maximum(i - npb, 0), 0)), _resident((6, D)),
             _resident_layer(rp["wr"], layer), _resident_layer(rp["wk"], layer), _resident_layer(rp["wv"], layer),
             _resident((D, lw)), _resident((lw, D)), _resident((D, la)), _resident((la, D)),
             _resident((D, lg)), _resident((lg, D)),
             _resident((1, D)), _resident((1, D)), _resident((1, D)), _resident((1, D))]
    pair_spec = pl.BlockSpec((npair, tm, LANES), lambda i: (0, i, 0))
    if has_vres:
        lv = _round_up(rp["v1"].shape[1], LANES)
        ops += [v_first, rp["v0"].reshape(1, D), _pad_cols(rp["v1"], lv).astype(BF16),
                _pad_rows(rp["v2"], lv).astype(BF16)]
        specs += [pair_spec, _resident((1, D)), _resident((D, lv)), _resident((lv, D))]
    pair_shape = jax.ShapeDtypeStruct((npair, T, LANES), F32)
    bps = L // tm
    n_seq = n_prompt_rows // L
    kern = functools.partial(_r_proj_kernel, tm=tm, blocks_per_seq=bps, n_prompt_blocks=npb,
                             sample_seq=sample_seq, has_vres=has_vres)
    return pl.pallas_call(
        kern,
        out_shape=(pair_shape,) * 7 + (jax.ShapeDtypeStruct((n_seq, 8, D), F32),
                                       jax.ShapeDtypeStruct((T - n_prompt_rows, D), F32)),
        grid=(T // tm,),
        in_specs=specs,
        out_specs=(pair_spec,) * 7 + (
            pl.BlockSpec((1, 8, D), lambda i: (jnp.minimum(i // bps, n_seq - 1), 0, 0)),
            pl.BlockSpec((tm, D), lambda i: (jnp.maximum(i - npb, 0), 0))),
        scratch_shapes=[pltpu.VMEM((tm + 8, D), F32), pltpu.VMEM((tm, D), F32)],
        compiler_params=_cparams(1),
        name="r_proj",
    )(*ops)


def _wkv_pair(r, k, v, kkr, a, w, g, rk, gnw, gnb, incl3, same3, states, lseq):
    C = r.shape[0]
    ns = C // lseq
    lane = lax.broadcasted_iota(jnp.int32, (1, LANES), 1)
    m0 = lane < R_HEADDIM

    def headsum(x):
        s0 = jnp.sum(jnp.where(m0, x, 0.0), axis=-1, keepdims=True)
        s1 = jnp.sum(jnp.where(m0, 0.0, x), axis=-1, keepdims=True)
        return jnp.where(m0, s0, s1)

    kk = kkr * lax.rsqrt(jnp.maximum(headsum(kkr * kkr), 1e-24))
    alpha = -kk
    beta = kk * a
    logd = -jnp.exp(w)

    ti = lax.broadcasted_iota(jnp.int32, (C, C), 0)
    si = lax.broadcasted_iota(jnp.int32, (C, C), 1)
    sh = int(math.log2(lseq))
    same = jnp.right_shift(ti, sh) == jnp.right_shift(si, sh)
    incl = jnp.logical_and(same, si <= ti)
    strict = jnp.logical_and(same, si < ti)
    lg = _dot_sel_l(incl3, logd)
    ltot = _dot_sel_l(same3, logd)
    yield
    g_in = jnp.exp(lg)
    ginv = jnp.exp(-lg)
    gend = jnp.exp(ltot - lg)
    w0 = alpha * jnp.exp(lg - logd)
    rg = r * g_in
    bi = beta * ginv
    ki = k * ginv
    wr = jnp.concatenate([w0, rg], axis=0)
    rsel = jnp.right_shift(jnp.bitwise_and(lax.broadcasted_iota(jnp.int32, (2 * C, 1), 0), C - 1), sh)

    xy0 = None
    for s in range(ns):
        t = _dot_nt(wr, states[s])
        if ns > 1:
            t = jnp.where(rsel == s, t, 0.0)
        xy0 = t if xy0 is None else xy0 + t
    x0 = xy0[:C]
    y0 = xy0[C:]
    yield

    eye = (ti == si).astype(F32)
    heads = (m0, jnp.logical_not(m0))
    wrm = [jnp.where(mh, wr, 0.0) for mh in heads]
    gb = [_dot_nt(x, bi) for x in wrm]
    gk = [_dot_nt(x, ki) for x in wrm]
    yield
    a_ab = [jnp.where(strict, x[:C], 0.0) for x in gb]
    a_ak = [jnp.where(strict, x[:C], 0.0) for x in gk]
    b_rb = [jnp.where(incl, x[C:], 0.0) for x in gb]
    b_rk = [jnp.where(incl, x[C:], 0.0) for x in gk]
    tinv = [eye + x for x in a_ab]
    pw = a_ab
    for _ in range(int(math.log2(lseq)) - 1):
        pw = [_dot(x, x) for x in pw]
        yield
        tinv = [t + _dot(t, x) for t, x in zip(tinv, pw)]
    xh = [x0 + _dot(x, v) for x in a_ak]
    yv = [_dot(x, v) for x in b_rk]
    yield
    uh = [_dot(t, x) for t, x in zip(tinv, xh)]
    yield
    yh = [y0 + _dot(b, x) + c for b, x, c in zip(b_rb, uh, yv)]
    u = jnp.where(m0, uh[0], uh[1])
    y = jnp.where(m0, yh[0], yh[1])
    yield

    uvt = jnp.concatenate([u, v], axis=0).T
    bkd = jnp.concatenate([beta * gend, k * gend], axis=0)
    bi_r = lax.broadcasted_iota(jnp.int32, (LANES, LANES), 0) >= R_HEADDIM
    bi_c = lax.broadcasted_iota(jnp.int32, (LANES, LANES), 1) >= R_HEADDIM
    same_head = bi_r == bi_c
    new_states = []
    for s in range(ns):
        rhs = bkd if ns == 1 else jnp.where(rsel == s, bkd, 0.0)
        ds = jnp.where(same_head, _dot(uvt, rhs), 0.0)
        new_states.append(states[s] * jnp.exp(ltot[s * lseq:s * lseq + 1, :]) + ds)
    yield

    mean = headsum(y) * (1.0 / R_HEADDIM)
    d = y - mean
    var = headsum(d * d) * (1.0 / R_HEADDIM)
    yn = d * lax.rsqrt(var + R_GN_EPS) * gnw + gnb
    out = (yn + headsum(r * k * rk) * v) * g
    return out, new_states


def _wkv_pair_chunk(r, k, v, kkr, a, w, g, rk, gnw, gnb, incl3, state_ref, y_ref):
    C = r.shape[0]
    H = 2 * C
    lane = lax.broadcasted_iota(jnp.int32, (1, LANES), 1)
    m0 = lane < R_HEADDIM

    def headsum(x):
        s0 = jnp.sum(jnp.where(m0, x, 0.0), axis=-1, keepdims=True)
        s1 = jnp.sum(jnp.where(m0, 0.0, x), axis=-1, keepdims=True)
        return jnp.where(m0, s0, s1)

    def stack(x):
        return jnp.concatenate([jnp.where(m0, x, 0.0), jnp.where(m0, 0.0, x)], axis=0)

    kk = kkr * lax.rsqrt(jnp.maximum(headsum(kkr * kkr), 1e-24))
    beta = kk * a
    logd = -jnp.exp(w)
    lg = _dot_sel_l(incl3, logd)
    yield
    ltot = lg[C - 1:C, :]
    ginv = jnp.exp(-lg)
    gend = jnp.exp(ltot - lg)
    w0s = stack(-kk * jnp.exp(lg - logd))
    rgs = stack(r * jnp.exp(lg))
    vs = stack(v)
    bds = stack(beta * gend)
    kds = stack(k * gend)
    gram = _dot_nt(jnp.concatenate([w0s, rgs], axis=0),
                   jnp.concatenate([stack(beta * ginv), stack(k * ginv)], axis=0))
    yield
    tr = jnp.bitwise_and(lax.broadcasted_iota(jnp.int32, (H, H), 0), C - 1)
    sc = jnp.bitwise_and(lax.broadcasted_iota(jnp.int32, (H, H), 1), C - 1)
    strict = sc < tr
    incl = sc <= tr
    a_ab = jnp.where(strict, gram[:H, :H], 0.0)
    a_ak = jnp.where(strict, gram[:H, H:], 0.0)
    b_rb = jnp.where(incl, gram[H:, :H], 0.0)
    b_rk = jnp.where(incl, gram[H:, H:], 0.0)
    eye = (lax.broadcasted_iota(jnp.int32, (H, H), 0) == lax.broadcasted_iota(jnp.int32, (H, H), 1)).astype(F32)
    tinv = eye + a_ab
    pw = _dot(a_ab, a_ab)
    av = _dot(a_ak, vs)
    bv = _dot(b_rk, vs)
    yield
    for _ in range(int(math.log2(C)) - 2):
        tinv = tinv + _dot(tinv, pw)
        pw = _dot(pw, pw)
        yield
    tinv = tinv + _dot(tinv, pw)
    yield
    tw = _dot(tinv, jnp.concatenate([w0s, av], axis=1))
    yield
    br = _dot(b_rb, tw)
    m_t = _dot(tw[:, :LANES].T, bds)
    z = _dot(jnp.concatenate([tw[:, LANES:], vs], axis=0).T, jnp.concatenate([bds, kds], axis=0))
    yield
    r_t = rgs + br[:, :LANES]
    y_t = br[:, LANES:] + bv
    state = state_ref[...]
    ys = _dot_nt(r_t, state) + y_t
    state_ref[...] = state * jnp.exp(ltot) + _dot(state, m_t) + z
    yield
    y = ys[:C] + ys[C:]
    mean = headsum(y) * (1.0 / R_HEADDIM)
    d = y - mean
    var = headsum(d * d) * (1.0 / R_HEADDIM)
    yn = d * lax.rsqrt(var + R_GN_EPS) * gnw + gnb
    y_ref[...] = ((yn + headsum(r * k * rk) * v) * g).astype(y_ref.dtype)


def _interleave(generators):
    live = list(generators)
    while live:
        nxt = []
        for gen in live:
            try:
                next(gen)
                nxt.append(gen)
            except StopIteration:
                pass
        live = nxt


def _wkv_prompt_kernel(*refs, steps_per_seq, n_aliased):
    (r_ref, k_ref, v_ref, kkr_ref, a_ref, w_ref, g_ref, rk_ref, gnw_ref, gnb_ref, incl_ref,
     y_ref, s_out_ref, s_ref) = refs[n_aliased:]
    c = pl.program_id(0) % steps_per_seq
    npair = r_ref.shape[0]
    C = R_CHUNK

    @pl.when(c == 0)
    def _():
        s_ref[...] = jnp.zeros(s_ref.shape, F32)

    def chunk(p, q):
        rows = slice(q * C, (q + 1) * C)
        return _wkv_pair_chunk(r_ref[p, rows], k_ref[p, rows], v_ref[p, rows], kkr_ref[p, rows], a_ref[p, rows],
                               w_ref[p, rows], g_ref[p, rows], rk_ref[p], gnw_ref[p], gnb_ref[p], incl_ref[...],
                               s_ref.at[p], y_ref.at[p, pl.ds(q * C, C)])

    _interleave(chunk(p, q) for q in range(r_ref.shape[1] // C) for p in range(npair))

    @pl.when(c == steps_per_seq - 1)
    def _():
        for p in range(npair):
            s = s_ref[p]
            s_out_ref[0, 2 * p:2 * p + 2] = (s[:, :R_HEADDIM] + s[:, R_HEADDIM:]).reshape(2, R_HEADDIM, R_HEADDIM)


def _wkv_prompt(streams, rk, gnw, gnb, B, L, layer, n_layers, prev):
    npair, T, _ = streams[0].shape
    C = R_CHUNK
    rows = R_PROMPT_CHUNKS_PER_STEP * C
    cps = L // rows
    blk = pl.BlockSpec((npair, rows, LANES), lambda i: (0, i, 0))
    vec = _resident((npair, 1, LANES))
    prev, prev_specs, aliases = _alias_prev(prev, first_out=1)
    return pl.pallas_call(
        functools.partial(_wkv_prompt_kernel, steps_per_seq=cps, n_aliased=len(prev)),
        out_shape=(jax.ShapeDtypeStruct((npair, T, LANES), BF16),
                   jax.ShapeDtypeStruct((n_layers, B, 2 * npair, R_HEADDIM, R_HEADDIM), F32)),
        grid=(B * cps,),
        in_specs=prev_specs + [blk] * 7 + [vec] * 3 + [_resident((C, 3 * C))],
        out_specs=(blk, pl.BlockSpec((None, 1, 2 * npair, R_HEADDIM, R_HEADDIM),
                                     lambda i: (layer, i // cps, 0, 0, 0))),
        scratch_shapes=[pltpu.VMEM((npair, LANES, LANES), F32)],
        input_output_aliases=aliases,
        compiler_params=_cparams(1),
        name="wkv_prompt",
    )(*prev, *streams, rk, gnw, gnb, _within_sequence_selectors(C, C)[0])


def _wkv_sample_kernel(*refs, seq, n_aliased):
    (r_ref, k_ref, v_ref, kkr_ref, a_ref, w_ref, g_ref, rk_ref, gnw_ref, gnb_ref, incl_ref, same_ref,
     s0_ref, y_ref, s_out_ref) = refs[n_aliased:]
    npair = r_ref.shape[0]
    ns = R_CHUNK // seq
    lane = lax.broadcasted_iota(jnp.int32, (LANES, LANES), 1) >= R_HEADDIM
    rowh = lax.broadcasted_iota(jnp.int32, (LANES, LANES), 0) >= R_HEADDIM
    same_head = lane == rowh

    def pair(p):
        states = []
        for s in range(ns):
            s2 = s0_ref[s, 2 * p:2 * p + 2].reshape(LANES, R_HEADDIM)
            states.append(jnp.where(same_head, jnp.concatenate([s2, s2], axis=1), 0.0))
        out, new = yield from _wkv_pair(r_ref[p], k_ref[p], v_ref[p], kkr_ref[p], a_ref[p], w_ref[p], g_ref[p],
                                        rk_ref[p], gnw_ref[p], gnb_ref[p], incl_ref[...], same_ref[...],
                                        states, seq)
        y_ref[p] = out.astype(y_ref.dtype)
        for s in range(ns):
            s_out_ref[s, 2 * p:2 * p + 2] = (new[s][:, :R_HEADDIM] + new[s][:, R_HEADDIM:]).reshape(
                2, R_HEADDIM, R_HEADDIM)

    _interleave(pair(p) for p in range(npair))


def _wkv_sample(y_all, streams, rk, gnw, gnb, s0, row0, seq, layer, prev):
    npair, T, _ = streams[0].shape
    C = R_CHUNK
    ns = C // seq
    DB = s0.shape[1]
    blk0 = row0 // C
    blk = pl.BlockSpec((npair, C, LANES), lambda i: (0, blk0 + i, 0))
    vec = _resident((npair, 1, LANES))
    sblk = pl.BlockSpec((None, ns, 2 * npair, R_HEADDIM, R_HEADDIM), lambda i: (layer, i, 0, 0, 0))
    aliased, alias_specs, aliases = _alias_prev((y_all,) + tuple(prev or ()), first_out=0)
    return pl.pallas_call(
        functools.partial(_wkv_sample_kernel, seq=seq, n_aliased=len(aliased)),
        out_shape=(jax.ShapeDtypeStruct((npair, T, LANES), BF16), jax.ShapeDtypeStruct(s0.shape, F32)),
        grid=(DB // ns,),
        in_specs=alias_specs + [blk] * 7 + [vec] * 3 + [_resident((C, 3 * C))] * 2 + [sblk],
        out_specs=(blk, sblk),
        input_output_aliases=aliases,
        compiler_params=_cparams(1),
        name="wkv_sample",
    )(*aliased, *streams, rk, gnw, gnb, *_within_sequence_selectors(C, seq), s0)


def kernel(x_prompt, x_sample, p_prompt, p_sample, state_ssm, state_conv, state_wkv, state_shift, norm_ffn1, ffn1_gate_up, ffn1_down, norm_mix, norm_ffn2, ffn2_gate_up, ffn2_down, norm_ple, ple_in, ple_gate, norm_final, m_in_proj, m_conv_w, m_conv_b, m_dt_bias, m_A_log, m_D, m_norm, m_out_proj, r_mu, r_wr, r_wk, r_wv, r_wo, r_w0, r_w1, r_w2, r_a0, r_a1, r_a2, r_g1, r_g2, r_k_k, r_k_a, r_r_k, r_gn_w, r_gn_b, r_v0, r_v1, r_v2):
    B, L, D = x_prompt.shape
    DB, DL, _ = x_sample.shape
    depth = norm_ffn1.shape[0]
    TP, TS = B * L, DB * DL
    n_heads = m_A_log.shape[1]
    d_inner = n_heads * M_HEADDIM
    conv_dim = m_conv_w.shape[2]
    npair = D // LANES

    h = (x_prompt.reshape(TP, D), x_sample.reshape(TS, D))
    pp = p_prompt.reshape(depth, TP, -1)
    ps = p_sample.reshape(depth, TS, -1)
    e64, e128 = _head_expanders(n_heads)
    n_mamba = state_ssm.shape[0]
    n_rwkv = state_wkv.shape[0]

    tri = _within_sequence_selectors(M_CHUNK, M_CHUNK)[0]
    ffn1_gu, ffn1_dn = ffn1_gate_up.astype(BF16), ffn1_down.astype(BF16)
    ffn2_gu, ffn2_dn = ffn2_gate_up.astype(BF16), ffn2_down.astype(BF16)
    ple_in_b, ple_gate_b = ple_in.astype(BF16), ple_gate.astype(BF16)
    mix_out = (m_out_proj.astype(BF16), r_wo.astype(BF16))
    r_wr_b, r_wk_b, r_wv_b = r_wr.astype(BF16), r_wk.astype(BF16), r_wv.astype(BF16)

    m_prompt = m_sample = wkv_p = wkv_s = None
    shift_p, shift_s = [], []
    v_first = None
    for i in range(depth):
        j = i // 2
        h = _ffn(h, norm_ffn1[i], ffn1_gu, ffn1_dn, i)
        if i % 2 == 0:
            prm = dict(conv_w=m_conv_w[j], conv_b=m_conv_b[j].reshape(1, conv_dim),
                       dt_bias=jnp.pad(m_dt_bias[j], (0, LANES - n_heads)).reshape(1, LANES),
                       a_log=jnp.pad(m_A_log[j], (0, LANES - n_heads)).reshape(1, LANES),
                       d_full=jnp.repeat(m_D[j], M_HEADDIM).reshape(1, d_inner),
                       norm_w=m_norm[j].reshape(1, d_inner), e64=e64, e128=e128, tri=tri)
            z, xbc, dt = _m_in_proj(h, norm_mix[i], m_in_proj[j], d_inner, conv_dim, n_heads)
            y, *m_prompt = _ssd_prompt(z, xbc, dt, prm, B, L, n_heads, j, n_mamba, m_prompt)
            y, *m_sample = _ssd_sample(y, z, xbc, dt, state_ssm, state_conv, prm, TP, DL, n_heads, j, m_sample)
        else:
            rp = dict(mu=r_mu[j], wr=r_wr_b, wk=r_wk_b, wv=r_wv_b, w0=r_w0[j], w1=r_w1[j], w2=r_w2[j],
                      a0=r_a0[j], a1=r_a1[j], a2=r_a2[j], g1=r_g1[j], g2=r_g2[j], k_k=r_k_k[j], k_a=r_k_a[j])
            if j > 0:
                rp.update(v0=r_v0[j - 1], v1=r_v1[j - 1], v2=r_v2[j - 1])
            pshift = jnp.zeros((B, D), F32)
            srows = jnp.zeros((DB, DL, D), F32).at[:, 0].set(state_shift[j]).reshape(TS, D)
            outs = _r_proj(h, norm_mix[i], pshift, srows, rp, v_first if j > 0 else None, L, TP, DL, j)
            streams, u_tail, u_sample = outs[:7], outs[7], outs[8]
            if j == 0:
                v_first = streams[2]
            rk = r_r_k[j].reshape(npair, 1, LANES)
            gnw = r_gn_w[j].reshape(npair, 1, LANES)
            gnb = r_gn_b[j].reshape(npair, 1, LANES)
            y, wkv_p = _wkv_prompt(streams, rk, gnw, gnb, B, L, j, n_rwkv, None if wkv_p is None else (wkv_p,))
            y, wkv_s = _wkv_sample(y, streams, rk, gnw, gnb, state_wkv, TP, DL, j, None if wkv_s is None else (wkv_s,))
            shift_p.append(u_tail[:, 7])
            shift_s.append(u_sample.reshape(DB, DL, D)[:, DL - 1])
        h = _layer_tail(h, y, mix_out[i % 2], j, norm_ffn2[i], ffn2_gu, ffn2_dn, pp, ps, i, norm_ple[i],
                        ple_in_b, ple_gate_b, norm_final, final=(i == depth - 1))

    y_prompt, y_sample = h
    return (y_prompt.reshape(B, L, D), y_sample.reshape(DB, DL, D),
            m_prompt[0], m_prompt[1], wkv_p, jnp.stack(shift_p),
            m_sample[0], m_sample[1], wkv_s, jnp.stack(shift_s))
```
